```python
import jax, jax.numpy as jnp
from jax import lax
import numpy as np

D_MODEL = 1024
BATCH = 8
SEQ = 2048
DEPTH = 2

CTX_LEN = 256
GRID_W = 64
N_Q_HEADS = 8
N_KV_HEADS = 2
HEAD_DIM = 64
Q_BLOCK = 128
ROPE_THETA = 10000.0
D_CONV = 512
CONV_K = 31
D_RNN = 512
RNN_BLOCKS = 8
RNN_CONV_K = 4
LRU_C = 8.0
D_FF = 3072
FFN_CONV_K = 3
N_BRANCH = 3
EPS = 1e-6

D_Q = N_Q_HEADS * HEAD_DIM
D_KV = N_KV_HEADS * HEAD_DIM
Q_GROUP = N_Q_HEADS // N_KV_HEADS
IN_SPLITS = (D_Q, D_KV, D_KV, 2 * D_CONV, D_RNN, D_RNN, N_BRANCH * D_MODEL)
D_IN = 5888

kernel_name = 'hybrid_gqa_conformer_rglru_dit_block'


def rms_norm(x, g):
    xf = x.astype(jnp.float32)
    y = xf * lax.rsqrt(jnp.mean(xf * xf, axis=-1, keepdims=True) + EPS)
    return (y * g.astype(jnp.float32)).astype(x.dtype)


def layer_norm(x, g, b):
    xf = x.astype(jnp.float32)
    mu = jnp.mean(xf, axis=-1, keepdims=True)
    var = jnp.mean(jnp.square(xf - mu), axis=-1, keepdims=True)
    y = (xf - mu) * lax.rsqrt(var + EPS)
    return (y * g.astype(jnp.float32) + b.astype(jnp.float32)).astype(x.dtype)


def modulate(h, shift, scale):
    return h * (1.0 + scale) + shift


def dwconv(x, w, b, pad):
    C = x.shape[-1]
    y = lax.conv_general_dilated(x, w[:, None, :].astype(x.dtype), window_strides=(1,), padding=[pad],
                                 dimension_numbers=('NWC', 'WIO', 'NWC'), feature_group_count=C)
    return y + b.astype(x.dtype)


def split_in(z):
    offs = [int(o) for o in np.cumsum(IN_SPLITS)[:-1]]
    return jnp.split(z, offs, axis=-1)


def axial_rope_tables(n):
    rows = n // GRID_W
    row = jnp.repeat(jnp.arange(rows), GRID_W).astype(jnp.float32)
    col = jnp.tile(jnp.arange(GRID_W), rows).astype(jnp.float32)
    n_freq = HEAD_DIM // 4
    freq = ROPE_THETA ** (-jnp.arange(n_freq, dtype=jnp.float32) / n_freq)
    ang = jnp.concatenate([row[:, None] * freq, col[:, None] * freq], axis=-1)
    return jnp.cos(ang), jnp.sin(ang)


def apply_rope(x, cos, sin):
    x1, x2 = jnp.split(x.astype(jnp.float32), 2, axis=-1)
    c = cos[None, :, None, :]
    s = sin[None, :, None, :]
    return jnp.concatenate([x1 * c - x2 * s, x1 * s + x2 * c], axis=-1).astype(x.dtype)


def heads(z, n_heads, gain):
    B, n = z.shape[:2]
    return rms_norm(z.reshape(B, n, n_heads, HEAD_DIM), gain)


def gqa_attend(q, k, v):
    B, n = q.shape[:2]
    qg = q.reshape(B, n, N_KV_HEADS, Q_GROUP, HEAD_DIM)
    s = jnp.einsum('bqkgd,bmkd->bkgqm', qg, k).astype(jnp.float32) * (HEAD_DIM ** -0.5)
    p = jax.nn.softmax(s, axis=-1).astype(v.dtype)
    o = jnp.einsum('bkgqm,bmkd->bqkgd', p, v)
    return o.reshape(B, n, D_Q)


def blocked_attention(q, k_all, v_all):
    B, N = q.shape[:2]
    nb = N // Q_BLOCK
    qb = q.reshape(B, nb, Q_BLOCK, N_Q_HEADS, HEAD_DIM).swapaxes(0, 1)
    o = lax.map(lambda qi: gqa_attend(qi, k_all, v_all), qb)
    return o.swapaxes(0, 1).reshape(B, N, D_Q)


def conformer_conv(z, p):
    val, gate = jnp.split(z, 2, axis=-1)
    u = val * jax.nn.sigmoid(gate)
    u = dwconv(u, p['conv_dw'], p['conv_dw_b'], (CONV_K // 2, CONV_K // 2))
    u = layer_norm(u, p['conv_ln_g'], p['conv_ln_b'])
    return jax.nn.silu(u) @ p['w_conv_out']


def _lin_combine(left, right):
    a_l, b_l = left
    a_r, b_r = right
    return a_l * a_r, a_r * b_l + b_r


def rglru_direction(xs, h0, conv_w, conv_b, wa, ba, wx, bx, lam):
    B, n = xs.shape[:2]
    xc = dwconv(xs, conv_w, conv_b, (RNN_CONV_K - 1, 0))
    xb = xc.reshape(B, n, RNN_BLOCKS, D_RNN // RNN_BLOCKS)
    r = jax.nn.sigmoid((jnp.einsum('bnhi,hij->bnhj', xb, wa).reshape(B, n, D_RNN) + ba).astype(jnp.float32))
    i = jax.nn.sigmoid((jnp.einsum('bnhi,hij->bnhj', xb, wx).reshape(B, n, D_RNN) + bx).astype(jnp.float32))
    log_a = LRU_C * r * jax.nn.log_sigmoid(lam.astype(jnp.float32))
    a = jnp.exp(log_a)
    b = jnp.sqrt(-jnp.expm1(2.0 * log_a)) * (i * xc.astype(jnp.float32))
    b = b.at[:, 0].add(a[:, 0] * h0)
    _, h = lax.associative_scan(_lin_combine, (a, b), axis=1)
    return h


def rglru_branch(x_lat, x_ctx, gate_lat, gate_ctx, p, need_ctx):
    B = x_lat.shape[0]
    h0 = jnp.zeros((B, D_RNN), jnp.float32)
    def direction(xs, init, d):
        return rglru_direction(xs, init, p['rnn_conv_w'][d], p['rnn_conv_b'][d], p['rnn_wa'][d],
                               p['rnn_ba'][d], p['rnn_wx'][d], p['rnn_bx'][d], p['rnn_lambda'][d])
    hc_f = direction(x_ctx, h0, 0)
    hl_f = direction(x_lat, hc_f[:, -1], 0)
    hc_b = direction(x_ctx[:, ::-1], h0, 1)
    hl_b = direction(x_lat[:, ::-1], hc_b[:, -1], 1)[:, ::-1]
    y_lat = ((hl_f + hl_b).astype(x_lat.dtype) * jax.nn.gelu(gate_lat)) @ p['w_rnn_out']
    if not need_ctx:
        return y_lat, None
    y_ctx = ((hc_f + hc_b[:, ::-1]).astype(x_ctx.dtype) * jax.nn.gelu(gate_ctx)) @ p['w_rnn_out']
    return y_lat, y_ctx


def merge(zg, attn_o, conv_o, rnn_o, w_out):
    g = jax.nn.sigmoid(zg.astype(jnp.float32)).astype(zg.dtype)
    g = g.reshape(zg.shape[:-1] + (N_BRANCH, D_MODEL))
    m = g[..., 0, :] * attn_o + g[..., 1, :] * conv_o + g[..., 2, :] * rnn_o
    return m @ w_out


def token_mixer(h_lat, h_ctx, p, cos, sin, need_ctx):
    B, N, _ = h_lat.shape
    M = h_ctx.shape[1]
    zq, zk, zv, zc, zx, zr, zg = split_in(h_lat @ p['w_in'])
    cq, ck, cv, cc, cx, cr, cg = split_in(h_ctx @ p['w_in'])
    q = apply_rope(heads(zq, N_Q_HEADS, p['q_norm']), cos, sin)
    k = apply_rope(heads(zk, N_KV_HEADS, p['k_norm']), cos, sin)
    v = zv.reshape(B, N, N_KV_HEADS, HEAD_DIM)
    kc = heads(ck, N_KV_HEADS, p['k_norm'])
    vc = cv.reshape(B, M, N_KV_HEADS, HEAD_DIM)
    k_all = jnp.concatenate([kc, k], axis=1)
    v_all = jnp.concatenate([vc, v], axis=1)
    attn_lat = blocked_attention(q, k_all, v_all) @ p['w_attn_out']
    conv_lat = conformer_conv(zc, p)
    rnn_lat, rnn_ctx = rglru_branch(zx, cx, zr, cr, p, need_ctx)
    o_lat = merge(zg, attn_lat, conv_lat, rnn_lat, p['w_out'])
    if not need_ctx:
        return o_lat, None
    qc = heads(cq, N_Q_HEADS, p['q_norm'])
    attn_ctx = gqa_attend(qc, kc, vc) @ p['w_attn_out']
    conv_ctx = conformer_conv(cc, p)
    o_ctx = merge(cg, attn_ctx, conv_ctx, rnn_ctx, p['w_out'])
    return o_lat, o_ctx


def conv_ffn(h, up, dw, dw_b, down):
    u = h @ up
    u = dwconv(u, dw, dw_b, (FFN_CONV_K // 2, FFN_CONV_K // 2))
    g, val = jnp.split(u, 2, axis=-1)
    return (jax.nn.gelu(g) * val) @ down


def setup_inputs(seed: int = 0) -> dict:
    key = jax.random.key(seed)
    keys = jax.random.split(key, 32)
    L = DEPTH
    f32 = jnp.float32
    def nrm(i, shape, scale):
        return jax.random.normal(keys[i], shape, f32) * scale
    def gain(i, shape):
        return 1.0 + nrm(i, shape, 0.05)
    d_blk = D_RNN // RNN_BLOCKS
    u = jax.random.uniform(keys[25], (L, 2, D_RNN), f32, 0.9, 0.999)
    s = u ** (1.0 / LRU_C)
    rnn_lambda = jnp.log(s) - jnp.log1p(-s)
    return {
        'x': nrm(0, (BATCH, SEQ, D_MODEL), 1.0),
        'c': nrm(1, (BATCH, D_MODEL), 1.0),
        'ctx': nrm(2, (BATCH, CTX_LEN, D_MODEL), 1.0),
        'c_ctx': nrm(3, (D_MODEL,), 1.0),
        'w_mod': nrm(4, (L, D_MODEL, 6 * D_MODEL), 0.5 * D_MODEL ** -0.5),
        'b_mod': nrm(5, (L, 6 * D_MODEL), 0.02),
        'norm_pre_mix': gain(6, (L, D_MODEL)),
        'norm_post_mix': gain(7, (L, D_MODEL)),
        'norm_pre_ffn': gain(8, (L, D_MODEL)),
        'norm_post_ffn': gain(9, (L, D_MODEL)),
        'w_in': nrm(10, (L, D_MODEL, D_IN), D_MODEL ** -0.5),
        'q_norm': gain(11, (L, HEAD_DIM)),
        'k_norm': gain(12, (L, HEAD_DIM)),
        'w_attn_out': nrm(13, (L, D_Q, D_MODEL), D_Q ** -0.5),
        'conv_dw': nrm(14, (L, CONV_K, D_CONV), CONV_K ** -0.5),
        'conv_dw_b': nrm(15, (L, D_CONV), 0.01),
        'conv_ln_g': gain(16, (L, D_CONV)),
        'conv_ln_b': nrm(17, (L, D_CONV), 0.01),
        'w_conv_out': nrm(18, (L, D_CONV, D_MODEL), D_CONV ** -0.5),
        'rnn_conv_w': nrm(19, (L, 2, RNN_CONV_K, D_RNN), RNN_CONV_K ** -0.5),
        'rnn_conv_b': nrm(20, (L, 2, D_RNN), 0.01),
        'rnn_wa': nrm(21, (L, 2, RNN_BLOCKS, d_blk, d_blk), d_blk ** -0.5),
        'rnn_ba': nrm(22, (L, 2, D_RNN), 0.01),
        'rnn_wx': nrm(23, (L, 2, RNN_BLOCKS, d_blk, d_blk), d_blk ** -0.5),
        'rnn_bx': nrm(24, (L, 2, D_RNN), 0.01),
        'rnn_lambda': rnn_lambda,
        'w_rnn_out': nrm(26, (L, D_RNN, D_MODEL), D_RNN ** -0.5),
        'w_out': nrm(27, (L, D_MODEL, D_MODEL), D_MODEL ** -0.5),
        'ffn_up': nrm(28, (L, D_MODEL, 2 * D_FF), D_MODEL ** -0.5),
        'ffn_dw': nrm(29, (L, FFN_CONV_K, 2 * D_FF), FFN_CONV_K ** -0.5),
        'ffn_dw_b': nrm(30, (L, 2 * D_FF), 0.01),
        'ffn_down': nrm(31, (L, D_FF, D_MODEL), D_FF ** -0.5),
    }


def reference(x, c, ctx, c_ctx, w_mod, b_mod, norm_pre_mix, norm_post_mix, norm_pre_ffn, norm_post_ffn,
              w_in, q_norm, k_norm, w_attn_out, conv_dw, conv_dw_b, conv_ln_g, conv_ln_b, w_conv_out,
              rnn_conv_w, rnn_conv_b, rnn_wa, rnn_ba, rnn_wx, rnn_bx, rnn_lambda, w_rnn_out, w_out,
              ffn_up, ffn_dw, ffn_dw_b, ffn_down):
    N = x.shape[1]
    cos, sin = axial_rope_tables(N)
    for l in range(DEPTH):
        need_ctx = l < DEPTH - 1
        sh1, sc1, g1, sh2, sc2, g2 = jnp.split((jax.nn.silu(c) @ w_mod[l] + b_mod[l])[:, None, :], 6, axis=-1)
        csh1, csc1, cg1, csh2, csc2, cg2 = jnp.split(jax.nn.silu(c_ctx) @ w_mod[l] + b_mod[l], 6, axis=-1)
        p = dict(w_in=w_in[l], q_norm=q_norm[l], k_norm=k_norm[l], w_attn_out=w_attn_out[l],
                 conv_dw=conv_dw[l], conv_dw_b=conv_dw_b[l], conv_ln_g=conv_ln_g[l], conv_ln_b=conv_ln_b[l],
                 w_conv_out=w_conv_out[l], rnn_conv_w=rnn_conv_w[l], rnn_conv_b=rnn_conv_b[l],
                 rnn_wa=rnn_wa[l], rnn_ba=rnn_ba[l], rnn_wx=rnn_wx[l], rnn_bx=rnn_bx[l],
                 rnn_lambda=rnn_lambda[l], w_rnn_out=w_rnn_out[l], w_out=w_out[l])
        h_lat = modulate(rms_norm(x, norm_pre_mix[l]), sh1, sc1)
        h_ctx = modulate(rms_norm(ctx, norm_pre_mix[l]), csh1, csc1)
        o_lat, o_ctx = token_mixer(h_lat, h_ctx, p, cos, sin, need_ctx)
        x = x + g1 * rms_norm(o_lat, norm_post_mix[l])
        h = modulate(rms_norm(x, norm_pre_ffn[l]), sh2, sc2)
        x = x + g2 * rms_norm(conv_ffn(h, ffn_up[l], ffn_dw[l], ffn_dw_b[l], ffn_down[l]), norm_post_ffn[l])
        if need_ctx:
            ctx = ctx + cg1 * rms_norm(o_ctx, norm_post_mix[l])
            hc = modulate(rms_norm(ctx, norm_pre_ffn[l]), csh2, csc2)
            ctx = ctx + cg2 * rms_norm(conv_ffn(hc, ffn_up[l], ffn_dw[l], ffn_dw_b[l], ffn_down[l]), norm_post_ffn[l])
    return x
```

```python
import functools

import jax
import jax.numpy as jnp
from jax import lax
from jax.experimental import pallas as pl
from jax.experimental.pallas import tpu as pltpu

F32 = jnp.float32
BF16 = jnp.bfloat16

EPS = 1e-6
HEAD_DIM = 64
N_Q_HEADS = 8
N_KV_HEADS = 2
Q_GROUP = N_Q_HEADS // N_KV_HEADS
D_Q = N_Q_HEADS * HEAD_DIM
D_KV = N_KV_HEADS * HEAD_DIM
D_CONV = 512
CONV_K = 31
D_RNN = 512
RNN_CONV_K = 4
LRU_C = 8.0
D_FF = 3072
GRID_W = 64
ROPE_THETA = 10000.0

LANES = 128
SUBLANES = 8
VMEM_LIMIT = 56 * 1024 * 1024

_OFF_Q = 0
_OFF_KV = D_Q
_OFF_CONV = _OFF_KV + 2 * D_KV
_OFF_ZX = _OFF_CONV + 2 * D_CONV
_OFF_ZR = _OFF_ZX + D_RNN
_OFF_GATE = _OFF_ZR + D_RNN


def _const_spec(shape):
    zeros = (0,) * len(shape)
    return pl.BlockSpec(shape, lambda *_: zeros, pipeline_mode=pl.Buffered(1))


def _params(*sem):
    return pltpu.CompilerParams(dimension_semantics=sem, vmem_limit_bytes=VMEM_LIMIT)


def _rms(x, g):
    return x * lax.rsqrt(jnp.mean(x * x, axis=-1, keepdims=True) + EPS) * g


def _mm(a, b):
    return jnp.dot(a, b, preferred_element_type=F32)


def _modulated(x, g, shift, scale):
    return _rms(x, g) * (1.0 + scale) + shift


def _mod_kernel(c_ref, w_ref, b_ref, o_ref):
    c = c_ref[...]
    a = (c * jax.nn.sigmoid(c)).astype(BF16)
    o_ref[...] = _mm(a, w_ref[...]) + b_ref[...]


def _mod_call(cc, w_mod, b_mod):
    L, D, D6 = w_mod.shape
    R = cc.shape[0]
    bn = D6 // 4
    return pl.pallas_call(
        _mod_kernel,
        grid=(L, D6 // bn),
        in_specs=[
            pl.BlockSpec((R, D), lambda l, j: (0, 0)),
            pl.BlockSpec((None, D, bn), lambda l, j: (l, 0, j)),
            pl.BlockSpec((None, 1, bn), lambda l, j: (l, 0, j)),
        ],
        out_specs=pl.BlockSpec((None, R, bn), lambda l, j: (l, 0, j)),
        out_shape=jax.ShapeDtypeStruct((L, R, D6), F32),
        compiler_params=_params("arbitrary", "arbitrary"),
        name="adaln_mod",
    )(cc, w_mod, b_mod)


def _head_rms(z, gain, bd):
    sq = z * z
    hi = sq.astype(BF16)
    lo = (sq - hi.astype(F32)).astype(BF16)
    ms = (_mm(hi, bd) + _mm(lo, bd)) * (1.0 / HEAD_DIM)
    return z * lax.rsqrt(ms + EPS) * gain


def _rope(x, c, s):
    rows, width = x.shape
    lane = lax.broadcasted_iota(jnp.int32, (rows, LANES), 1)
    first_half = (lane & (HEAD_DIM // 2)) == 0
    cols = []
    for j in range(width // LANES):
        sl = slice(j * LANES, (j + 1) * LANES)
        xc = x[:, sl]
        partner = jnp.where(first_half,
                            pltpu.roll(xc, LANES - HEAD_DIM // 2, 1),
                            pltpu.roll(xc, HEAD_DIM // 2, 1))
        cols.append(xc * c[:, sl] + partner * s[:, sl])
    return jnp.concatenate(cols, axis=1) if len(cols) > 1 else cols[0]


def _inproj_kernel(*refs, rope):
    if rope:
        (x_ref, sh_ref, sc_ref, g_ref, w_ref, qg_ref, kg_ref, bd_ref, cos_ref, sin_ref,
         q_ref, k_ref, v_ref, u_ref, zx_ref, zr_ref) = refs
    else:
        (x_ref, sh_ref, sc_ref, g_ref, w_ref, qg_ref, kg_ref, bd_ref,
         q_ref, k_ref, v_ref, u_ref, zx_ref, zr_ref) = refs
    h = _modulated(x_ref[...], g_ref[...], sh_ref[...], sc_ref[...]).astype(BF16)

    zq = _mm(h, w_ref[:, _OFF_Q:_OFF_Q + D_Q])
    qn = _head_rms(zq, qg_ref[...], bd_ref[...])
    if rope:
        qn = _rope(qn, cos_ref[...], sin_ref[...])
    q_ref[...] = (qn * (HEAD_DIM ** -0.5)).astype(BF16)

    zkv = _mm(h, w_ref[:, _OFF_KV:_OFF_KV + 2 * D_KV])
    kn = _head_rms(zkv[:, :D_KV], kg_ref[...], bd_ref[:D_KV, :D_KV])
    if rope:
        kn = _rope(kn, cos_ref[:, :D_KV], sin_ref[:, :D_KV])
    k_ref[...] = kn
    v_ref[...] = zkv[:, D_KV:]

    zc = _mm(h, w_ref[:, _OFF_CONV:_OFF_CONV + 2 * D_CONV])
    u_ref[...] = zc[:, :D_CONV] * jax.nn.sigmoid(zc[:, D_CONV:])

    zx_ref[...] = _mm(h, w_ref[:, _OFF_ZX:_OFF_ZX + D_RNN])
    zr_ref[...] = _mm(h, w_ref[:, _OFF_ZR:_OFF_ZR + D_RNN])


def _inproj_call(x, shift, scale, gain, w, qg, kg, bd, cos, sin, *, tm):
    B, T, D = x.shape
    rope = cos is not None
    tok = lambda w_: pl.BlockSpec((None, tm, w_), lambda b, t: (b, t, 0))
    vec = pl.BlockSpec((None, 1, D), lambda b, t: (b, 0, 0))
    in_specs = [tok(D), vec, vec, _const_spec((1, D)), _const_spec(w.shape),
                _const_spec((1, D_Q)), _const_spec((1, D_KV)), _const_spec(bd.shape)]
    args = [x, shift, scale, gain, w, qg, kg, bd]
    if rope:
        tab = pl.BlockSpec((tm, D_Q), lambda b, t: (t, 0))
        in_specs += [tab, tab]
        args += [cos, sin]
    out_shape = [jax.ShapeDtypeStruct((B, T, D_Q), BF16),
                 jax.ShapeDtypeStruct((B, T, D_KV), F32),
                 jax.ShapeDtypeStruct((B, T, D_KV), F32),
                 jax.ShapeDtypeStruct((B, T, D_CONV), F32),
                 jax.ShapeDtypeStruct((B, T, D_RNN), F32),
                 jax.ShapeDtypeStruct((B, T, D_RNN), F32)]
    out_specs = [tok(D_Q), tok(D_KV), tok(D_KV), tok(D_CONV), tok(D_RNN), tok(D_RNN)]
    return pl.pallas_call(
        functools.partial(_inproj_kernel, rope=rope),
        grid=(B, T // tm),
        in_specs=in_specs, out_specs=out_specs, out_shape=out_shape,
        compiler_params=_params("arbitrary", "arbitrary"),
        name="inproj_rope" if rope else "inproj",
    )(*args)


def _attn_kernel(*refs, seg_lens):
    nseg = len(seg_lens)
    q_ref = refs[0]
    kv_refs = refs[1:1 + 2 * nseg]
    o_ref = refs[1 + 2 * nseg]
    kbd_ref, vbd_ref, p_ref = refs[2 + 2 * nseg:]
    m_tot = sum(seg_lens)
    g = pl.program_id(1)

    @pl.when(pl.program_id(2) == 0)
    def _():
        off = 0
        for i, ms in enumerate(seg_lens):
            lane_half = lax.broadcasted_iota(jnp.int32, (ms, LANES), 1) // HEAD_DIM
            zero = jnp.zeros((ms, LANES), BF16)
            for src, dst in ((kv_refs[2 * i], kbd_ref), (kv_refs[2 * i + 1], vbd_ref)):
                a = src[...]
                dup = jnp.where(lane_half == g, a, pltpu.roll(a, HEAD_DIM, 1))
                for h in range(Q_GROUP):
                    half, col = h % 2, h // 2
                    ah = jnp.where(lane_half == half, dup, 0.0).astype(BF16)
                    rows = slice(h * m_tot + off, h * m_tot + off + ms)
                    dst[rows, col * LANES:(col + 1) * LANES] = ah
                    dst[rows, (1 - col) * LANES:(2 - col) * LANES] = zero
            off += ms

    q = q_ref[...]
    inv = []
    for h in range(Q_GROUP):
        cols = slice(h * m_tot, (h + 1) * m_tot)
        s = lax.dot_general(q, kbd_ref[cols, :], (((1,), (1,)), ((), ())),
                            preferred_element_type=F32)
        p = jnp.exp(s - jnp.max(s, axis=-1, keepdims=True))
        inv.append(1.0 / jnp.sum(p, axis=-1, keepdims=True))
        p_ref[:, cols] = p.astype(BF16)
    o = _mm(p_ref[...], vbd_ref[...])
    head = lax.broadcasted_iota(jnp.int32, o.shape, 1) // HEAD_DIM
    scale = jnp.where(head == 0, inv[0], jnp.where(head == 1, inv[1],
                      jnp.where(head == 2, inv[2], inv[3])))
    o_ref[...] = (o * scale).astype(BF16)


def _attn_call(q, segs, *, tq):
    B, T, _ = q.shape
    seg_lens = tuple(k.shape[1] for k, _ in segs)
    m_tot = sum(seg_lens)
    gw = Q_GROUP * HEAD_DIM
    in_specs = [pl.BlockSpec((None, tq, gw), lambda b, g, t: (b, t, g))]
    args = [q]
    for k, v in segs:
        spec = pl.BlockSpec((None, k.shape[1], D_KV), lambda b, g, t: (b, 0, 0))
        in_specs += [spec, spec]
        args += [k, v]
    return pl.pallas_call(
        functools.partial(_attn_kernel, seg_lens=seg_lens),
        grid=(B, N_KV_HEADS, T // tq),
        in_specs=in_specs,
        out_specs=pl.BlockSpec((None, tq, gw), lambda b, g, t: (b, t, g)),
        out_shape=jax.ShapeDtypeStruct((B, T, D_Q), BF16),
        scratch_shapes=[pltpu.VMEM((Q_GROUP * m_tot, gw), BF16),
                        pltpu.VMEM((Q_GROUP * m_tot, gw), BF16),
                        pltpu.VMEM((tq, Q_GROUP * m_tot), BF16)],
        compiler_params=_params("arbitrary", "arbitrary", "arbitrary"),
        name="gqa_attention",
    )(*args)


_CONV_HALO = 16
_CONV_ROWS = 32


def _halo_specs(tm, halo, width, n_rows):
    per = tm // halo
    last = n_rows // halo - 1
    prev = pl.BlockSpec((None, halo, width), lambda b, t: (b, jnp.maximum(t * per - 1, 0), 0))
    nxt = pl.BlockSpec((None, halo, width), lambda b, t: (b, jnp.minimum((t + 1) * per, last), 0))
    return prev, nxt


def _conv_kernel(up_ref, u_ref, un_ref, w_ref, b_ref, lg_ref, lb_ref, o_ref, pad_ref):
    tm = u_ref.shape[0]
    t = pl.program_id(1)
    pad_ref[0:_CONV_HALO, :] = up_ref[...] * jnp.where(t > 0, 1.0, 0.0)
    pad_ref[_CONV_HALO:_CONV_HALO + tm, :] = u_ref[...]
    pad_ref[_CONV_HALO + tm:, :] = un_ref[...] * jnp.where(t < pl.num_programs(1) - 1, 1.0, 0.0)
    first = _CONV_HALO - CONV_K // 2
    for c in range(tm // _CONV_ROWS):
        r = c * _CONV_ROWS
        acc = jnp.zeros((_CONV_ROWS, D_CONV), F32)
        for j in range(CONV_K):
            acc = acc + pad_ref[r + first + j:r + first + j + _CONV_ROWS, :] * w_ref[j:j + 1, :]
        acc = acc + b_ref[...]
        mu = jnp.mean(acc, axis=-1, keepdims=True)
        cen = acc - mu
        var = jnp.mean(cen * cen, axis=-1, keepdims=True)
        y = cen * lax.rsqrt(var + EPS) * lg_ref[...] + lb_ref[...]
        o_ref[r:r + _CONV_ROWS, :] = (y * jax.nn.sigmoid(y)).astype(BF16)


def _conv_call(u, w, b, lg, lb, *, tm):
    B, T, C = u.shape
    prev, nxt = _halo_specs(tm, _CONV_HALO, C, T)
    tile = pl.BlockSpec((None, tm, C), lambda b_, t: (b_, t, 0))
    return pl.pallas_call(
        _conv_kernel,
        grid=(B, T // tm),
        in_specs=[prev, tile, nxt,
                  _const_spec(w.shape), _const_spec((1, C)), _const_spec((1, C)), _const_spec((1, C))],
        out_specs=tile,
        out_shape=jax.ShapeDtypeStruct((B, T, C), BF16),
        scratch_shapes=[pltpu.VMEM((tm + 2 * _CONV_HALO, C), F32)],
        compiler_params=_params("arbitrary", "arbitrary"),
        name="conformer_conv",
    )(u, u, u, w, b, lg, lb)


def _rnn_kernel(zp_ref, zx_ref, zn_ref, h0_ref, cw_ref, cb_ref, w_ref, b_ref, lam_ref, h_ref, hl_ref,
                xp_ref, a_ref, bb_ref, hc_ref):
    chunk = zx_ref.shape[0]
    d = pl.program_id(1)
    c = pl.program_id(2)
    n_chunks = pl.num_programs(2)
    cc = c + d * (n_chunks - 1 - 2 * c)

    @pl.when(c == 0)
    def _():
        hc_ref[...] = h0_ref[...]

    xp_ref[0:SUBLANES, :] = zp_ref[...] * jnp.where(cc > 0, 1.0, 0.0)
    xp_ref[SUBLANES:SUBLANES + chunk, :] = zx_ref[...]
    xp_ref[SUBLANES + chunk:, :] = zn_ref[...] * jnp.where(cc < n_chunks - 1, 1.0, 0.0)

    lam = lam_ref[...]
    log_sig = jnp.minimum(lam, 0.0) - jnp.log1p(jnp.exp(-jnp.abs(lam)))
    c_log = LRU_C * log_sig
    row = lax.broadcasted_iota(jnp.int32, (SUBLANES, D_RNN), 0)

    def run(reverse):
        xc = jnp.zeros((chunk, D_RNN), F32)
        for j in range(RNN_CONV_K):
            off = SUBLANES + ((RNN_CONV_K - 1 - j) if reverse else (j - RNN_CONV_K + 1))
            xc = xc + xp_ref[off:off + chunk, :] * cw_ref[j:j + 1, :]
        xc = xc + cb_ref[...]
        gates = _mm(xc.astype(BF16), w_ref[...]) + b_ref[...]
        r_gate = jax.nn.sigmoid(gates[:, :D_RNN])
        i_gate = jax.nn.sigmoid(gates[:, D_RNN:])
        a = jnp.exp(c_log * r_gate)
        a_ref[...] = a
        bb_ref[...] = jnp.sqrt((1.0 - a) * (1.0 + a)) * (i_gate * xc)

        def tile_body(k, hc):
            kk = (chunk // SUBLANES - 1 - k) if reverse else k
            t0 = pl.multiple_of(kk * SUBLANES, SUBLANES)
            at = a_ref[pl.ds(t0, SUBLANES), :]
            bt = bb_ref[pl.ds(t0, SUBLANES), :]
            for s in (1, 2, 4):
                if reverse:
                    ok = row < SUBLANES - s
                    shift = SUBLANES - s
                else:
                    ok = row >= s
                    shift = s
                a_sh = jnp.where(ok, pltpu.roll(at, shift, 0), 1.0)
                b_sh = jnp.where(ok, pltpu.roll(bt, shift, 0), 0.0)
                bt = at * b_sh + bt
                at = at * a_sh
            ht = at * hc + bt
            h_ref[pl.ds(t0, SUBLANES), :] = ht
            edge = ht[0:1, :] if reverse else ht[SUBLANES - 1:SUBLANES, :]
            return jnp.broadcast_to(edge, (SUBLANES, D_RNN))

        hc = lax.fori_loop(0, chunk // SUBLANES, tile_body, hc_ref[...], unroll=4)
        hc_ref[...] = hc
        hl_ref[...] = hc

    pl.when(d == 0)(lambda: run(False))
    pl.when(d == 1)(lambda: run(True))


def _rnn_call(zx, h0, cw, cb, w, b, lam, *, chunk):
    B, T, C = zx.shape
    n_chunks = T // chunk
    per = chunk // SUBLANES
    last = T // SUBLANES - 1
    pos = lambda d, c: c + d * (n_chunks - 1 - 2 * c)
    per_dir = lambda r, c_: pl.BlockSpec((None, r, c_), lambda b_, d, c: (d, 0, 0))
    state = pl.BlockSpec((None, None, SUBLANES, C), lambda b_, d, c: (b_, d, 0, 0))
    return pl.pallas_call(
        _rnn_kernel,
        grid=(B, 2, n_chunks),
        in_specs=[pl.BlockSpec((None, SUBLANES, C),
                               lambda b_, d, c: (b_, jnp.maximum(pos(d, c) * per - 1, 0), 0)),
                  pl.BlockSpec((None, chunk, C), lambda b_, d, c: (b_, pos(d, c), 0)),
                  pl.BlockSpec((None, SUBLANES, C),
                               lambda b_, d, c: (b_, jnp.minimum((pos(d, c) + 1) * per, last), 0)),
                  state,
                  per_dir(cw.shape[1], C), per_dir(1, C), per_dir(C, 2 * C), per_dir(1, 2 * C),
                  per_dir(1, C)],
        out_specs=[pl.BlockSpec((None, None, chunk, C), lambda b_, d, c: (b_, d, pos(d, c), 0)), state],
        out_shape=[jax.ShapeDtypeStruct((B, 2, T, C), F32),
                   jax.ShapeDtypeStruct((B, 2, SUBLANES, C), F32)],
        scratch_shapes=[pltpu.VMEM((chunk + 2 * SUBLANES, C), F32),
                        pltpu.VMEM((chunk, C), F32), pltpu.VMEM((chunk, C), F32),
                        pltpu.VMEM((SUBLANES, C), F32)],
        compiler_params=_params("arbitrary", "arbitrary", "arbitrary"),
        name="rglru",
    )(zx, zx, zx, h0, cw, cb, w, b, lam)


def _merge_kernel(x_ref, sh_ref, sc_ref, gt_ref, gpre_ref, gpost_ref, wg_ref, at_ref, cv_ref,
                  hr_ref, zr_ref, wao_ref, wco_ref, wro_ref, wout_ref, o_ref):
    D = x_ref.shape[-1]
    x = x_ref[...]
    h = _modulated(x, gpre_ref[...], sh_ref[...], sc_ref[...]).astype(BF16)
    rnn = ((hr_ref[0] + hr_ref[1]) * jax.nn.gelu(zr_ref[...])).astype(BF16)
    branches = ((at_ref[...], wao_ref), (cv_ref[...], wco_ref), (rnn, wro_ref))
    m = None
    for i, (act, w_ref) in enumerate(branches):
        gate = jax.nn.sigmoid(_mm(h, wg_ref[:, i * D:(i + 1) * D]))
        term = gate * _mm(act, w_ref[...])
        m = term if m is None else m + term
    o = _mm(m.astype(BF16), wout_ref[...])
    o_ref[...] = x + gt_ref[...] * _rms(o, gpost_ref[...])


def _merge_call(x, shift, scale, gate, gpre, gpost, wg, attn, conv, hr, zr, wao, wco, wro, wout,
                *, tm):
    B, T, D = x.shape
    tok = lambda w_: pl.BlockSpec((None, tm, w_), lambda b, t: (b, t, 0))
    vec = pl.BlockSpec((None, 1, D), lambda b, t: (b, 0, 0))
    return pl.pallas_call(
        _merge_kernel,
        grid=(B, T // tm),
        in_specs=[tok(D), vec, vec, vec, _const_spec((1, D)), _const_spec((1, D)),
                  _const_spec(wg.shape), tok(D_Q), tok(D_CONV),
                  pl.BlockSpec((None, 2, tm, D_RNN), lambda b, t: (b, 0, t, 0)), tok(D_RNN),
                  _const_spec(wao.shape), _const_spec(wco.shape), _const_spec(wro.shape),
                  _const_spec(wout.shape)],
        out_specs=tok(D),
        out_shape=jax.ShapeDtypeStruct((B, T, D), F32),
        compiler_params=_params("arbitrary", "arbitrary"),
        name="merge",
    )(x, shift, scale, gate, gpre, gpost, wg, attn, conv, hr, zr, wao, wco, wro, wout)


_FFN_CHUNK = 512


def _ffn_kernel(xp_ref, x_ref, xn_ref, sh_ref, sc_ref, gt_ref, gpre_ref, gpost_ref, up_ref,
                dw_ref, db_ref, down_ref, o_ref, ug_ref, uv_ref):
    tm = x_ref.shape[0]
    t = pl.program_id(1)
    x = x_ref[...]
    mod = lambda v: _modulated(v, gpre_ref[...], sh_ref[...], sc_ref[...])
    hp = mod(xp_ref[...]) * jnp.where(t > 0, 1.0, 0.0)
    hn = mod(xn_ref[...]) * jnp.where(t < pl.num_programs(1) - 1, 1.0, 0.0)
    h = jnp.concatenate([hp, mod(x), hn], axis=0).astype(BF16)

    acc = jnp.zeros(x.shape, F32)
    for c in range(D_FF // _FFN_CHUNK):
        cg = slice(c * _FFN_CHUNK, (c + 1) * _FFN_CHUNK)
        cv = slice(D_FF + c * _FFN_CHUNK, D_FF + (c + 1) * _FFN_CHUNK)
        ug_ref[...] = _mm(h, up_ref[:, cg])
        uv_ref[...] = _mm(h, up_ref[:, cv])
        yg = db_ref[:, cg]
        yv = db_ref[:, cv]
        for j in range(3):
            rows = slice(SUBLANES - 1 + j, SUBLANES - 1 + j + tm)
            yg = yg + ug_ref[rows, :] * dw_ref[j:j + 1, cg]
            yv = yv + uv_ref[rows, :] * dw_ref[j:j + 1, cv]
        act = (jax.nn.gelu(yg) * yv).astype(BF16)
        acc = acc + _mm(act, down_ref[cg, :])
    o_ref[...] = x + gt_ref[...] * _rms(acc, gpost_ref[...])


def _ffn_call(x, shift, scale, gate, gpre, gpost, up, dw, db, down, *, tm):
    B, T, D = x.shape
    per = tm // SUBLANES
    last = T // SUBLANES - 1
    vec = pl.BlockSpec((None, 1, D), lambda b, t: (b, 0, 0))
    return pl.pallas_call(
        _ffn_kernel,
        grid=(B, T // tm),
        in_specs=[pl.BlockSpec((None, SUBLANES, D), lambda b, t: (b, jnp.maximum(t * per - 1, 0), 0)),
                  pl.BlockSpec((None, tm, D), lambda b, t: (b, t, 0)),
                  pl.BlockSpec((None, SUBLANES, D), lambda b, t: (b, jnp.minimum((t + 1) * per, last), 0)),
                  vec, vec, vec, _const_spec((1, D)), _const_spec((1, D)), _const_spec(up.shape),
                  _const_spec(dw.shape), _const_spec(db.shape), _const_spec(down.shape)],
        out_specs=pl.BlockSpec((None, tm, D), lambda b, t: (b, t, 0)),
        out_shape=jax.ShapeDtypeStruct((B, T, D), F32),
        scratch_shapes=[pltpu.VMEM((tm + 2 * SUBLANES, _FFN_CHUNK), F32),
                        pltpu.VMEM((tm + 2 * SUBLANES, _FFN_CHUNK), F32)],
        compiler_params=_params("arbitrary", "arbitrary"),
        name="conv_ffn",
    )(x, x, x, shift, scale, gate, gpre, gpost, up, dw, db, down)


def _rope_tables(n):
    rows = n // GRID_W
    row = jnp.repeat(jnp.arange(rows), GRID_W).astype(F32)
    col = jnp.tile(jnp.arange(GRID_W), rows).astype(F32)
    n_freq = HEAD_DIM // 4
    freq = ROPE_THETA ** (-jnp.arange(n_freq, dtype=F32) / n_freq)
    ang = jnp.concatenate([row[:, None] * freq, col[:, None] * freq], axis=-1)
    cos, sin = jnp.cos(ang), jnp.sin(ang)
    cos_t = jnp.tile(jnp.concatenate([cos, cos], axis=-1), (1, N_Q_HEADS))
    sin_t = jnp.tile(jnp.concatenate([-sin, sin], axis=-1), (1, N_Q_HEADS))
    return cos_t, sin_t


def _block_diag(w):
    nb, bi, bj = w.shape
    eye = jnp.eye(nb, dtype=w.dtype)
    return jnp.einsum('hij,hg->higj', w, eye).reshape(nb * bi, nb * bj)


def kernel(x, c, ctx, c_ctx, w_mod, b_mod, norm_pre_mix, norm_post_mix, norm_pre_ffn, norm_post_ffn,
           w_in, q_norm, k_norm, w_attn_out, conv_dw, conv_dw_b, conv_ln_g, conv_ln_b, w_conv_out,
           rnn_conv_w, rnn_conv_b, rnn_wa, rnn_ba, rnn_wx, rnn_bx, rnn_lambda, w_rnn_out, w_out,
           ffn_up, ffn_dw, ffn_dw_b, ffn_down):
    B, N, D = x.shape
    M = ctx.shape[1]
    L = w_mod.shape[0]
    tm_lat = min(N, 256)
    tm_ctx = min(M, 256)
    chunk_lat = min(N, 256)
    chunk_ctx = min(M, 256)

    mod_rows = 2 * SUBLANES
    cc = jnp.concatenate([c, c_ctx[None, :], jnp.zeros((mod_rows - B - 1, D), F32)], axis=0)
    mods = _mod_call(cc, w_mod.astype(BF16), b_mod[:, None, :])

    cos_t, sin_t = _rope_tables(N)
    head_id = jnp.arange(D_Q) // HEAD_DIM
    bd = (head_id[:, None] == head_id[None, :]).astype(BF16)
    h0 = jnp.zeros((B, 2, SUBLANES, D_RNN), F32)

    for l in range(L):
        need_ctx = l < L - 1
        lat = [mods[l, :B, k * D:(k + 1) * D][:, None, :] for k in range(6)]
        cm = [jnp.broadcast_to(mods[l, B:B + 1, k * D:(k + 1) * D][None], (B, 1, D)) for k in range(6)]
        row = lambda v: v[l][None, :]
        w_in_l = w_in[l].astype(BF16)
        w_proj, w_gate = w_in_l[:, :_OFF_GATE], w_in_l[:, _OFF_GATE:]
        qg = jnp.tile(q_norm[l], N_Q_HEADS)[None, :]
        kg = jnp.tile(k_norm[l], N_KV_HEADS)[None, :]
        rnn_w = jnp.stack([jnp.concatenate([_block_diag(rnn_wa[l, d]), _block_diag(rnn_wx[l, d])], axis=1)
                           for d in range(2)]).astype(BF16)
        rnn_b = jnp.concatenate([rnn_ba[l], rnn_bx[l]], axis=-1)[:, None, :]
        rnn_cw = rnn_conv_w[l]
        rnn_cb = rnn_conv_b[l][:, None, :]
        lam = rnn_lambda[l][:, None, :]
        wao, wco, wro, wout = (w_attn_out[l].astype(BF16), w_conv_out[l].astype(BF16),
                               w_rnn_out[l].astype(BF16), w_out[l].astype(BF16))
        up, down = ffn_up[l].astype(BF16), ffn_down[l].astype(BF16)
        dw, db = ffn_dw[l], ffn_dw_b[l][None, :]

        qc, kc, vc, uc, zxc, zrc = _inproj_call(ctx, cm[0], cm[1], row(norm_pre_mix), w_proj, qg, kg, bd,
                                                None, None, tm=tm_ctx)
        hc, hc_last = _rnn_call(zxc, h0, rnn_cw, rnn_cb, rnn_w, rnn_b, lam, chunk=chunk_ctx)

        q, k, v, u, zx, zr = _inproj_call(x, lat[0], lat[1], row(norm_pre_mix), w_proj, qg, kg, bd,
                                          cos_t, sin_t, tm=tm_lat)
        attn = _attn_call(q, [(kc, vc), (k, v)], tq=tm_lat)
        conv = _conv_call(u, conv_dw[l], row(conv_dw_b), row(conv_ln_g), row(conv_ln_b), tm=tm_lat)
        hr, _ = _rnn_call(zx, hc_last, rnn_cw, rnn_cb, rnn_w, rnn_b, lam, chunk=chunk_lat)
        x = _merge_call(x, lat[0], lat[1], lat[2], row(norm_pre_mix), row(norm_post_mix), w_gate,
                        attn, conv, hr, zr, wao, wco, wro, wout, tm=tm_lat)
        x = _ffn_call(x, lat[3], lat[4], lat[5], row(norm_pre_ffn), row(norm_post_ffn),
                      up, dw, db, down, tm=tm_lat)

        if need_ctx:
            attn_c = _attn_call(qc, [(kc, vc)], tq=tm_ctx)
            conv_c = _conv_call(uc, conv_dw[l], row(conv_dw_b), row(conv_ln_g), row(conv_ln_b), tm=tm_ctx)
            ctx = _merge_call(ctx, cm[0], cm[1], cm[2], row(norm_pre_mix), row(norm_post_mix), w_gate,
                              attn_c, conv_c, hc, zrc, wao, wco, wro, wout, tm=tm_ctx)
            ctx = _ffn_call(ctx, cm[3], cm[4], cm[5], row(norm_pre_ffn), row(norm_post_ffn),
                            up, dw, db, down, tm=tm_ctx)
    return x
```

```python
import functools

import jax
import jax.numpy as jnp
from jax import lax
from jax.experimental import pallas as pl
from jax.experimental.pallas import tpu as pltpu

F32 = jnp.float32
BF16 = jnp.bfloat16

EPS = 1e-6
HEAD_DIM = 64
N_Q_HEADS = 8
N_KV_HEADS = 2
Q_GROUP = N_Q_HEADS // N_KV_HEADS
D_Q = N_Q_HEADS * HEAD_DIM
D_KV = N_KV_HEADS * HEAD_DIM
D_CONV = 512
CONV_K = 31
D_RNN = 512
RNN_CONV_K = 4
LRU_C = 8.0
D_FF = 3072
GRID_W = 64
ROPE_THETA = 10000.0

LANES = 128
SUBLANES = 8
VMEM_LIMIT = 56 * 1024 * 1024

_OFF_Q = 0
_OFF_KV = D_Q
_OFF_CONV = _OFF_KV + 2 * D_KV
_OFF_ZX = _OFF_CONV + 2 * D_CONV
_OFF_ZR = _OFF_ZX + D_RNN
_OFF_GATE = _OFF_ZR + D_RNN


def _const_spec(shape):
    zeros = (0,) * len(shape)
    return pl.BlockSpec(shape, lambda *_: zeros, pipeline_mode=pl.Buffered(1))


def _params(*sem):
    return pltpu.CompilerParams(dimension_semantics=sem, vmem_limit_bytes=VMEM_LIMIT)


def _rms(x, g):
    return x * lax.rsqrt(jnp.mean(x * x, axis=-1, keepdims=True) + EPS) * g


def _mm(a, b):
    return jnp.dot(a, b, preferred_element_type=F32)


def _modulated(x, g, shift, scale):
    return _rms(x, g) * (1.0 + scale) + shift


def _mod_kernel(c_ref, w_ref, b_ref, o_ref):
    c = c_ref[...]
    a = (c * jax.nn.sigmoid(c)).astype(BF16)
    o_ref[...] = _mm(a, w_ref[...]) + b_ref[...]


def _mod_call(cc, w_mod, b_mod):
    L, D, D6 = w_mod.shape
    R = cc.shape[0]
    bn = D6 // 4
    return pl.pallas_call(
        _mod_kernel,
        grid=(L, D6 // bn),
        in_specs=[
            pl.BlockSpec((R, D), lambda l, j: (0, 0)),
            pl.BlockSpec((None, D, bn), lambda l, j: (l, 0, j)),
            pl.BlockSpec((None, 1, bn), lambda l, j: (l, 0, j)),
        ],
        out_specs=pl.BlockSpec((None, R, bn), lambda l, j: (l, 0, j)),
        out_shape=jax.ShapeDtypeStruct((L, R, D6), F32),
        compiler_params=_params("arbitrary", "arbitrary"),
        name="adaln_mod",
    )(cc, w_mod, b_mod)


def _head_rms(z, gain, bd):
    sq = z * z
    hi = sq.astype(BF16)
    lo = (sq - hi.astype(F32)).astype(BF16)
    ms = (_mm(hi, bd) + _mm(lo, bd)) * (1.0 / HEAD_DIM)
    return z * lax.rsqrt(ms + EPS) * gain


def _rope(x, c, s):
    rows, width = x.shape
    lane = lax.broadcasted_iota(jnp.int32, (rows, LANES), 1)
    first_half = (lane & (HEAD_DIM // 2)) == 0
    cols = []
    for j in range(width // LANES):
        sl = slice(j * LANES, (j + 1) * LANES)
        xc = x[:, sl]
        partner = jnp.where(first_half,
                            pltpu.roll(xc, LANES - HEAD_DIM // 2, 1),
                            pltpu.roll(xc, HEAD_DIM // 2, 1))
        cols.append(xc * c[:, sl] + partner * s[:, sl])
    return jnp.concatenate(cols, axis=1) if len(cols) > 1 else cols[0]


def _inproj_kernel(*refs, rope):
    if rope:
        (x_ref, sh_ref, sc_ref, g_ref, w_ref, qg_ref, kg_ref, bd_ref, cos_ref, sin_ref,
         q_ref, k_ref, v_ref, u_ref, zx_ref, zr_ref) = refs
    else:
        (x_ref, sh_ref, sc_ref, g_ref, w_ref, qg_ref, kg_ref, bd_ref,
         q_ref, k_ref, v_ref, u_ref, zx_ref, zr_ref) = refs
    h = _modulated(x_ref[...], g_ref[...], sh_ref[...], sc_ref[...]).astype(BF16)

    zq = _mm(h, w_ref[:, _OFF_Q:_OFF_Q + D_Q])
    qn = _head_rms(zq, qg_ref[...], bd_ref[...])
    if rope:
        qn = _rope(qn, cos_ref[...], sin_ref[...])
    q_ref[...] = (qn * (HEAD_DIM ** -0.5)).astype(BF16)

    zkv = _mm(h, w_ref[:, _OFF_KV:_OFF_KV + 2 * D_KV])
    kn = _head_rms(zkv[:, :D_KV], kg_ref[...], bd_ref[:D_KV, :D_KV])
    if rope:
        kn = _rope(kn, cos_ref[:, :D_KV], sin_ref[:, :D_KV])
    k_ref[...] = kn
    v_ref[...] = zkv[:, D_KV:]

    zc = _mm(h, w_ref[:, _OFF_CONV:_OFF_CONV + 2 * D_CONV])
    u_ref[...] = zc[:, :D_CONV] * jax.nn.sigmoid(zc[:, D_CONV:])

    zx_ref[...] = _mm(h, w_ref[:, _OFF_ZX:_OFF_ZX + D_RNN])
    zr_ref[...] = _mm(h, w_ref[:, _OFF_ZR:_OFF_ZR + D_RNN])


def _inproj_call(x, shift, scale, gain, w, qg, kg, bd, cos, sin, *, tm):
    B, T, D = x.shape
    rope = cos is not None
    tok = lambda w_: pl.BlockSpec((None, tm, w_), lambda b, t: (b, t, 0))
    vec = pl.BlockSpec((None, 1, D), lambda b, t: (b, 0, 0))
    in_specs = [tok(D), vec, vec, _const_spec((1, D)), _const_spec(w.shape),
                _const_spec((1, D_Q)), _const_spec((1, D_KV)), _const_spec(bd.shape)]
    args = [x, shift, scale, gain, w, qg, kg, bd]
    if rope:
        tab = pl.BlockSpec((tm, D_Q), lambda b, t: (t, 0))
        in_specs += [tab, tab]
        args += [cos, sin]
    out_shape = [jax.ShapeDtypeStruct((B, T, D_Q), BF16),
                 jax.ShapeDtypeStruct((B, T, D_KV), F32),
                 jax.ShapeDtypeStruct((B, T, D_KV), F32),
                 jax.ShapeDtypeStruct((B, T, D_CONV), F32),
                 jax.ShapeDtypeStruct((B, T, D_RNN), F32),
                 jax.ShapeDtypeStruct((B, T, D_RNN), F32)]
    out_specs = [tok(D_Q), tok(D_KV), tok(D_KV), tok(D_CONV), tok(D_RNN), tok(D_RNN)]
    return pl.pallas_call(
        functools.partial(_inproj_kernel, rope=rope),
        grid=(B, T // tm),
        in_specs=in_specs, out_specs=out_specs, out_shape=out_shape,
        compiler_params=_params("arbitrary", "arbitrary"),
        name="inproj_rope" if rope else "inproj",
    )(*args)


def _attn_kernel(*refs, seg_lens):
    nseg = len(seg_lens)
    q_ref = refs[0]
    kv_refs = refs[1:1 + 2 * nseg]
    o_ref = refs[1 + 2 * nseg]
    kbd_ref, vbd_ref, p_ref = refs[2 + 2 * nseg:]
    m_tot = sum(seg_lens)
    g = pl.program_id(1)

    @pl.when(pl.program_id(2) == 0)
    def _():
        off = 0
        for i, ms in enumerate(seg_lens):
            lane_half = lax.broadcasted_iota(jnp.int32, (ms, LANES), 1) // HEAD_DIM
            zero = jnp.zeros((ms, LANES), BF16)
            for src, dst in ((kv_refs[2 * i], kbd_ref), (kv_refs[2 * i + 1], vbd_ref)):
                a = src[...]
                dup = jnp.where(lane_half == g, a, pltpu.roll(a, HEAD_DIM, 1))
                for h in range(Q_GROUP):
                    half, col = h % 2, h // 2
                    ah = jnp.where(lane_half == half, dup, 0.0).astype(BF16)
                    rows = slice(h * m_tot + off, h * m_tot + off + ms)
                    dst[rows, col * LANES:(col + 1) * LANES] = ah
                    dst[rows, (1 - col) * LANES:(2 - col) * LANES] = zero
            off += ms

    q = q_ref[...]
    inv = []
    for h in range(Q_GROUP):
        cols = slice(h * m_tot, (h + 1) * m_tot)
        s = lax.dot_general(q, kbd_ref[cols, :], (((1,), (1,)), ((), ())),
                            preferred_element_type=F32)
        p = jnp.exp(s - jnp.max(s, axis=-1, keepdims=True))
        inv.append(1.0 / jnp.sum(p, axis=-1, keepdims=True))
        p_ref[:, cols] = p.astype(BF16)
    o = _mm(p_ref[...], vbd_ref[...])
    head = lax.broadcasted_iota(jnp.int32, o.shape, 1) // HEAD_DIM
    scale = jnp.where(head == 0, inv[0], jnp.where(head == 1, inv[1],
                      jnp.where(head == 2, inv[2], inv[3])))
    o_ref[...] = (o * scale).astype(BF16)


def _attn_call(q, segs, *, tq):
    B, T, _ = q.shape
    seg_lens = tuple(k.shape[1] for k, _ in segs)
    m_tot = sum(seg_lens)
    gw = Q_GROUP * HEAD_DIM
    in_specs = [pl.BlockSpec((None, tq, gw), lambda b, g, t: (b, t, g))]
    args = [q]
    for k, v in segs:
        spec = pl.BlockSpec((None, k.shape[1], D_KV), lambda b, g, t: (b, 0, 0))
        in_specs += [spec, spec]
        args += [k, v]
    return pl.pallas_call(
        functools.partial(_attn_kernel, seg_lens=seg_lens),
        grid=(B, N_KV_HEADS, T // tq),
        in_specs=in_specs,
        out_specs=pl.BlockSpec((None, tq, gw), lambda b, g, t: (b, t, g)),
        out_shape=jax.ShapeDtypeStruct((B, T, D_Q), BF16),
        scratch_shapes=[pltpu.VMEM((Q_GROUP * m_tot, gw), BF16),
                        pltpu.VMEM((Q_GROUP * m_tot, gw), BF16),
                        pltpu.VMEM((tq, Q_GROUP * m_tot), BF16)],
        compiler_params=_params("arbitrary", "arbitrary", "arbitrary"),
        name="gqa_attention",
    )(*args)


_CONV_HALO = 16
_CONV_ROWS = 32


def _halo_specs(tm, halo, width, n_rows):
    per = tm // halo
    last = n_rows // halo - 1
    prev = pl.BlockSpec((None, halo, width), lambda b, t: (b, jnp.maximum(t * per - 1, 0), 0))
    nxt = pl.BlockSpec((None, halo, width), lambda b, t: (b, jnp.minimum((t + 1) * per, last), 0))
    return prev, nxt


def _conv_kernel(up_ref, u_ref, un_ref, w_ref, b_ref, lg_ref, lb_ref, o_ref, pad_ref):
    tm = u_ref.shape[0]
    t = pl.program_id(1)
    pad_ref[0:_CONV_HALO, :] = up_ref[...] * jnp.where(t > 0, 1.0, 0.0)
    pad_ref[_CONV_HALO:_CONV_HALO + tm, :] = u_ref[...]
    pad_ref[_CONV_HALO + tm:, :] = un_ref[...] * jnp.where(t < pl.num_programs(1) - 1, 1.0, 0.0)
    first = _CONV_HALO - CONV_K // 2
    for c in range(tm // _CONV_ROWS):
        r = c * _CONV_ROWS
        acc = jnp.zeros((_CONV_ROWS, D_CONV), F32)
        for j in range(CONV_K):
            acc = acc + pad_ref[r + first + j:r + first + j + _CONV_ROWS, :] * w_ref[j:j + 1, :]
        acc = acc + b_ref[...]
        mu = jnp.mean(acc, axis=-1, keepdims=True)
        cen = acc - mu
        var = jnp.mean(cen * cen, axis=-1, keepdims=True)
        y = cen * lax.rsqrt(var + EPS) * lg_ref[...] + lb_ref[...]
        o_ref[r:r + _CONV_ROWS, :] = (y * jax.nn.sigmoid(y)).astype(BF16)


def _conv_call(u, w, b, lg, lb, *, tm):
    B, T, C = u.shape
    prev, nxt = _halo_specs(tm, _CONV_HALO, C, T)
    tile = pl.BlockSpec((None, tm, C), lambda b_, t: (b_, t, 0))
    return pl.pallas_call(
        _conv_kernel,
        grid=(B, T // tm),
        in_specs=[prev, tile, nxt,
                  _const_spec(w.shape), _const_spec((1, C)), _const_spec((1, C)), _const_spec((1, C))],
        out_specs=tile,
        out_shape=jax.ShapeDtypeStruct((B, T, C), BF16),
        scratch_shapes=[pltpu.VMEM((tm + 2 * _CONV_HALO, C), F32)],
        compiler_params=_params("arbitrary", "arbitrary"),
        name="conformer_conv",
    )(u, u, u, w, b, lg, lb)


def _rnn_kernel(zp_ref, zx_ref, zn_ref, h0_ref, cw_ref, cb_ref, w_ref, b_ref, lam_ref, h_ref, hl_ref,
                xp_ref, a_ref, bb_ref, hc_ref):
    chunk = zx_ref.shape[0]
    d = pl.program_id(1)
    c = pl.program_id(2)
    n_chunks = pl.num_programs(2)
    cc = c + d * (n_chunks - 1 - 2 * c)

    @pl.when(c == 0)
    def _():
        hc_ref[...] = h0_ref[...]

    xp_ref[0:SUBLANES, :] = zp_ref[...] * jnp.where(cc > 0, 1.0, 0.0)
    xp_ref[SUBLANES:SUBLANES + chunk, :] = zx_ref[...]
    xp_ref[SUBLANES + chunk:, :] = zn_ref[...] * jnp.where(cc < n_chunks - 1, 1.0, 0.0)

    lam = lam_ref[...]
    log_sig = jnp.minimum(lam, 0.0) - jnp.log1p(jnp.exp(-jnp.abs(lam)))
    c_log = LRU_C * log_sig
    row = lax.broadcasted_iota(jnp.int32, (SUBLANES, D_RNN), 0)

    def run(reverse):
        xc = jnp.zeros((chunk, D_RNN), F32)
        for j in range(RNN_CONV_K):
            off = SUBLANES + ((RNN_CONV_K - 1 - j) if reverse else (j - RNN_CONV_K + 1))
            xc = xc + xp_ref[off:off + chunk, :] * cw_ref[j:j + 1, :]
        xc = xc + cb_ref[...]
        gates = _mm(xc.astype(BF16), w_ref[...]) + b_ref[...]
        r_gate = jax.nn.sigmoid(gates[:, :D_RNN])
        i_gate = jax.nn.sigmoid(gates[:, D_RNN:])
        a = jnp.exp(c_log * r_gate)
        a_ref[...] = a
        bb_ref[...] = jnp.sqrt((1.0 - a) * (1.0 + a)) * (i_gate * xc)

        def tile_body(k, hc):
            kk = (chunk // SUBLANES - 1 - k) if reverse else k
            t0 = pl.multiple_of(kk * SUBLANES, SUBLANES)
            at = a_ref[pl.ds(t0, SUBLANES), :]
            bt = bb_ref[pl.ds(t0, SUBLANES), :]
            for s in (1, 2, 4):
                if reverse:
                    ok = row < SUBLANES - s
                    shift = SUBLANES - s
                else:
                    ok = row >= s
                    shift = s
                a_sh = jnp.where(ok, pltpu.roll(at, shift, 0), 1.0)
                b_sh = jnp.where(ok, pltpu.roll(bt, shift, 0), 0.0)
                bt = at * b_sh + bt
                at = at * a_sh
            ht = at * hc + bt
            h_ref[pl.ds(t0, SUBLANES), :] = ht
            edge = ht[0:1, :] if reverse else ht[SUBLANES - 1:SUBLANES, :]
            return jnp.broadcast_to(edge, (SUBLANES, D_RNN))

        hc = lax.fori_loop(0, chunk // SUBLANES, tile_body, hc_ref[...], unroll=4)
        hc_ref[...] = hc
        hl_ref[...] = hc

    pl.when(d == 0)(lambda: run(False))
    pl.when(d == 1)(lambda: run(True))


def _rnn_call(zx, h0, cw, cb, w, b, lam, *, chunk):
    B, T, C = zx.shape
    n_chunks = T // chunk
    per = chunk // SUBLANES
    last = T // SUBLANES - 1
    pos = lambda d, c: c + d * (n_chunks - 1 - 2 * c)
    per_dir = lambda r, c_: pl.BlockSpec((None, r, c_), lambda b_, d, c: (d, 0, 0))
    state = pl.BlockSpec((None, None, SUBLANES, C), lambda b_, d, c: (b_, d, 0, 0))
    return pl.pallas_call(
        _rnn_kernel,
        grid=(B, 2, n_chunks),
        in_specs=[pl.BlockSpec((None, SUBLANES, C),
                               lambda b_, d, c: (b_, jnp.maximum(pos(d, c) * per - 1, 0), 0)),
                  pl.BlockSpec((None, chunk, C), lambda b_, d, c: (b_, pos(d, c), 0)),
                  pl.BlockSpec((None, SUBLANES, C),
                               lambda b_, d, c: (b_, jnp.minimum((pos(d, c) + 1) * per, last), 0)),
                  state,
                  per_dir(cw.shape[1], C), per_dir(1, C), per_dir(C, 2 * C), per_dir(1, 2 * C),
                  per_dir(1, C)],
        out_specs=[pl.BlockSpec((None, None, chunk, C), lambda b_, d, c: (b_, d, pos(d, c), 0)), state],
        out_shape=[jax.ShapeDtypeStruct((B, 2, T, C), F32),
                   jax.ShapeDtypeStruct((B, 2, SUBLANES, C), F32)],
        scratch_shapes=[pltpu.VMEM((chunk + 2 * SUBLANES, C), F32),
                        pltpu.VMEM((chunk, C), F32), pltpu.VMEM((chunk, C), F32),
                        pltpu.VMEM((SUBLANES, C), F32)],
        compiler_params=_params("arbitrary", "arbitrary", "arbitrary"),
        name="rglru",
    )(zx, zx, zx, h0, cw, cb, w, b, lam)


def _merge_kernel(x_ref, sh_ref, sc_ref, gt_ref, gpre_ref, gpost_ref, wg_ref, at_ref, cv_ref,
                  hr_ref, zr_ref, wao_ref, wco_ref, wro_ref, wout_ref, o_ref):
    D = x_ref.shape[-1]
    x = x_ref[...]
    h = _modulated(x, gpre_ref[...], sh_ref[...], sc_ref[...]).astype(BF16)
    rnn = ((hr_ref[0] + hr_ref[1]) * jax.nn.gelu(zr_ref[...])).astype(BF16)
    branches = ((at_ref[...], wao_ref), (cv_ref[...], wco_ref), (rnn, wro_ref))
    m = None
    for i, (act, w_ref) in enumerate(branches):
        gate = jax.nn.sigmoid(_mm(h, wg_ref[:, i * D:(i + 1) * D]))
        term = gate * _mm(act, w_ref[...])
        m = term if m is None else m + term
    o = _mm(m.astype(BF16), wout_ref[...])
    o_ref[...] = x + gt_ref[...] * _rms(o, gpost_ref[...])


def _merge_call(x, shift, scale, gate, gpre, gpost, wg, attn, conv, hr, zr, wao, wco, wro, wout,
                *, tm):
    B, T, D = x.shape
    tok = lambda w_: pl.BlockSpec((None, tm, w_), lambda b, t: (b, t, 0))
    vec = pl.BlockSpec((None, 1, D), lambda b, t: (b, 0, 0))
    return pl.pallas_call(
        _merge_kernel,
        grid=(B, T // tm),
        in_specs=[tok(D), vec, vec, vec, _const_spec((1, D)), _const_spec((1, D)),
                  _const_spec(wg.shape), tok(D_Q), tok(D_CONV),
                  pl.BlockSpec((None, 2, tm, D_RNN), lambda b, t: (b, 0, t, 0)), tok(D_RNN),
                  _const_spec(wao.shape), _const_spec(wco.shape), _const_spec(wro.shape),
                  _const_spec(wout.shape)],
        out_specs=tok(D),
        out_shape=jax.ShapeDtypeStruct((B, T, D), F32),
        compiler_params=_params("arbitrary", "arbitrary"),
        name="merge",
    )(x, shift, scale, gate, gpre, gpost, wg, attn, conv, hr, zr, wao, wco, wro, wout)


_FFN_CHUNK = 512


def _ffn_kernel(xp_ref, x_ref, xn_ref, sh_ref, sc_ref, gt_ref, gpre_ref, gpost_ref, up_ref,
                dw_ref, db_ref, down_ref, o_ref, ug_ref, uv_ref):
    tm = x_ref.shape[0]
    t = pl.program_id(1)
    x = x_ref[...]
    mod = lambda v: _modulated(v, gpre_ref[...], sh_ref[...], sc_ref[...])
    hp = mod(xp_ref[...]) * jnp.where(t > 0, 1.0, 0.0)
    hn = mod(xn_ref[...]) * jnp.where(t < pl.num_programs(1) - 1, 1.0, 0.0)
    h = jnp.concatenate([hp, mod(x), hn], axis=0).astype(BF16)

    acc = jnp.zeros(x.shape, F32)
    for c in range(D_FF // _FFN_CHUNK):
        cg = slice(c * _FFN_CHUNK, (c + 1) * _FFN_CHUNK)
        cv = slice(D_FF + c * _FFN_CHUNK, D_FF + (c + 1) * _FFN_CHUNK)
        ug_ref[...] = _mm(h, up_ref[:, cg])
        uv_ref[...] = _mm(h, up_ref[:, cv])
        yg = db_ref[:, cg]
        yv = db_ref[:, cv]
        for j in range(3):
            rows = slice(SUBLANES - 1 + j, SUBLANES - 1 + j + tm)
            yg = yg + ug_ref[rows, :] * dw_ref[j:j + 1, cg]
            yv = yv + uv_ref[rows, :] * dw_ref[j:j + 1, cv]
        act = (jax.nn.gelu(yg) * yv).astype(BF16)
        acc = acc + _mm(act, down_ref[cg, :])
    o_ref[...] = x + gt_ref[...] * _rms(acc, gpost_ref[...])


def _ffn_call(x, shift, scale, gate, gpre, gpost, up, dw, db, down, *, tm):
    B, T, D = x.shape
    per = tm // SUBLANES
    last = T // SUBLANES - 1
    vec = pl.BlockSpec((None, 1, D), lambda b, t: (b, 0, 0))
    return pl.pallas_call(
        _ffn_kernel,
        grid=(B, T // tm),
        in_specs=[pl.BlockSpec((None, SUBLANES, D), lambda b, t: (b, jnp.maximum(t * per - 1, 0), 0)),
                  pl.BlockSpec((None, tm, D), lambda b, t: (b, t, 0)),
                  pl.BlockSpec((None, SUBLANES, D), lambda b, t: (b, jnp.minimum((t + 1) * per, last), 0)),
                  vec, vec, vec, _const_spec((1, D)), _const_spec((1, D)), _const_spec(up.shape),
                  _const_spec(dw.shape), _const_spec(db.shape), _const_spec(down.shape)],
        out_specs=pl.BlockSpec((None, tm, D), lambda b, t: (b, t, 0)),
        out_shape=jax.ShapeDtypeStruct((B, T, D), F32),
        scratch_shapes=[pltpu.VMEM((tm + 2 * SUBLANES, _FFN_CHUNK), F32),
                        pltpu.VMEM((tm + 2 * SUBLANES, _FFN_CHUNK), F32)],
        compiler_params=_params("arbitrary", "arbitrary"),
        name="conv_ffn",
    )(x, x, x, shift, scale, gate, gpre, gpost, up, dw, db, down)


def _rope_tables(n):
    rows = n // GRID_W
    row = jnp.repeat(jnp.arange(rows), GRID_W).astype(F32)
    col = jnp.tile(jnp.arange(GRID_W), rows).astype(F32)
    n_freq = HEAD_DIM // 4
    freq = ROPE_THETA ** (-jnp.arange(n_freq, dtype=F32) / n_freq)
    ang = jnp.concatenate([row[:, None] * freq, col[:, None] * freq], axis=-1)
    cos, sin = jnp.cos(ang), jnp.sin(ang)
    cos_t = jnp.tile(jnp.concatenate([cos, cos], axis=-1), (1, N_Q_HEADS))
    sin_t = jnp.tile(jnp.concatenate([-sin, sin], axis=-1), (1, N_Q_HEADS))
    return cos_t, sin_t


def _block_diag(w):
    nb, bi, bj = w.shape
    eye = jnp.eye(nb, dtype=w.dtype)
    return jnp.einsum('hij,hg->higj', w, eye).reshape(nb * bi, nb * bj)


def kernel(x, c, ctx, c_ctx, w_mod, b_mod, norm_pre_mix, norm_post_mix, norm_pre_ffn, norm_post_ffn,
           w_in, q_norm, k_norm, w_attn_out, conv_dw, conv_dw_b, conv_ln_g, conv_ln_b, w_conv_out,
           rnn_conv_w, rnn_conv_b, rnn_wa, rnn_ba, rnn_wx, rnn_bx, rnn_lambda, w_rnn_out, w_out,
           ffn_up, ffn_dw, ffn_dw_b, ffn_down):
    B, N, D = x.shape
    M = ctx.shape[1]
    L = w_mod.shape[0]
    tm_lat = min(N, 512)
    tm_ctx = min(M, 256)
    tc_lat = min(N, 256)
    chunk_lat = min(N, 256)
    chunk_ctx = min(M, 256)

    mod_rows = 2 * SUBLANES
    cc = jnp.concatenate([c, c_ctx[None, :], jnp.zeros((mod_rows - B - 1, D), F32)], axis=0)
    mods = _mod_call(cc, w_mod.astype(BF16), b_mod[:, None, :])

    cos_t, sin_t = _rope_tables(N)
    head_id = jnp.arange(D_Q) // HEAD_DIM
    bd = (head_id[:, None] == head_id[None, :]).astype(BF16)
    h0 = jnp.zeros((B, 2, SUBLANES, D_RNN), F32)

    for l in range(L):
        need_ctx = l < L - 1
        lat = [mods[l, :B, k * D:(k + 1) * D][:, None, :] for k in range(6)]
        cm = [jnp.broadcast_to(mods[l, B:B + 1, k * D:(k + 1) * D][None], (B, 1, D)) for k in range(6)]
        row = lambda v: v[l][None, :]
        w_in_l = w_in[l].astype(BF16)
        w_proj, w_gate = w_in_l[:, :_OFF_GATE], w_in_l[:, _OFF_GATE:]
        qg = jnp.tile(q_norm[l], N_Q_HEADS)[None, :]
        kg = jnp.tile(k_norm[l], N_KV_HEADS)[None, :]
        rnn_w = jnp.stack([jnp.concatenate([_block_diag(rnn_wa[l, d]), _block_diag(rnn_wx[l, d])], axis=1)
                           for d in range(2)]).astype(BF16)
        rnn_b = jnp.concatenate([rnn_ba[l], rnn_bx[l]], axis=-1)[:, None, :]
        rnn_cw = rnn_conv_w[l]
        rnn_cb = rnn_conv_b[l][:, None, :]
        lam = rnn_lambda[l][:, None, :]
        wao, wco, wro, wout = (w_attn_out[l].astype(BF16), w_conv_out[l].astype(BF16),
                               w_rnn_out[l].astype(BF16), w_out[l].astype(BF16))
        up, down = ffn_up[l].astype(BF16), ffn_down[l].astype(BF16)
        dw, db = ffn_dw[l], ffn_dw_b[l][None, :]

        qc, kc, vc, uc, zxc, zrc = _inproj_call(ctx, cm[0], cm[1], row(norm_pre_mix), w_proj, qg, kg, bd,
                                                None, None, tm=tm_ctx)
        hc, hc_last = _rnn_call(zxc, h0, rnn_cw, rnn_cb, rnn_w, rnn_b, lam, chunk=chunk_ctx)

        q, k, v, u, zx, zr = _inproj_call(x, lat[0], lat[1], row(norm_pre_mix), w_proj, qg, kg, bd,
                                          cos_t, sin_t, tm=tm_lat)
        attn = _attn_call(q, [(kc, vc), (k, v)], tq=tm_lat)
        conv = _conv_call(u, conv_dw[l], row(conv_dw_b), row(conv_ln_g), row(conv_ln_b), tm=tc_lat)
        hr, _ = _rnn_call(zx, hc_last, rnn_cw, rnn_cb, rnn_w, rnn_b, lam, chunk=chunk_lat)
        x = _merge_call(x, lat[0], lat[1], lat[2], row(norm_pre_mix), row(norm_post_mix), w_gate,
                        attn, conv, hr, zr, wao, wco, wro, wout, tm=tm_lat)
        x = _ffn_call(x, lat[3], lat[4], lat[5], row(norm_pre_ffn), row(norm_post_ffn),
                      up, dw, db, down, tm=tm_lat)

        if need_ctx:
            attn_c = _attn_call(qc, [(kc, vc)], tq=tm_ctx)
            conv_c = _conv_call(uc, conv_dw[l], row(conv_dw_b), row(conv_ln_g), row(conv_ln_b), tm=tm_ctx)
            ctx = _merge_call(ctx, cm[0], cm[1], cm[2], row(norm_pre_mix), row(norm_post_mix), w_gate,
                              attn_c, conv_c, hc, zrc, wao, wco, wro, wout, tm=tm_ctx)
            ctx = _ffn_call(ctx, cm[3], cm[4], cm[5], row(norm_pre_ffn), row(norm_post_ffn),
                            up, dw, db, down, tm=tm_ctx)
    return x
```

```python
import functools
import math

import jax
import jax.numpy as jnp
from jax import lax
from jax.experimental import pallas as pl
from jax.experimental.pallas import tpu as pltpu

F32 = jnp.float32
BF16 = jnp.bfloat16

EPS = 1e-6
HEAD_DIM = 64
N_Q_HEADS = 8
N_KV_HEADS = 2
Q_GROUP = N_Q_HEADS // N_KV_HEADS
D_Q = N_Q_HEADS * HEAD_DIM
D_KV = N_KV_HEADS * HEAD_DIM
D_CONV = 512
CONV_K = 31
D_RNN = 512
RNN_CONV_K = 4
LRU_C = 8.0
D_FF = 3072
GRID_W = 64
ROPE_THETA = 10000.0
N_MOD = 6

LANES = 128
SUBLANES = 8
SEG = SUBLANES
VMEM_LIMIT = 56 * 1024 * 1024

_OFF_Q = 0
_OFF_KV = D_Q
_OFF_CONV = _OFF_KV + 2 * D_KV
_OFF_ZX = _OFF_CONV + 2 * D_CONV
_OFF_ZR = _OFF_ZX + D_RNN
_OFF_GATE = _OFF_ZR + D_RNN

_Q_SCALE = HEAD_DIM ** -0.5 * math.log2(math.e)


def _const_spec(shape):
    zeros = (0,) * len(shape)
    return pl.BlockSpec(tuple(shape), lambda *_: zeros, pipeline_mode=pl.Buffered(1))


def _layer_spec(arr, l):
    tail = (0,) * (arr.ndim - 1)
    return pl.BlockSpec((None,) + tuple(arr.shape[1:]), lambda *_: (l,) + tail,
                        pipeline_mode=pl.Buffered(1))


def _mod_spec(l, k, row):
    def index(b, *_):
        return (l, b if row is None else row, k, 0, 0)
    return lambda d: pl.BlockSpec((None, None, None, 1, d), index)


def _params(*sem):
    return pltpu.CompilerParams(dimension_semantics=sem, vmem_limit_bytes=VMEM_LIMIT)


def _rms(x, g):
    return x * lax.rsqrt(jnp.mean(x * x, axis=-1, keepdims=True) + EPS) * g


def _mm(a, b):
    return jnp.dot(a, b, preferred_element_type=F32)


def _modulated(x, g, shift, scale):
    return _rms(x, g) * (1.0 + scale) + shift


def _seam_prev(a):
    sub = lax.broadcasted_iota(jnp.int32, a.shape, 0) % SEG
    return jnp.where(sub == 0, 0.0, pltpu.roll(a, 1, 0))


def _seam_next(a):
    sub = lax.broadcasted_iota(jnp.int32, a.shape, 0) % SEG
    return jnp.where(sub == SEG - 1, 0.0, pltpu.roll(a, a.shape[0] - 1, 0))


def _halo_specs(tm, halo, width, n_rows):
    per = tm // halo
    n_blocks = n_rows // halo
    prev = pl.BlockSpec((None, halo, width), lambda b, t: (b, (t * per + n_blocks - 1) % n_blocks, 0))
    nxt = pl.BlockSpec((None, halo, width), lambda b, t: (b, ((t + 1) * per) % n_blocks, 0))
    return prev, nxt


def _mod_kernel(c_ref, w_ref, b_ref, o_ref):
    c = c_ref[...]
    a = (c * jax.nn.sigmoid(c)).astype(BF16)
    o_ref[...] = _mm(a, w_ref[...].astype(BF16)) + b_ref[...]


def _mod_call(cc, w_mod, b_mod):
    L, D, D6 = w_mod.shape
    R = cc.shape[0]
    bn = D6 // 4
    return pl.pallas_call(
        _mod_kernel,
        grid=(L, D6 // bn),
        in_specs=[
            pl.BlockSpec((R, D), lambda l, j: (0, 0)),
            pl.BlockSpec((None, D, bn), lambda l, j: (l, 0, j)),
            pl.BlockSpec((None, 1, bn), lambda l, j: (l, 0, j)),
        ],
        out_specs=pl.BlockSpec((None, R, bn), lambda l, j: (l, 0, j)),
        out_shape=jax.ShapeDtypeStruct((L, R, D6), F32),
        compiler_params=_params("arbitrary", "arbitrary"),
        name="adaln_mod",
    )(cc, w_mod, b_mod)


def _head_rms(z, gain, bd):
    sq = z * z
    hi = sq.astype(BF16)
    lo = (sq - hi.astype(F32)).astype(BF16)
    ms = (_mm(hi, bd) + _mm(lo, bd)) * (1.0 / HEAD_DIM)
    return z * lax.rsqrt(ms + EPS) * gain


def _rope(x, c, s):
    rows, width = x.shape
    lane = lax.broadcasted_iota(jnp.int32, (rows, LANES), 1)
    first_half = (lane & (HEAD_DIM // 2)) == 0
    cols = []
    for j in range(width // LANES):
        sl = slice(j * LANES, (j + 1) * LANES)
        xc = x[:, sl]
        partner = jnp.where(first_half,
                            pltpu.roll(xc, LANES - HEAD_DIM // 2, 1),
                            pltpu.roll(xc, HEAD_DIM // 2, 1))
        cols.append(xc * c[:, sl] + partner * s[:, sl])
    return jnp.concatenate(cols, axis=1) if len(cols) > 1 else cols[0]


def _inproj_kernel(*refs, rope):
    if rope:
        (x_ref, sh_ref, sc_ref, g_ref, w_ref, qg_ref, kg_ref, bd_ref, cos_ref, sin_ref,
         q_ref, k_ref, v_ref, u_ref, zx_ref, zr_ref) = refs
    else:
        (x_ref, sh_ref, sc_ref, g_ref, w_ref, qg_ref, kg_ref, bd_ref,
         q_ref, k_ref, v_ref, u_ref, zx_ref, zr_ref) = refs
    h = _modulated(x_ref[...], g_ref[...], sh_ref[...], sc_ref[...]).astype(BF16)

    zq = _mm(h, w_ref[:, _OFF_Q:_OFF_Q + D_Q])
    qn = _head_rms(zq, qg_ref[...], bd_ref[...])
    if rope:
        qn = _rope(qn, cos_ref[...], sin_ref[...])
    q_ref[...] = (qn * _Q_SCALE).astype(BF16)

    zkv = _mm(h, w_ref[:, _OFF_KV:_OFF_KV + 2 * D_KV])
    kn = _head_rms(zkv[:, :D_KV], kg_ref[...], bd_ref[:D_KV, :D_KV])
    if rope:
        kn = _rope(kn, cos_ref[:, :D_KV], sin_ref[:, :D_KV])
    k_ref[...] = kn
    v_ref[...] = zkv[:, D_KV:]

    zc = _mm(h, w_ref[:, _OFF_CONV:_OFF_CONV + 2 * D_CONV])
    u_ref[...] = zc[:, :D_CONV] * jax.nn.sigmoid(zc[:, D_CONV:])

    zx_ref[...] = _mm(h, w_ref[:, _OFF_ZX:_OFF_ZX + D_RNN])
    zr_ref[...] = _mm(h, w_ref[:, _OFF_ZR:_OFF_ZR + D_RNN])


def _inproj_call(x, mods, l, mod_row, gain, w, qg, kg, bd, cos, sin, *, tm):
    B, T, D = x.shape
    rope = cos is not None
    tok = lambda w_: pl.BlockSpec((None, tm, w_), lambda b, t: (b, t, 0))
    in_specs = [tok(D), _mod_spec(l, 0, mod_row)(D), _mod_spec(l, 1, mod_row)(D),
                _layer_spec(gain, l), _layer_spec(w, l), _layer_spec(qg, l), _layer_spec(kg, l),
                _const_spec(bd.shape)]
    args = [x, mods, mods, gain, w, qg, kg, bd]
    if rope:
        tab = pl.BlockSpec((tm, D_Q), lambda b, t: (t, 0))
        in_specs += [tab, tab]
        args += [cos, sin]
    out_shape = [jax.ShapeDtypeStruct((B, T, D_Q), BF16),
                 jax.ShapeDtypeStruct((B, T, D_KV), F32),
                 jax.ShapeDtypeStruct((B, T, D_KV), F32),
                 jax.ShapeDtypeStruct((B, T, D_CONV), F32),
                 jax.ShapeDtypeStruct((B, T, D_RNN), F32),
                 jax.ShapeDtypeStruct((B, T, D_RNN), F32)]
    out_specs = [tok(D_Q), tok(D_KV), tok(D_KV), tok(D_CONV), tok(D_RNN), tok(D_RNN)]
    return pl.pallas_call(
        functools.partial(_inproj_kernel, rope=rope),
        grid=(B, T // tm),
        in_specs=in_specs, out_specs=out_specs, out_shape=out_shape,
        compiler_params=_params("arbitrary", "arbitrary"),
        name="inproj_rope" if rope else "inproj",
    )(*args)


_VT_ROWS = HEAD_DIM + 2 * SUBLANES


def _attn_kernel(*refs, seg_lens):
    nseg = len(seg_lens)
    q_ref = refs[0]
    kv_refs = refs[1:1 + 2 * nseg]
    o_ref = refs[1 + 2 * nseg]
    kq_ref, vt_ref, p_ref = refs[2 + 2 * nseg:]
    m_tot = sum(seg_lens)
    g = pl.program_id(1)

    @pl.when(pl.program_id(2) == 0)
    def _():
        off = 0
        for i, ms in enumerate(seg_lens):
            lane_half = lax.broadcasted_iota(jnp.int32, (ms, LANES), 1) // HEAD_DIM
            k = kv_refs[2 * i][...]
            kd = jnp.where(lane_half == g, k, pltpu.roll(k, HEAD_DIM, 1)).astype(BF16)
            kq_ref[off:off + ms, :] = jnp.concatenate([kd, kd], axis=1)
            v = kv_refs[2 * i + 1][...]
            vd = jnp.where(lane_half == g, v, pltpu.roll(v, HEAD_DIM, 1))
            vt_ref[0:HEAD_DIM, off:off + ms] = vd.T[0:HEAD_DIM, :].astype(BF16)
            off += ms
        vt_ref[HEAD_DIM:, :] = jnp.ones((_VT_ROWS - HEAD_DIM, m_tot), BF16)

    q = q_ref[...]
    head_lane = lax.broadcasted_iota(jnp.int32, q.shape, 1) // HEAD_DIM
    def scores(h):
        qh = jnp.where(head_lane == h, q, jnp.zeros_like(q))
        return lax.dot_general(kq_ref[...], qh, (((1,), (1,)), ((), ())), preferred_element_type=F32)

    outs = []
    st_next = scores(0)
    for h in range(Q_GROUP):
        st = st_next
        if h + 1 < Q_GROUP:
            st_next = scores(h + 1)
        p_ref[h] = jnp.exp2(st - jnp.max(st, axis=0, keepdims=True)).astype(BF16)
        ot = _mm(vt_ref[...], p_ref[h])
        outs.append(ot[0:HEAD_DIM, :] * (1.0 / ot[HEAD_DIM:HEAD_DIM + 1, :]))
    for c in range(Q_GROUP // 2):
        pair = jnp.concatenate([outs[2 * c], outs[2 * c + 1]], axis=0)
        o_ref[:, c * LANES:(c + 1) * LANES] = pair.T.astype(BF16)


def _attn_call(q, segs, *, tq):
    B, T, _ = q.shape
    seg_lens = tuple(k.shape[1] for k, _ in segs)
    m_tot = sum(seg_lens)
    gw = Q_GROUP * HEAD_DIM
    in_specs = [pl.BlockSpec((None, tq, gw), lambda b, g, t: (b, t, g))]
    args = [q]
    for k, v in segs:
        spec = pl.BlockSpec((None, k.shape[1], D_KV), lambda b, g, t: (b, 0, 0))
        in_specs += [spec, spec]
        args += [k, v]
    return pl.pallas_call(
        functools.partial(_attn_kernel, seg_lens=seg_lens),
        grid=(B, N_KV_HEADS, T // tq),
        in_specs=in_specs,
        out_specs=pl.BlockSpec((None, tq, gw), lambda b, g, t: (b, t, g)),
        out_shape=jax.ShapeDtypeStruct((B, T, D_Q), BF16),
        scratch_shapes=[pltpu.VMEM((m_tot, gw), BF16),
                        pltpu.VMEM((_VT_ROWS, m_tot), BF16),
                        pltpu.VMEM((Q_GROUP, m_tot, tq), BF16)],
        compiler_params=_params("arbitrary", "arbitrary", "arbitrary"),
        name="gqa_attention",
    )(*args)


_CONV_HALO = 16 * SEG
_CONV_ROWS = 32


def _conv_kernel(up_ref, u_ref, un_ref, w_ref, b_ref, lg_ref, lb_ref, o_ref, pad_ref):
    tm = u_ref.shape[0]
    t = pl.program_id(1)
    pad_ref[0:_CONV_HALO, :] = up_ref[...]
    pad_ref[_CONV_HALO:_CONV_HALO + tm, :] = u_ref[...]
    pad_ref[_CONV_HALO + tm:, :] = un_ref[...]

    @pl.when(t == 0)
    def _():
        pad_ref[0:_CONV_HALO, :] = _seam_prev(up_ref[...])

    @pl.when(t == pl.num_programs(1) - 1)
    def _():
        pad_ref[_CONV_HALO + tm:, :] = _seam_next(un_ref[...])

    first = _CONV_HALO // SEG - CONV_K // 2

    def body(c, carry):
        r = pl.multiple_of(c * _CONV_ROWS, _CONV_ROWS)
        acc = jnp.zeros((_CONV_ROWS, D_CONV), F32)
        for j in range(CONV_K):
            acc = acc + pad_ref[pl.ds(r + (first + j) * SEG, _CONV_ROWS), :] * w_ref[j:j + 1, :]
        acc = acc + b_ref[...]
        mu = jnp.mean(acc, axis=-1, keepdims=True)
        cen = acc - mu
        var = jnp.mean(cen * cen, axis=-1, keepdims=True)
        y = cen * lax.rsqrt(var + EPS) * lg_ref[...] + lb_ref[...]
        o_ref[pl.ds(r, _CONV_ROWS), :] = (y * jax.nn.sigmoid(y)).astype(BF16)
        return carry

    lax.fori_loop(0, tm // _CONV_ROWS, body, 0, unroll=2)


def _conv_call(u, l, w, b, lg, lb, *, tm):
    B, T, C = u.shape
    prev, nxt = _halo_specs(tm, _CONV_HALO, C, T)
    tile = pl.BlockSpec((None, tm, C), lambda b_, t: (b_, t, 0))
    return pl.pallas_call(
        _conv_kernel,
        grid=(B, T // tm),
        in_specs=[prev, tile, nxt, _layer_spec(w, l), _layer_spec(b, l), _layer_spec(lg, l),
                  _layer_spec(lb, l)],
        out_specs=tile,
        out_shape=jax.ShapeDtypeStruct((B, T, C), BF16),
        scratch_shapes=[pltpu.VMEM((tm + 2 * _CONV_HALO, C), F32)],
        compiler_params=_params("arbitrary", "arbitrary"),
        name="conformer_conv",
    )(u, u, u, w, b, lg, lb)


_RNN_HALO = 4 * SEG


def _rnn_kernel(zx_ref, h0_ref, cw_ref, cb_ref, w_ref, b_ref, lam_ref, h_ref, hl_ref,
                xp_ref, a_ref, bb_ref, ac_ref, *, chunk):
    T = zx_ref.shape[0]
    n_chunks = T // chunk
    steps = chunk // SEG

    xp_ref[0:_RNN_HALO, :] = _seam_prev(zx_ref[T - _RNN_HALO:T, :])
    xp_ref[_RNN_HALO + T:, :] = _seam_next(zx_ref[0:_RNN_HALO, :])

    def copy(i, carry):
        r = pl.multiple_of(i * chunk, chunk)
        xp_ref[pl.ds(_RNN_HALO + r, chunk), :] = zx_ref[pl.ds(r, chunk), :]
        return carry

    lax.fori_loop(0, n_chunks, copy, 0)

    lam = lam_ref[...]
    log_sig = jnp.minimum(lam, 0.0) - jnp.log1p(jnp.exp(-jnp.abs(lam)))
    c_log = LRU_C * log_sig
    row = lax.broadcasted_iota(jnp.int32, (SEG, D_RNN), 0)

    def run(reverse):
        def chunk_body(c, carry):
            cc = (n_chunks - 1 - c) if reverse else c
            r0 = pl.multiple_of(cc * chunk, chunk)
            xc = jnp.zeros((chunk, D_RNN), F32)
            for j in range(RNN_CONV_K):
                step = (RNN_CONV_K - 1 - j) if reverse else (j - RNN_CONV_K + 1)
                xc = xc + xp_ref[pl.ds(r0 + _RNN_HALO + step * SEG, chunk), :] * cw_ref[j:j + 1, :]
            xc = xc + cb_ref[...]
            gates = _mm(xc.astype(BF16), w_ref[...]) + b_ref[...]
            r_gate = jax.nn.sigmoid(gates[:, :D_RNN])
            i_gate = jax.nn.sigmoid(gates[:, D_RNN:])
            a = jnp.exp(c_log * r_gate)
            a_ref[...] = a
            bb_ref[...] = jnp.sqrt((1.0 - a) * (1.0 + a)) * (i_gate * xc)

            def step_body(k, state):
                a_cum, h_zero = state
                kk = (steps - 1 - k) if reverse else k
                t0 = pl.multiple_of(kk * SEG, SEG)
                at = a_ref[pl.ds(t0, SEG), :]
                h_zero = at * h_zero + bb_ref[pl.ds(t0, SEG), :]
                a_cum = at * a_cum
                h_ref[pl.ds(r0 + t0, SEG), :] = h_zero
                ac_ref[pl.ds(r0 + t0, SEG), :] = a_cum
                return a_cum, h_zero

            return lax.fori_loop(0, steps, step_body, carry, unroll=8)

        a_end, h_end = lax.fori_loop(0, n_chunks, chunk_body,
                                     (jnp.ones((SEG, D_RNN), F32), jnp.zeros((SEG, D_RNN), F32)))

        start = SEG - 1 if reverse else 0
        state_in = jnp.where(row == start, h0_ref[...], 0.0)
        for i in range(1, SEG):
            s = start - i if reverse else start + i
            leaving = h_end + a_end * state_in
            state_in = jnp.where(row == s, pltpu.roll(leaving, SEG - 1 if reverse else 1, 0), state_in)

        def fix(k, carry):
            t0 = pl.multiple_of(k * SEG, SEG)
            h_ref[pl.ds(t0, SEG), :] = h_ref[pl.ds(t0, SEG), :] + ac_ref[pl.ds(t0, SEG), :] * state_in
            return carry

        lax.fori_loop(0, T // SEG, fix, 0, unroll=8)
        final = h_ref[0:SEG, :][0:1, :] if reverse else h_ref[T - SEG:T, :][SEG - 1:SEG, :]
        hl_ref[...] = jnp.broadcast_to(final, (SEG, D_RNN))

    d = pl.program_id(1)
    pl.when(d == 0)(lambda: run(False))
    pl.when(d == 1)(lambda: run(True))


def _rnn_call(zx, h0, l, cw, cb, w, b, lam, *, chunk):
    B, T, C = zx.shape
    per_dir = lambda a: pl.BlockSpec((None, None) + tuple(a.shape[2:]), lambda b_, d: (l, d, 0, 0))
    state = pl.BlockSpec((None, None, SEG, C), lambda b_, d: (b_, d, 0, 0))
    return pl.pallas_call(
        functools.partial(_rnn_kernel, chunk=chunk),
        grid=(B, 2),
        in_specs=[pl.BlockSpec((None, T, C), lambda b_, d: (b_, 0, 0)), state,
                  per_dir(cw), per_dir(cb), per_dir(w), per_dir(b), per_dir(lam)],
        out_specs=[pl.BlockSpec((None, None, T, C), lambda b_, d: (b_, d, 0, 0)), state],
        out_shape=[jax.ShapeDtypeStruct((B, 2, T, C), F32),
                   jax.ShapeDtypeStruct((B, 2, SEG, C), F32)],
        scratch_shapes=[pltpu.VMEM((T + 2 * _RNN_HALO, C), F32),
                        pltpu.VMEM((chunk, C), F32), pltpu.VMEM((chunk, C), F32),
                        pltpu.VMEM((T, C), F32)],
        compiler_params=_params("arbitrary", "arbitrary"),
        name="rglru",
    )(zx, h0, cw, cb, w, b, lam)


def _merge_kernel(x_ref, sh_ref, sc_ref, gt_ref, gpre_ref, gpost_ref, wg_ref, at_ref, cv_ref,
                  hr_ref, zr_ref, wao_ref, wco_ref, wro_ref, wout_ref, o_ref):
    D = x_ref.shape[-1]
    x = x_ref[...]
    h = _modulated(x, gpre_ref[...], sh_ref[...], sc_ref[...]).astype(BF16)
    rnn = ((hr_ref[0] + hr_ref[1]) * jax.nn.gelu(zr_ref[...])).astype(BF16)
    branches = ((at_ref[...], wao_ref), (cv_ref[...], wco_ref), (rnn, wro_ref))
    m = None
    for i, (act, w_ref) in enumerate(branches):
        gate = jax.nn.sigmoid(_mm(h, wg_ref[:, i * D:(i + 1) * D]))
        term = gate * _mm(act, w_ref[...])
        m = term if m is None else m + term
    o = _mm(m.astype(BF16), wout_ref[...])
    o_ref[...] = x + gt_ref[...] * _rms(o, gpost_ref[...])


def _merge_call(x, mods, l, mod_row, gpre, gpost, wg, attn, conv, hr, zr, wao, wco, wro, wout, *, tm):
    B, T, D = x.shape
    tok = lambda w_: pl.BlockSpec((None, tm, w_), lambda b, t: (b, t, 0))
    return pl.pallas_call(
        _merge_kernel,
        grid=(B, T // tm),
        in_specs=[tok(D), _mod_spec(l, 0, mod_row)(D), _mod_spec(l, 1, mod_row)(D),
                  _mod_spec(l, 2, mod_row)(D), _layer_spec(gpre, l), _layer_spec(gpost, l),
                  _layer_spec(wg, l), tok(D_Q), tok(D_CONV),
                  pl.BlockSpec((None, 2, tm, D_RNN), lambda b, t: (b, 0, t, 0)), tok(D_RNN),
                  _layer_spec(wao, l), _layer_spec(wco, l), _layer_spec(wro, l), _layer_spec(wout, l)],
        out_specs=tok(D),
        out_shape=jax.ShapeDtypeStruct((B, T, D), F32),
        compiler_params=_params("arbitrary", "arbitrary"),
        name="merge",
    )(x, mods, mods, mods, gpre, gpost, wg, attn, conv, hr, zr, wao, wco, wro, wout)


_FFN_CHUNK = 512


def _ffn_kernel(xp_ref, x_ref, xn_ref, sh_ref, sc_ref, gt_ref, gpre_ref, gpost_ref, up_ref,
                dw_ref, db_ref, down_ref, o_ref, hp_ref, hn_ref):
    tm = x_ref.shape[0]
    t = pl.program_id(1)
    x = x_ref[...]
    mod = lambda v: _modulated(v, gpre_ref[...], sh_ref[...], sc_ref[...])
    hp_ref[...] = mod(xp_ref[...])
    hn_ref[...] = mod(xn_ref[...])

    @pl.when(t == 0)
    def _():
        hp_ref[...] = _seam_prev(hp_ref[...])

    @pl.when(t == pl.num_programs(1) - 1)
    def _():
        hn_ref[...] = _seam_next(hn_ref[...])

    h = jnp.concatenate([hp_ref[...], mod(x), hn_ref[...]], axis=0).astype(BF16)

    acc = jnp.zeros(x.shape, F32)
    for c in range(D_FF // _FFN_CHUNK):
        cg = slice(c * _FFN_CHUNK, (c + 1) * _FFN_CHUNK)
        cv = slice(D_FF + c * _FFN_CHUNK, D_FF + (c + 1) * _FFN_CHUNK)
        ug = _mm(h, up_ref[:, cg])
        uv = _mm(h, up_ref[:, cv])
        yg = db_ref[:, cg]
        yv = db_ref[:, cv]
        for j in range(3):
            rows = slice(j * SEG, j * SEG + tm)
            yg = yg + ug[rows, :] * dw_ref[j:j + 1, cg]
            yv = yv + uv[rows, :] * dw_ref[j:j + 1, cv]
        act = (jax.nn.gelu(yg) * yv).astype(BF16)
        acc = acc + _mm(act, down_ref[cg, :])
    o_ref[...] = x + gt_ref[...] * _rms(acc, gpost_ref[...])


def _ffn_call(x, mods, l, mod_row, gpre, gpost, up, dw, db, down, *, tm):
    B, T, D = x.shape
    prev, nxt = _halo_specs(tm, SEG, D, T)
    return pl.pallas_call(
        _ffn_kernel,
        grid=(B, T // tm),
        in_specs=[prev, pl.BlockSpec((None, tm, D), lambda b, t: (b, t, 0)), nxt,
                  _mod_spec(l, 3, mod_row)(D), _mod_spec(l, 4, mod_row)(D), _mod_spec(l, 5, mod_row)(D),
                  _layer_spec(gpre, l), _layer_spec(gpost, l), _layer_spec(up, l),
                  _layer_spec(dw, l), _layer_spec(db, l), _layer_spec(down, l)],
        out_specs=pl.BlockSpec((None, tm, D), lambda b, t: (b, t, 0)),
        out_shape=jax.ShapeDtypeStruct((B, T, D), F32),
        scratch_shapes=[pltpu.VMEM((SEG, D), F32), pltpu.VMEM((SEG, D), F32)],
        compiler_params=_params("arbitrary", "arbitrary"),
        name="conv_ffn",
    )(x, x, x, mods, mods, mods, gpre, gpost, up, dw, db, down)


def _seg_interleave(a):
    *lead, T, W = a.shape
    return a.reshape(*lead, SEG, T // SEG, W).swapaxes(-3, -2).reshape(*lead, T, W)


def _seg_deinterleave(a):
    *lead, T, W = a.shape
    return a.reshape(*lead, T // SEG, SEG, W).swapaxes(-3, -2).reshape(*lead, T, W)


def _rope_tables(n):
    rows = n // GRID_W
    row = jnp.repeat(jnp.arange(rows), GRID_W).astype(F32)
    col = jnp.tile(jnp.arange(GRID_W), rows).astype(F32)
    n_freq = HEAD_DIM // 4
    freq = ROPE_THETA ** (-jnp.arange(n_freq, dtype=F32) / n_freq)
    ang = jnp.concatenate([row[:, None] * freq, col[:, None] * freq], axis=-1)
    cos, sin = jnp.cos(ang), jnp.sin(ang)
    cos_t = jnp.tile(jnp.concatenate([cos, cos], axis=-1), (1, N_Q_HEADS))
    sin_t = jnp.tile(jnp.concatenate([-sin, sin], axis=-1), (1, N_Q_HEADS))
    return cos_t, sin_t


def _block_diag(w):
    *lead, nb, bi, bj = w.shape
    eye = jnp.eye(nb, dtype=w.dtype)
    return jnp.einsum('...hij,hg->...higj', w, eye).reshape(*lead, nb * bi, nb * bj)


def kernel(x, c, ctx, c_ctx, w_mod, b_mod, norm_pre_mix, norm_post_mix, norm_pre_ffn, norm_post_ffn,
           w_in, q_norm, k_norm, w_attn_out, conv_dw, conv_dw_b, conv_ln_g, conv_ln_b, w_conv_out,
           rnn_conv_w, rnn_conv_b, rnn_wa, rnn_ba, rnn_wx, rnn_bx, rnn_lambda, w_rnn_out, w_out,
           ffn_up, ffn_dw, ffn_dw_b, ffn_down):
    B, N, D = x.shape
    M = ctx.shape[1]
    L = w_mod.shape[0]
    tm_lat = min(N, 512)
    tm_ctx = min(M, 256)
    chunk_lat = min(N, 256)
    chunk_ctx = min(M, 256)

    mod_rows = 2 * SUBLANES
    cc = jnp.concatenate([c, c_ctx[None, :], jnp.zeros((mod_rows - B - 1, D), F32)], axis=0)
    mods = _mod_call(cc, w_mod, b_mod[:, None, :]).reshape(L, mod_rows, N_MOD, 1, D)

    cos_t, sin_t = (_seg_interleave(t) for t in _rope_tables(N))
    head_id = jnp.arange(D_Q) // HEAD_DIM
    bd = (head_id[:, None] == head_id[None, :]).astype(BF16)
    h0 = jnp.zeros((B, 2, SEG, D_RNN), F32)

    vec = lambda a: a[:, None, :]
    w_in_bf = w_in.astype(BF16)
    w_proj, w_gate = w_in_bf[:, :, :_OFF_GATE], w_in_bf[:, :, _OFF_GATE:]
    qg = vec(jnp.tile(q_norm, (1, N_Q_HEADS)))
    kg = vec(jnp.tile(k_norm, (1, N_KV_HEADS)))
    rnn_w = jnp.concatenate([_block_diag(rnn_wa), _block_diag(rnn_wx)], axis=-1).astype(BF16)
    rnn_b = jnp.concatenate([rnn_ba, rnn_bx], axis=-1)[:, :, None, :]
    rnn_cb = rnn_conv_b[:, :, None, :]
    lam = rnn_lambda[:, :, None, :]
    wao, wco, wro, wout = (w_attn_out.astype(BF16), w_conv_out.astype(BF16), w_rnn_out.astype(BF16),
                           w_out.astype(BF16))
    up, down = ffn_up.astype(BF16), ffn_down.astype(BF16)
    gpre_mix, gpost_mix, gpre_ffn, gpost_ffn = (vec(norm_pre_mix), vec(norm_post_mix), vec(norm_pre_ffn),
                                                vec(norm_post_ffn))
    cdb, clg, clb, fdb = vec(conv_dw_b), vec(conv_ln_g), vec(conv_ln_b), vec(ffn_dw_b)

    x = _seg_interleave(x)
    ctx = _seg_interleave(ctx)
    for l in range(L):
        need_ctx = l < L - 1
        qc, kc, vc, uc, zxc, zrc = _inproj_call(ctx, mods, l, B, gpre_mix, w_proj, qg, kg, bd, None, None,
                                                tm=tm_ctx)
        hc, hc_last = _rnn_call(zxc, h0, l, rnn_conv_w, rnn_cb, rnn_w, rnn_b, lam, chunk=chunk_ctx)

        q, k, v, u, zx, zr = _inproj_call(x, mods, l, None, gpre_mix, w_proj, qg, kg, bd, cos_t, sin_t,
                                          tm=tm_lat)
        attn = _attn_call(q, [(kc, vc), (k, v)], tq=tm_lat)
        conv = _conv_call(u, l, conv_dw, cdb, clg, clb, tm=tm_lat)
        hr, _ = _rnn_call(zx, hc_last, l, rnn_conv_w, rnn_cb, rnn_w, rnn_b, lam, chunk=chunk_lat)
        x = _merge_call(x, mods, l, None, gpre_mix, gpost_mix, w_gate, attn, conv, hr, zr,
                        wao, wco, wro, wout, tm=tm_lat)
        x = _ffn_call(x, mods, l, None, gpre_ffn, gpost_ffn, up, ffn_dw, fdb, down, tm=tm_lat)

        if need_ctx:
            attn_c = _attn_call(qc, [(kc, vc)], tq=tm_ctx)
            conv_c = _conv_call(uc, l, conv_dw, cdb, clg, clb, tm=tm_ctx)
            ctx = _merge_call(ctx, mods, l, B, gpre_mix, gpost_mix, w_gate, attn_c, conv_c, hc, zrc,
                              wao, wco, wro, wout, tm=tm_ctx)
            ctx = _ffn_call(ctx, mods, l, B, gpre_ffn, gpost_ffn, up, ffn_dw, fdb, down, tm=tm_ctx)
    return _seg_deinterleave(x)
```

```python
import functools
import math

import jax
import jax.numpy as jnp
from jax import lax
from jax.experimental import pallas as pl
from jax.experimental.pallas import tpu as pltpu

F32 = jnp.float32
BF16 = jnp.bfloat16

EPS = 1e-6
HEAD_DIM = 64
N_Q_HEADS = 8
N_KV_HEADS = 2
Q_GROUP = N_Q_HEADS // N_KV_HEADS
D_Q = N_Q_HEADS * HEAD_DIM
D_KV = N_KV_HEADS * HEAD_DIM
D_CONV = 512
CONV_K = 31
D_RNN = 512
RNN_CONV_K = 4
LRU_C = 8.0
D_FF = 3072
GRID_W = 64
ROPE_THETA = 10000.0
N_MOD = 6

LANES = 128
SUBLANES = 8
SEG = SUBLANES
VMEM_LIMIT = 56 * 1024 * 1024

_OFF_Q = 0
_OFF_KV = D_Q
_OFF_CONV = _OFF_KV + 2 * D_KV
_OFF_ZX = _OFF_CONV + 2 * D_CONV
_OFF_ZR = _OFF_ZX + D_RNN
_OFF_GATE = _OFF_ZR + D_RNN

_Q_SCALE = HEAD_DIM ** -0.5 * math.log2(math.e)


def _const_spec(shape):
    zeros = (0,) * len(shape)
    return pl.BlockSpec(tuple(shape), lambda *_: zeros, pipeline_mode=pl.Buffered(1))


def _layer_spec(arr, l):
    tail = (0,) * (arr.ndim - 1)
    return pl.BlockSpec((None,) + tuple(arr.shape[1:]), lambda *_: (l,) + tail,
                        pipeline_mode=pl.Buffered(1))


def _mod_spec(l, k, row):
    def index(b, *_):
        return (l, b if row is None else row, k, 0, 0)
    return lambda d: pl.BlockSpec((None, None, None, 1, d), index)


def _params(*sem, flags=None):
    return pltpu.CompilerParams(dimension_semantics=sem, vmem_limit_bytes=VMEM_LIMIT, flags=flags)


def _rms(x, g):
    return x * lax.rsqrt(jnp.mean(x * x, axis=-1, keepdims=True) + EPS) * g


def _mm(a, b):
    return jnp.dot(a, b, preferred_element_type=F32)


def _modulated(x, g, shift, scale):
    return _rms(x, g) * (1.0 + scale) + shift


def _seam_prev(a):
    sub = lax.broadcasted_iota(jnp.int32, a.shape, 0) % SEG
    return jnp.where(sub == 0, 0.0, pltpu.roll(a, 1, 0))


def _seam_next(a):
    sub = lax.broadcasted_iota(jnp.int32, a.shape, 0) % SEG
    return jnp.where(sub == SEG - 1, 0.0, pltpu.roll(a, a.shape[0] - 1, 0))


def _time_order_spec(tm, width):
    return pl.BlockSpec((None, SEG, tm // SEG, width), lambda b, t: (b, 0, t, 0))


def _slab_scratch(tm, width):
    return pltpu.VMEM((width // LANES, tm, LANES), F32)


def _load_interleaved(x_ref, slab_ref):
    seg, steps, width = x_ref.shape
    for j in range(width // LANES):
        for s in range(seg):
            slab_ref[j, pl.ds(s, steps, stride=seg), :] = x_ref[s, :, j * LANES:(j + 1) * LANES]


def _read_slabs(slab_ref):
    return jnp.concatenate([slab_ref[j] for j in range(slab_ref.shape[0])], axis=1)


def _store_time_order(y, o_ref, slab_ref):
    seg, steps, width = o_ref.shape
    for j in range(width // LANES):
        slab_ref[j] = y[:, j * LANES:(j + 1) * LANES]
    for j in range(width // LANES):
        for s in range(seg):
            o_ref[s, :, j * LANES:(j + 1) * LANES] = slab_ref[j, pl.ds(s, steps, stride=seg), :]


def _halo_specs(tm, halo, width, n_rows):
    per = tm // halo
    n_blocks = n_rows // halo
    prev = pl.BlockSpec((None, halo, width), lambda b, t: (b, (t * per + n_blocks - 1) % n_blocks, 0))
    nxt = pl.BlockSpec((None, halo, width), lambda b, t: (b, ((t + 1) * per) % n_blocks, 0))
    return prev, nxt


def _mod_kernel(c_ref, w_ref, b_ref, o_ref):
    c = c_ref[...]
    a = (c * jax.nn.sigmoid(c)).astype(BF16)
    o_ref[...] = _mm(a, w_ref[...].astype(BF16)) + b_ref[...]


def _mod_call(cc, w_mod, b_mod):
    L, D, D6 = w_mod.shape
    R = cc.shape[0]
    bn = D6 // 4
    return pl.pallas_call(
        _mod_kernel,
        grid=(L, D6 // bn),
        in_specs=[
            pl.BlockSpec((R, D), lambda l, j: (0, 0)),
            pl.BlockSpec((None, D, bn), lambda l, j: (l, 0, j)),
            pl.BlockSpec((None, 1, bn), lambda l, j: (l, 0, j)),
        ],
        out_specs=pl.BlockSpec((None, R, bn), lambda l, j: (l, 0, j)),
        out_shape=jax.ShapeDtypeStruct((L, R, D6), F32),
        compiler_params=_params("arbitrary", "arbitrary"),
        name="adaln_mod",
    )(cc, w_mod, b_mod)


def _head_rms(z, gain, bd):
    sq = z * z
    hi = sq.astype(BF16)
    lo = (sq - hi.astype(F32)).astype(BF16)
    ms = (_mm(hi, bd) + _mm(lo, bd)) * (1.0 / HEAD_DIM)
    return z * lax.rsqrt(ms + EPS) * gain


def _rope(x, c, s):
    rows, width = x.shape
    lane = lax.broadcasted_iota(jnp.int32, (rows, LANES), 1)
    first_half = (lane & (HEAD_DIM // 2)) == 0
    cols = []
    for j in range(width // LANES):
        sl = slice(j * LANES, (j + 1) * LANES)
        xc = x[:, sl]
        partner = jnp.where(first_half,
                            pltpu.roll(xc, LANES - HEAD_DIM // 2, 1),
                            pltpu.roll(xc, HEAD_DIM // 2, 1))
        cols.append(xc * c[:, sl] + partner * s[:, sl])
    return jnp.concatenate(cols, axis=1) if len(cols) > 1 else cols[0]


def _inproj_kernel(*refs, rope, time_order):
    x_ref, sh_ref, sc_ref, g_ref, w_ref, qg_ref, kg_ref, bd_ref = refs[:8]
    refs = refs[8:]
    if rope:
        cos_ref, sin_ref = refs[:2]
        refs = refs[2:]
    h_ref, q_ref, k_ref, v_ref, u_ref, zx_ref, zr_ref = refs[:7]
    if time_order:
        _load_interleaved(x_ref, refs[7])
        x = _read_slabs(refs[7])
    else:
        x = x_ref[...]
    h = _modulated(x, g_ref[...], sh_ref[...], sc_ref[...]).astype(BF16)
    h_ref[...] = h

    zq = _mm(h, w_ref[:, _OFF_Q:_OFF_Q + D_Q])
    qn = _head_rms(zq, qg_ref[...], bd_ref[...])
    if rope:
        qn = _rope(qn, cos_ref[...], sin_ref[...])
    q_ref[...] = (qn * _Q_SCALE).astype(BF16)

    zkv = _mm(h, w_ref[:, _OFF_KV:_OFF_KV + 2 * D_KV])
    kn = _head_rms(zkv[:, :D_KV], kg_ref[...], bd_ref[:D_KV, :D_KV])
    if rope:
        kn = _rope(kn, cos_ref[:, :D_KV], sin_ref[:, :D_KV])
    k_ref[...] = kn
    v_ref[...] = zkv[:, D_KV:]

    zc = _mm(h, w_ref[:, _OFF_CONV:_OFF_CONV + 2 * D_CONV])
    u_ref[...] = zc[:, :D_CONV] * jax.nn.sigmoid(zc[:, D_CONV:])

    zx_ref[...] = _mm(h, w_ref[:, _OFF_ZX:_OFF_ZX + D_RNN])
    zr_ref[...] = _mm(h, w_ref[:, _OFF_ZR:_OFF_ZR + D_RNN])


def _inproj_call(x, mods, l, mod_row, gain, w, qg, kg, bd, cos, sin, *, tm):
    time_order = x.ndim == 4
    B, D = x.shape[0], x.shape[-1]
    T = x.shape[1] * x.shape[2] if time_order else x.shape[1]
    rope = cos is not None
    tok = lambda w_: pl.BlockSpec((None, tm, w_), lambda b, t: (b, t, 0))
    x_spec = _time_order_spec(tm, D) if time_order else tok(D)
    in_specs = [x_spec, _mod_spec(l, 0, mod_row)(D), _mod_spec(l, 1, mod_row)(D),
                _layer_spec(gain, l), _layer_spec(w, l), _layer_spec(qg, l), _layer_spec(kg, l),
                _const_spec(bd.shape)]
    args = [x, mods, mods, gain, w, qg, kg, bd]
    if rope:
        tab = pl.BlockSpec((tm, D_Q), lambda b, t: (t, 0))
        in_specs += [tab, tab]
        args += [cos, sin]
    out_shape = [jax.ShapeDtypeStruct((B, T, D), BF16),
                 jax.ShapeDtypeStruct((B, T, D_Q), BF16),
                 jax.ShapeDtypeStruct((B, T, D_KV), F32),
                 jax.ShapeDtypeStruct((B, T, D_KV), F32),
                 jax.ShapeDtypeStruct((B, T, D_CONV), F32),
                 jax.ShapeDtypeStruct((B, T, D_RNN), F32),
                 jax.ShapeDtypeStruct((B, T, D_RNN), F32)]
    out_specs = [tok(D), tok(D_Q), tok(D_KV), tok(D_KV), tok(D_CONV), tok(D_RNN), tok(D_RNN)]
    return pl.pallas_call(
        functools.partial(_inproj_kernel, rope=rope, time_order=time_order),
        grid=(B, T // tm),
        in_specs=in_specs, out_specs=out_specs, out_shape=out_shape,
        scratch_shapes=[_slab_scratch(tm, D)] if time_order else [],
        compiler_params=_params("arbitrary", "arbitrary"),
        name="inproj_rope" if rope else "inproj",
    )(*args)


_VT_ROWS = HEAD_DIM + 2 * SUBLANES
_ATTN_KEYS = 256


def _attn_kernel(*refs, seg_lens):
    nseg = len(seg_lens)
    q_ref = refs[0]
    kv_refs = refs[1:1 + 2 * nseg]
    o_ref = refs[1 + 2 * nseg]
    kq_ref, vt_ref = refs[2 + 2 * nseg:]
    m_tot = sum(seg_lens)
    g = pl.program_id(1)

    @pl.when(pl.program_id(2) == 0)
    def _():
        off = 0
        for i, ms in enumerate(seg_lens):
            lane_half = lax.broadcasted_iota(jnp.int32, (ms, LANES), 1) // HEAD_DIM
            k = kv_refs[2 * i][...]
            kd = jnp.where(lane_half == g, k, pltpu.roll(k, HEAD_DIM, 1)).astype(BF16)
            kq_ref[off:off + ms, :] = jnp.concatenate([kd, kd], axis=1)
            v = kv_refs[2 * i + 1][...]
            vd = jnp.where(lane_half == g, v, pltpu.roll(v, HEAD_DIM, 1))
            vt_ref[0:HEAD_DIM, off:off + ms] = vd.T[0:HEAD_DIM, :].astype(BF16)
            off += ms
        vt_ref[HEAD_DIM:, :] = jnp.ones((_VT_ROWS - HEAD_DIM, m_tot), BF16)

    q = q_ref[...]
    head_lane = lax.broadcasted_iota(jnp.int32, q.shape, 1) // HEAD_DIM
    q_heads = [jnp.where(head_lane == h, q, jnp.zeros_like(q)) for h in range(Q_GROUP)]
    n_chunks = m_tot // _ATTN_KEYS

    def scores(h):
        return lax.dot_general(kq_ref[...], q_heads[h], (((1,), (1,)), ((), ())),
                               preferred_element_type=F32)

    outs = []
    st_next = scores(0)
    for h in range(Q_GROUP):
        st_all = st_next
        if h + 1 < Q_GROUP:
            st_next = scores(h + 1)
        for c in range(n_chunks):
            rows = slice(c * _ATTN_KEYS, (c + 1) * _ATTN_KEYS)
            st = st_all[rows, :]
            vt = vt_ref[:, rows]
            m_chunk = jnp.max(st, axis=0, keepdims=True)
            if c == 0:
                m_run = m_chunk
                acc = _mm(vt, jnp.exp2(st - m_run).astype(BF16))
            else:
                m_new = jnp.maximum(m_run, m_chunk)
                acc = acc * jnp.exp2(m_run - m_new) + _mm(vt, jnp.exp2(st - m_new).astype(BF16))
                m_run = m_new
        outs.append(acc[0:HEAD_DIM, :] * (1.0 / acc[HEAD_DIM:HEAD_DIM + 1, :]))
    for c in range(Q_GROUP // 2):
        pair = jnp.concatenate([outs[2 * c], outs[2 * c + 1]], axis=0)
        o_ref[:, c * LANES:(c + 1) * LANES] = pair.T.astype(BF16)


def _attn_call(q, segs, *, tq):
    B, T, _ = q.shape
    seg_lens = tuple(k.shape[1] for k, _ in segs)
    m_tot = sum(seg_lens)
    gw = Q_GROUP * HEAD_DIM
    in_specs = [pl.BlockSpec((None, tq, gw), lambda b, g, t: (b, t, g))]
    args = [q]
    for k, v in segs:
        spec = pl.BlockSpec((None, k.shape[1], D_KV), lambda b, g, t: (b, 0, 0))
        in_specs += [spec, spec]
        args += [k, v]
    return pl.pallas_call(
        functools.partial(_attn_kernel, seg_lens=seg_lens),
        grid=(B, N_KV_HEADS, T // tq),
        in_specs=in_specs,
        out_specs=pl.BlockSpec((None, tq, gw), lambda b, g, t: (b, t, g)),
        out_shape=jax.ShapeDtypeStruct((B, T, D_Q), BF16),
        scratch_shapes=[pltpu.VMEM((m_tot, gw), BF16),
                        pltpu.VMEM((_VT_ROWS, m_tot), BF16)],
        compiler_params=_params("arbitrary", "arbitrary", "arbitrary"),
        name="gqa_attention",
    )(*args)


_CONV_HALO = 16 * SEG
_CONV_ROWS = 32


def _conv_fill(up_ref, u_ref, un_ref, pad_ref):
    tm = u_ref.shape[0]
    t = pl.program_id(1)
    pad_ref[0:_CONV_HALO, :] = up_ref[...]
    pad_ref[_CONV_HALO:_CONV_HALO + tm, :] = u_ref[...]
    pad_ref[_CONV_HALO + tm:, :] = un_ref[...]

    @pl.when(t == 0)
    def _():
        pad_ref[0:_CONV_HALO, :] = _seam_prev(up_ref[...])

    @pl.when(t == pl.num_programs(1) - 1)
    def _():
        pad_ref[_CONV_HALO + tm:, :] = _seam_next(un_ref[...])


def _conv_rows(pad_ref, w_ref, b_ref, lg_ref, lb_ref, o_ref, row_lo, row_hi):
    first = _CONV_HALO // SEG - CONV_K // 2
    for r in range(row_lo, row_hi, _CONV_ROWS):
        acc = jnp.zeros((_CONV_ROWS, D_CONV), F32)
        for j in range(CONV_K):
            lo = r + (first + j) * SEG
            acc = acc + pad_ref[lo:lo + _CONV_ROWS, :] * w_ref[j:j + 1, :]
        acc = acc + b_ref[...]
        mu = jnp.mean(acc, axis=-1, keepdims=True)
        cen = acc - mu
        var = jnp.mean(cen * cen, axis=-1, keepdims=True)
        y = cen * lax.rsqrt(var + EPS) * lg_ref[...] + lb_ref[...]
        o_ref[r:r + _CONV_ROWS, :] = (y * jax.nn.sigmoid(y)).astype(BF16)


_RNN_HALO = 4 * SEG


_RNN_OUT_ROWS = 2 * SUBLANES


def _rnn_kernel(zx_ref, zr_ref, h0_ref, cw_ref, cb_ref, w_ref, b_ref, lam_ref, o_ref, hl_ref,
                xp_ref, a_ref, bb_ref, ac_ref, h_ref, hf_ref, *, chunk):
    T = zx_ref.shape[0]
    n_chunks = T // chunk
    steps = chunk // SEG

    xp_ref[0:_RNN_HALO, :] = _seam_prev(zx_ref[T - _RNN_HALO:T, :])
    xp_ref[_RNN_HALO + T:, :] = _seam_next(zx_ref[0:_RNN_HALO, :])

    def copy(i, carry):
        r = pl.multiple_of(i * chunk, chunk)
        xp_ref[pl.ds(_RNN_HALO + r, chunk), :] = zx_ref[pl.ds(r, chunk), :]
        return carry

    lax.fori_loop(0, n_chunks, copy, 0)

    lam = lam_ref[...]
    log_sig = jnp.minimum(lam, 0.0) - jnp.log1p(jnp.exp(-jnp.abs(lam)))
    c_log = LRU_C * log_sig
    row = lax.broadcasted_iota(jnp.int32, (SEG, D_RNN), 0)

    def run(reverse):
        def chunk_body(c, carry):
            cc = (n_chunks - 1 - c) if reverse else c
            r0 = pl.multiple_of(cc * chunk, chunk)
            xc = jnp.zeros((chunk, D_RNN), F32)
            for j in range(RNN_CONV_K):
                step = (RNN_CONV_K - 1 - j) if reverse else (j - RNN_CONV_K + 1)
                xc = xc + xp_ref[pl.ds(r0 + _RNN_HALO + step * SEG, chunk), :] * cw_ref[j:j + 1, :]
            xc = xc + cb_ref[...]
            gates = _mm(xc.astype(BF16), w_ref[...]) + b_ref[...]
            r_gate = jax.nn.sigmoid(gates[:, :D_RNN])
            i_gate = jax.nn.sigmoid(gates[:, D_RNN:])
            a = jnp.exp(c_log * r_gate)
            a_ref[...] = a
            bb_ref[...] = jnp.sqrt((1.0 - a) * (1.0 + a)) * (i_gate * xc)

            def step_body(k, state):
                a_cum, h_zero = state
                kk = (steps - 1 - k) if reverse else k
                t0 = pl.multiple_of(kk * SEG, SEG)
                at = a_ref[pl.ds(t0, SEG), :]
                h_zero = at * h_zero + bb_ref[pl.ds(t0, SEG), :]
                a_cum = at * a_cum
                h_ref[pl.ds(r0 + t0, SEG), :] = h_zero
                ac_ref[pl.ds(r0 + t0, SEG), :] = a_cum
                return a_cum, h_zero

            return lax.fori_loop(0, steps, step_body, carry, unroll=8)

        a_end, h_end = lax.fori_loop(0, n_chunks, chunk_body,
                                     (jnp.ones((SEG, D_RNN), F32), jnp.zeros((SEG, D_RNN), F32)))

        start = SEG - 1 if reverse else 0
        state_in = jnp.where(row == start, h0_ref[...], 0.0)
        for i in range(1, SEG):
            s = start - i if reverse else start + i
            leaving = h_end + a_end * state_in
            state_in = jnp.where(row == s, pltpu.roll(leaving, SEG - 1 if reverse else 1, 0), state_in)

        state_rows = jnp.concatenate([state_in] * (_RNN_OUT_ROWS // SEG), axis=0)

        def fix(k, carry):
            t0 = pl.multiple_of(k * _RNN_OUT_ROWS, _RNN_OUT_ROWS)
            rows = pl.ds(t0, _RNN_OUT_ROWS)
            h = h_ref[rows, :] + ac_ref[rows, :] * state_rows
            if reverse:
                o_ref[rows, :] = ((hf_ref[rows, :] + h) * jax.nn.gelu(zr_ref[rows, :])).astype(BF16)
            else:
                hf_ref[rows, :] = h
            return carry

        lax.fori_loop(0, T // _RNN_OUT_ROWS, fix, 0, unroll=4)
        edge = slice(0, SEG) if reverse else slice(T - SEG, T)
        last_rows = h_ref[edge, :] + ac_ref[edge, :] * state_in
        final = last_rows[0:1, :] if reverse else last_rows[SEG - 1:SEG, :]
        hl_ref[...] = jnp.broadcast_to(final, (SEG, D_RNN))

    d = pl.program_id(1)
    pl.when(d == 0)(lambda: run(False))
    pl.when(d == 1)(lambda: run(True))


def _rnn_call(zx, zr, h0, l, cw, cb, w, b, lam, *, chunk):
    B, T, C = zx.shape
    per_dir = lambda a: pl.BlockSpec((None, None) + tuple(a.shape[2:]), lambda b_, d: (l, d, 0, 0))
    state = pl.BlockSpec((None, None, SEG, C), lambda b_, d: (b_, d, 0, 0))
    seq = pl.BlockSpec((None, T, C), lambda b_, d: (b_, 0, 0))
    return pl.pallas_call(
        functools.partial(_rnn_kernel, chunk=chunk),
        grid=(B, 2),
        in_specs=[seq, seq, state, per_dir(cw), per_dir(cb), per_dir(w), per_dir(b), per_dir(lam)],
        out_specs=[seq, state],
        out_shape=[jax.ShapeDtypeStruct((B, T, C), BF16),
                   jax.ShapeDtypeStruct((B, 2, SEG, C), F32)],
        scratch_shapes=[pltpu.VMEM((T + 2 * _RNN_HALO, C), F32),
                        pltpu.VMEM((chunk, C), F32), pltpu.VMEM((chunk, C), F32),
                        pltpu.VMEM((T, C), F32), pltpu.VMEM((T, C), F32), pltpu.VMEM((T, C), F32)],
        compiler_params=_params("arbitrary", "arbitrary"),
        name="rglru",
    )(zx, zr, h0, cw, cb, w, b, lam)


def _merge_kernel(x_ref, h_ref, gt_ref, gpost_ref, wg_ref, at_ref,
                  up_ref, u_ref, un_ref, cw_ref, cb_ref, lg_ref, lb_ref,
                  rn_ref, wao_ref, wco_ref, wro_ref, wout_ref, o_ref, pad_ref, cv_ref, *slab):
    D = x_ref.shape[-1]
    _conv_fill(up_ref, u_ref, un_ref, pad_ref)
    tm = u_ref.shape[0]
    conv = functools.partial(_conv_rows, pad_ref, cw_ref, cb_ref, lg_ref, lb_ref, cv_ref)
    gate = lambda i: jax.nn.sigmoid(_mm(h_ref[...], wg_ref[:, i * D:(i + 1) * D]))
    quarter = tm // 4
    independent = (lambda: _mm(at_ref[...], wao_ref[...]), lambda: gate(0),
                   lambda: _mm(rn_ref[...], wro_ref[...]), lambda: gate(2))
    done, conv_proj = [], []
    for i in range(4):
        done.append(independent[i]())
        conv(i * quarter, (i + 1) * quarter)
        conv_proj.append(_mm(cv_ref[i * quarter:(i + 1) * quarter, :], wco_ref[...]))
    m = done[0] * done[1] + done[2] * done[3] + gate(1) * jnp.concatenate(conv_proj, axis=0)
    o = _mm(m.astype(BF16), wout_ref[...])
    if slab:
        _load_interleaved(x_ref, slab[0])
        x = _read_slabs(slab[0])
    else:
        x = x_ref[...]
    o_ref[...] = x + gt_ref[...] * _rms(o, gpost_ref[...])


def _merge_call(x, h, mods, l, mod_row, gpost, wg, attn, u, cw, cb, lg, lb, rnn, wao, wco, wro, wout, *, tm):
    time_order = x.ndim == 4
    B, D = x.shape[0], x.shape[-1]
    T = attn.shape[1]
    tok = lambda w_: pl.BlockSpec((None, tm, w_), lambda b, t: (b, t, 0))
    u_prev, u_next = _halo_specs(tm, _CONV_HALO, D_CONV, T)
    return pl.pallas_call(
        _merge_kernel,
        grid=(B, T // tm),
        in_specs=[_time_order_spec(tm, D) if time_order else tok(D), tok(D),
                  _mod_spec(l, 2, mod_row)(D), _layer_spec(gpost, l),
                  _layer_spec(wg, l), tok(D_Q),
                  u_prev, tok(D_CONV), u_next, _layer_spec(cw, l), _layer_spec(cb, l),
                  _layer_spec(lg, l), _layer_spec(lb, l), tok(D_RNN),
                  _layer_spec(wao, l), _layer_spec(wco, l), _layer_spec(wro, l), _layer_spec(wout, l)],
        out_specs=tok(D),
        out_shape=jax.ShapeDtypeStruct((B, T, D), F32),
        scratch_shapes=[pltpu.VMEM((tm + 2 * _CONV_HALO, D_CONV), F32), pltpu.VMEM((tm, D_CONV), BF16)]
        + ([_slab_scratch(tm, D)] if time_order else []),
        compiler_params=_params("arbitrary", "arbitrary"),
        name="merge",
    )(x, h, mods, gpost, wg, attn, u, u, u, cw, cb, lg, lb, rnn, wao, wco, wro, wout)


_FFN_CHUNK = 512


def _ffn_kernel(xp_ref, x_ref, xn_ref, sh_ref, sc_ref, gt_ref, gpre_ref, gpost_ref, up_ref,
                dw_ref, db_ref, down_ref, o_ref, hp_ref, hn_ref, *slab):
    tm = x_ref.shape[0]
    t = pl.program_id(1)
    mod = lambda v: _modulated(v, gpre_ref[...], sh_ref[...], sc_ref[...])
    hp_ref[...] = mod(xp_ref[...])
    hn_ref[...] = mod(xn_ref[...])

    @pl.when(t == 0)
    def _():
        hp_ref[...] = _seam_prev(hp_ref[...])

    @pl.when(t == pl.num_programs(1) - 1)
    def _():
        hn_ref[...] = _seam_next(hn_ref[...])

    h = jnp.concatenate([hp_ref[...], mod(x_ref[...]), hn_ref[...]], axis=0).astype(BF16)

    acc = jnp.zeros(x_ref.shape, F32)
    for c in range(D_FF // _FFN_CHUNK):
        cg = slice(c * _FFN_CHUNK, (c + 1) * _FFN_CHUNK)
        cv = slice(D_FF + c * _FFN_CHUNK, D_FF + (c + 1) * _FFN_CHUNK)
        ug = _mm(h, up_ref[:, cg])
        uv = _mm(h, up_ref[:, cv])
        yg = db_ref[:, cg]
        yv = db_ref[:, cv]
        for j in range(3):
            rows = slice(j * SEG, j * SEG + tm)
            yg = yg + ug[rows, :] * dw_ref[j:j + 1, cg]
            yv = yv + uv[rows, :] * dw_ref[j:j + 1, cv]
        act = (jax.nn.gelu(yg) * yv).astype(BF16)
        acc = acc + _mm(act, down_ref[cg, :])
    y = x_ref[...] + gt_ref[...] * _rms(acc, gpost_ref[...])
    if slab:
        _store_time_order(y, o_ref, slab[0])
    else:
        o_ref[...] = y


def _ffn_call(x, mods, l, mod_row, gpre, gpost, up, dw, db, down, *, tm, time_order_out=False):
    B, T, D = x.shape
    prev, nxt = _halo_specs(tm, SEG, D, T)
    if time_order_out:
        out_spec = _time_order_spec(tm, D)
        out_shape = jax.ShapeDtypeStruct((B, SEG, T // SEG, D), F32)
    else:
        out_spec = pl.BlockSpec((None, tm, D), lambda b, t: (b, t, 0))
        out_shape = jax.ShapeDtypeStruct((B, T, D), F32)
    return pl.pallas_call(
        _ffn_kernel,
        grid=(B, T // tm),
        in_specs=[prev, pl.BlockSpec((None, tm, D), lambda b, t: (b, t, 0)), nxt,
                  _mod_spec(l, 3, mod_row)(D), _mod_spec(l, 4, mod_row)(D), _mod_spec(l, 5, mod_row)(D),
                  _layer_spec(gpre, l), _layer_spec(gpost, l), _layer_spec(up, l),
                  _layer_spec(dw, l), _layer_spec(db, l), _layer_spec(down, l)],
        out_specs=out_spec,
        out_shape=out_shape,
        scratch_shapes=[pltpu.VMEM((SEG, D), F32), pltpu.VMEM((SEG, D), F32)]
        + ([_slab_scratch(tm, D)] if time_order_out else []),
        compiler_params=_params("arbitrary", "arbitrary"),
        name="conv_ffn",
    )(x, x, x, mods, mods, mods, gpre, gpost, up, dw, db, down)


def _seg_interleave(a):
    *lead, T, W = a.shape
    return a.reshape(*lead, SEG, T // SEG, W).swapaxes(-3, -2).reshape(*lead, T, W)


def _rope_tables(n):
    rows = n // GRID_W
    row = jnp.repeat(jnp.arange(rows), GRID_W).astype(F32)
    col = jnp.tile(jnp.arange(GRID_W), rows).astype(F32)
    n_freq = HEAD_DIM // 4
    freq = ROPE_THETA ** (-jnp.arange(n_freq, dtype=F32) / n_freq)
    ang = jnp.concatenate([row[:, None] * freq, col[:, None] * freq], axis=-1)
    cos, sin = jnp.cos(ang), jnp.sin(ang)
    cos_t = jnp.tile(jnp.concatenate([cos, cos], axis=-1), (1, N_Q_HEADS))
    sin_t = jnp.tile(jnp.concatenate([-sin, sin], axis=-1), (1, N_Q_HEADS))
    return cos_t, sin_t


def _block_diag(w):
    *lead, nb, bi, bj = w.shape
    eye = jnp.eye(nb, dtype=w.dtype)
    return jnp.einsum('...hij,hg->...higj', w, eye).reshape(*lead, nb * bi, nb * bj)


def kernel(x, c, ctx, c_ctx, w_mod, b_mod, norm_pre_mix, norm_post_mix, norm_pre_ffn, norm_post_ffn,
           w_in, q_norm, k_norm, w_attn_out, conv_dw, conv_dw_b, conv_ln_g, conv_ln_b, w_conv_out,
           rnn_conv_w, rnn_conv_b, rnn_wa, rnn_ba, rnn_wx, rnn_bx, rnn_lambda, w_rnn_out, w_out,
           ffn_up, ffn_dw, ffn_dw_b, ffn_down):
    B, N, D = x.shape
    M = ctx.shape[1]
    L = w_mod.shape[0]
    tm_lat = min(N, 512)
    tm_ctx = min(M, 256)
    chunk_lat = min(N, 256)
    chunk_ctx = min(M, 256)

    mod_rows = 2 * SUBLANES
    cc = jnp.concatenate([c, c_ctx[None, :], jnp.zeros((mod_rows - B - 1, D), F32)], axis=0)
    mods = _mod_call(cc, w_mod, b_mod[:, None, :]).reshape(L, mod_rows, N_MOD, 1, D)

    cos_t, sin_t = (_seg_interleave(t) for t in _rope_tables(N))
    head_id = jnp.arange(D_Q) // HEAD_DIM
    bd = (head_id[:, None] == head_id[None, :]).astype(BF16)
    h0 = jnp.zeros((B, 2, SEG, D_RNN), F32)

    vec = lambda a: a[:, None, :]
    w_in_bf = w_in.astype(BF16)
    w_proj, w_gate = w_in_bf[:, :, :_OFF_GATE], w_in_bf[:, :, _OFF_GATE:]
    qg = vec(jnp.tile(q_norm, (1, N_Q_HEADS)))
    kg = vec(jnp.tile(k_norm, (1, N_KV_HEADS)))
    rnn_w = jnp.concatenate([_block_diag(rnn_wa), _block_diag(rnn_wx)], axis=-1).astype(BF16)
    rnn_b = jnp.concatenate([rnn_ba, rnn_bx], axis=-1)[:, :, None, :]
    rnn_cb = rnn_conv_b[:, :, None, :]
    lam = rnn_lambda[:, :, None, :]
    wao, wco, wro, wout = (w_attn_out.astype(BF16), w_conv_out.astype(BF16), w_rnn_out.astype(BF16),
                           w_out.astype(BF16))
    up, down = ffn_up.astype(BF16), ffn_down.astype(BF16)
    gpre_mix, gpost_mix, gpre_ffn, gpost_ffn = (vec(norm_pre_mix), vec(norm_post_mix), vec(norm_pre_ffn),
                                                vec(norm_post_ffn))
    cdb, clg, clb, fdb = vec(conv_dw_b), vec(conv_ln_g), vec(conv_ln_b), vec(ffn_dw_b)

    x = x.reshape(B, SEG, N // SEG, D)
    ctx = ctx.reshape(B, SEG, M // SEG, D)
    for l in range(L):
        need_ctx = l < L - 1
        hcx, qc, kc, vc, uc, zxc, zrc = _inproj_call(ctx, mods, l, B, gpre_mix, w_proj, qg, kg, bd, None, None,
                                                     tm=tm_ctx)
        rnn_c, hc_last = _rnn_call(zxc, zrc, h0, l, rnn_conv_w, rnn_cb, rnn_w, rnn_b, lam, chunk=chunk_ctx)

        hx, q, k, v, u, zx, zr = _inproj_call(x, mods, l, None, gpre_mix, w_proj, qg, kg, bd, cos_t, sin_t,
                                              tm=tm_lat)
        attn = _attn_call(q, [(kc, vc), (k, v)], tq=tm_lat)
        rnn, _ = _rnn_call(zx, zr, hc_last, l, rnn_conv_w, rnn_cb, rnn_w, rnn_b, lam, chunk=chunk_lat)
        x = _merge_call(x, hx, mods, l, None, gpost_mix, w_gate, attn, u, conv_dw, cdb, clg, clb,
                        rnn, wao, wco, wro, wout, tm=tm_lat)
        x = _ffn_call(x, mods, l, None, gpre_ffn, gpost_ffn, up, ffn_dw, fdb, down, tm=tm_lat,
                      time_order_out=not need_ctx)

        if need_ctx:
            attn_c = _attn_call(qc, [(kc, vc)], tq=tm_ctx)
            ctx = _merge_call(ctx, hcx, mods, l, B, gpost_mix, w_gate, attn_c, uc, conv_dw, cdb, clg, clb,
                              rnn_c, wao, wco, wro, wout, tm=tm_ctx)
            ctx = _ffn_call(ctx, mods, l, B, gpre_ffn, gpost_ffn, up, ffn_dw, fdb, down, tm=tm_ctx)
    return x.reshape(B, N, D)
```

```python
import functools
import math

import jax
import jax.numpy as jnp
from jax import lax
from jax.experimental import pallas as pl
from jax.experimental.pallas import tpu as pltpu

F32 = jnp.float32
BF16 = jnp.bfloat16

EPS = 1e-6
HEAD_DIM = 64
N_Q_HEADS = 8
N_KV_HEADS = 2
Q_GROUP = N_Q_HEADS // N_KV_HEADS
D_Q = N_Q_HEADS * HEAD_DIM
D_KV = N_KV_HEADS * HEAD_DIM
D_CONV = 512
CONV_K = 31
D_RNN = 512
RNN_CONV_K = 4
LRU_C = 8.0
D_FF = 3072
GRID_W = 64
ROPE_THETA = 10000.0
N_MOD = 6

LANES = 128
SUBLANES = 8
SEG = SUBLANES
VMEM_LIMIT = 56 * 1024 * 1024

_OFF_Q = 0
_OFF_KV = D_Q
_OFF_CONV = _OFF_KV + 2 * D_KV
_OFF_ZX = _OFF_CONV + 2 * D_CONV
_OFF_ZR = _OFF_ZX + D_RNN
_OFF_GATE = _OFF_ZR + D_RNN

_Q_SCALE = HEAD_DIM ** -0.5 * math.log2(math.e)


def _const_spec(shape):
    zeros = (0,) * len(shape)
    return pl.BlockSpec(tuple(shape), lambda *_: zeros, pipeline_mode=pl.Buffered(1))


def _layer_spec(arr, l):
    tail = (0,) * (arr.ndim - 1)
    return pl.BlockSpec((None,) + tuple(arr.shape[1:]), lambda *_: (l,) + tail,
                        pipeline_mode=pl.Buffered(1))


def _mod_spec(l, k, row):
    def index(b, *_):
        return (l, b if row is None else row, k, 0, 0)
    return lambda d: pl.BlockSpec((None, None, None, 1, d), index)


def _params(*sem, flags=None):
    return pltpu.CompilerParams(dimension_semantics=sem, vmem_limit_bytes=VMEM_LIMIT, flags=flags)


def _rms(x, g):
    return x * lax.rsqrt(jnp.mean(x * x, axis=-1, keepdims=True) + EPS) * g


def _mm(a, b):
    return jnp.dot(a, b, preferred_element_type=F32)


def _modulated(x, g, shift, scale):
    return _rms(x, g) * (1.0 + scale) + shift


def _seam_prev(a):
    sub = lax.broadcasted_iota(jnp.int32, a.shape, 0) % SEG
    return jnp.where(sub == 0, 0.0, pltpu.roll(a, 1, 0))


def _seam_next(a):
    sub = lax.broadcasted_iota(jnp.int32, a.shape, 0) % SEG
    return jnp.where(sub == SEG - 1, 0.0, pltpu.roll(a, a.shape[0] - 1, 0))


def _time_order_spec(tm, width):
    return pl.BlockSpec((None, SEG, tm // SEG, width), lambda b, t: (b, 0, t, 0))


def _slab_scratch(tm, width):
    return pltpu.VMEM((width // LANES, tm, LANES), F32)


def _load_interleaved(x_ref, slab_ref):
    seg, steps, width = x_ref.shape
    for j in range(width // LANES):
        for s in range(seg):
            slab_ref[j, pl.ds(s, steps, stride=seg), :] = x_ref[s, :, j * LANES:(j + 1) * LANES]


def _read_slabs(slab_ref):
    return jnp.concatenate([slab_ref[j] for j in range(slab_ref.shape[0])], axis=1)


def _store_time_order(y, o_ref, slab_ref):
    seg, steps, width = o_ref.shape
    for j in range(width // LANES):
        slab_ref[j] = y[:, j * LANES:(j + 1) * LANES]
    for j in range(width // LANES):
        for s in range(seg):
            o_ref[s, :, j * LANES:(j + 1) * LANES] = slab_ref[j, pl.ds(s, steps, stride=seg), :]


def _halo_specs(tm, halo, width, n_rows):
    per = tm // halo
    n_blocks = n_rows // halo
    prev = pl.BlockSpec((None, halo, width), lambda b, t: (b, (t * per + n_blocks - 1) % n_blocks, 0))
    nxt = pl.BlockSpec((None, halo, width), lambda b, t: (b, ((t + 1) * per) % n_blocks, 0))
    return prev, nxt


def _mod_kernel(c_ref, w_ref, b_ref, o_ref):
    c = c_ref[...]
    a = (c * jax.nn.sigmoid(c)).astype(BF16)
    o_ref[...] = _mm(a, w_ref[...].astype(BF16)) + b_ref[...]


def _mod_call(cc, w_mod, b_mod):
    L, D, D6 = w_mod.shape
    R = cc.shape[0]
    bn = D6 // 4
    return pl.pallas_call(
        _mod_kernel,
        grid=(L, D6 // bn),
        in_specs=[
            pl.BlockSpec((R, D), lambda l, j: (0, 0)),
            pl.BlockSpec((None, D, bn), lambda l, j: (l, 0, j)),
            pl.BlockSpec((None, 1, bn), lambda l, j: (l, 0, j)),
        ],
        out_specs=pl.BlockSpec((None, R, bn), lambda l, j: (l, 0, j)),
        out_shape=jax.ShapeDtypeStruct((L, R, D6), F32),
        compiler_params=_params("arbitrary", "arbitrary"),
        name="adaln_mod",
    )(cc, w_mod, b_mod)


def _head_rms(z, gain, bd):
    sq = z * z
    hi = sq.astype(BF16)
    lo = (sq - hi.astype(F32)).astype(BF16)
    ms = (_mm(hi, bd) + _mm(lo, bd)) * (1.0 / HEAD_DIM)
    return z * lax.rsqrt(ms + EPS) * gain


def _rope(x, c, s):
    rows, width = x.shape
    lane = lax.broadcasted_iota(jnp.int32, (rows, LANES), 1)
    first_half = (lane & (HEAD_DIM // 2)) == 0
    cols = []
    for j in range(width // LANES):
        sl = slice(j * LANES, (j + 1) * LANES)
        xc = x[:, sl]
        partner = jnp.where(first_half,
                            pltpu.roll(xc, LANES - HEAD_DIM // 2, 1),
                            pltpu.roll(xc, HEAD_DIM // 2, 1))
        cols.append(xc * c[:, sl] + partner * s[:, sl])
    return jnp.concatenate(cols, axis=1) if len(cols) > 1 else cols[0]


def _inproj_kernel(*refs, rope, time_order):
    x_ref, sh_ref, sc_ref, g_ref, w_ref, qg_ref, kg_ref, bd_ref = refs[:8]
    refs = refs[8:]
    if rope:
        cos_ref, sin_ref = refs[:2]
        refs = refs[2:]
    h_ref, q_ref, k_ref, v_ref, u_ref, zx_ref, zr_ref = refs[:7]
    if time_order:
        _load_interleaved(x_ref, refs[7])
        x = _read_slabs(refs[7])
    else:
        x = x_ref[...]
    h = _modulated(x, g_ref[...], sh_ref[...], sc_ref[...]).astype(BF16)
    h_ref[...] = h

    zq = _mm(h, w_ref[:, _OFF_Q:_OFF_Q + D_Q])
    qn = _head_rms(zq, qg_ref[...], bd_ref[...])
    if rope:
        qn = _rope(qn, cos_ref[...], sin_ref[...])
    q_ref[...] = (qn * _Q_SCALE).astype(BF16)

    zkv = _mm(h, w_ref[:, _OFF_KV:_OFF_KV + 2 * D_KV])
    kn = _head_rms(zkv[:, :D_KV], kg_ref[...], bd_ref[:D_KV, :D_KV])
    if rope:
        kn = _rope(kn, cos_ref[:, :D_KV], sin_ref[:, :D_KV])
    k_ref[...] = kn
    v_ref[...] = zkv[:, D_KV:]

    zc = _mm(h, w_ref[:, _OFF_CONV:_OFF_CONV + 2 * D_CONV])
    u_ref[...] = zc[:, :D_CONV] * jax.nn.sigmoid(zc[:, D_CONV:])

    zx_ref[...] = _mm(h, w_ref[:, _OFF_ZX:_OFF_ZX + D_RNN])
    zr_ref[...] = _mm(h, w_ref[:, _OFF_ZR:_OFF_ZR + D_RNN])


def _inproj_call(x, mods, l, mod_row, gain, w, qg, kg, bd, cos, sin, *, tm):
    time_order = x.ndim == 4
    B, D = x.shape[0], x.shape[-1]
    T = x.shape[1] * x.shape[2] if time_order else x.shape[1]
    rope = cos is not None
    tok = lambda w_: pl.BlockSpec((None, tm, w_), lambda b, t: (b, t, 0))
    x_spec = _time_order_spec(tm, D) if time_order else tok(D)
    in_specs = [x_spec, _mod_spec(l, 0, mod_row)(D), _mod_spec(l, 1, mod_row)(D),
                _layer_spec(gain, l), _layer_spec(w, l), _layer_spec(qg, l), _layer_spec(kg, l),
                _const_spec(bd.shape)]
    args = [x, mods, mods, gain, w, qg, kg, bd]
    if rope:
        tab = pl.BlockSpec((tm, D_Q), lambda b, t: (t, 0))
        in_specs += [tab, tab]
        args += [cos, sin]
    out_shape = [jax.ShapeDtypeStruct((B, T, D), BF16),
                 jax.ShapeDtypeStruct((B, T, D_Q), BF16),
                 jax.ShapeDtypeStruct((B, T, D_KV), F32),
                 jax.ShapeDtypeStruct((B, T, D_KV), F32),
                 jax.ShapeDtypeStruct((B, T, D_CONV), F32),
                 jax.ShapeDtypeStruct((B, T, D_RNN), F32),
                 jax.ShapeDtypeStruct((B, T, D_RNN), F32)]
    out_specs = [tok(D), tok(D_Q), tok(D_KV), tok(D_KV), tok(D_CONV), tok(D_RNN), tok(D_RNN)]
    return pl.pallas_call(
        functools.partial(_inproj_kernel, rope=rope, time_order=time_order),
        grid=(B, T // tm),
        in_specs=in_specs, out_specs=out_specs, out_shape=out_shape,
        scratch_shapes=[_slab_scratch(tm, D)] if time_order else [],
        compiler_params=_params("arbitrary", "arbitrary"),
        name="inproj_rope" if rope else "inproj",
    )(*args)


_VT_ROWS = HEAD_DIM + 2 * SUBLANES
_ATTN_KEYS = 256


def _attn_kernel(*refs, seg_lens):
    nseg = len(seg_lens)
    q_ref = refs[0]
    kv_refs = refs[1:1 + 2 * nseg]
    o_ref = refs[1 + 2 * nseg]
    kq_ref, vt_ref = refs[2 + 2 * nseg:]
    m_tot = sum(seg_lens)
    g = pl.program_id(1)

    @pl.when(pl.program_id(2) == 0)
    def _():
        off = 0
        for i, ms in enumerate(seg_lens):
            lane_half = lax.broadcasted_iota(jnp.int32, (ms, LANES), 1) // HEAD_DIM
            k = kv_refs[2 * i][...]
            kd = jnp.where(lane_half == g, k, pltpu.roll(k, HEAD_DIM, 1)).astype(BF16)
            kq_ref[off:off + ms, :] = jnp.concatenate([kd, kd], axis=1)
            v = kv_refs[2 * i + 1][...]
            vd = jnp.where(lane_half == g, v, pltpu.roll(v, HEAD_DIM, 1))
            vt_ref[0:HEAD_DIM, off:off + ms] = vd.T[0:HEAD_DIM, :].astype(BF16)
            off += ms
        vt_ref[HEAD_DIM:, :] = jnp.ones((_VT_ROWS - HEAD_DIM, m_tot), BF16)

    q = q_ref[...]
    head_lane = lax.broadcasted_iota(jnp.int32, q.shape, 1) // HEAD_DIM
    q_heads = [jnp.where(head_lane == h, q, jnp.zeros_like(q)) for h in range(Q_GROUP)]
    n_chunks = m_tot // _ATTN_KEYS

    def scores(h):
        return lax.dot_general(kq_ref[...], q_heads[h], (((1,), (1,)), ((), ())),
                               preferred_element_type=F32)

    outs = []
    st_next = scores(0)
    for h in range(Q_GROUP):
        st_all = st_next
        if h + 1 < Q_GROUP:
            st_next = scores(h + 1)
        for c in range(n_chunks):
            rows = slice(c * _ATTN_KEYS, (c + 1) * _ATTN_KEYS)
            st = st_all[rows, :]
            vt = vt_ref[:, rows]
            m_chunk = jnp.max(st, axis=0, keepdims=True)
            if c == 0:
                m_run = m_chunk
                acc = _mm(vt, jnp.exp2(st - m_run).astype(BF16))
            else:
                m_new = jnp.maximum(m_run, m_chunk)
                acc = acc * jnp.exp2(m_run - m_new) + _mm(vt, jnp.exp2(st - m_new).astype(BF16))
                m_run = m_new
        outs.append(acc[0:HEAD_DIM, :] * (1.0 / acc[HEAD_DIM:HEAD_DIM + 1, :]))
    for c in range(Q_GROUP // 2):
        pair = jnp.concatenate([outs[2 * c], outs[2 * c + 1]], axis=0)
        o_ref[:, c * LANES:(c + 1) * LANES] = pair.T.astype(BF16)


def _attn_call(q, segs, *, tq):
    B, T, _ = q.shape
    seg_lens = tuple(k.shape[1] for k, _ in segs)
    m_tot = sum(seg_lens)
    gw = Q_GROUP * HEAD_DIM
    in_specs = [pl.BlockSpec((None, tq, gw), lambda b, g, t: (b, t, g))]
    args = [q]
    for k, v in segs:
        spec = pl.BlockSpec((None, k.shape[1], D_KV), lambda b, g, t: (b, 0, 0))
        in_specs += [spec, spec]
        args += [k, v]
    return pl.pallas_call(
        functools.partial(_attn_kernel, seg_lens=seg_lens),
        grid=(B, N_KV_HEADS, T // tq),
        in_specs=in_specs,
        out_specs=pl.BlockSpec((None, tq, gw), lambda b, g, t: (b, t, g)),
        out_shape=jax.ShapeDtypeStruct((B, T, D_Q), BF16),
        scratch_shapes=[pltpu.VMEM((m_tot, gw), BF16),
                        pltpu.VMEM((_VT_ROWS, m_tot), BF16)],
        compiler_params=_params("arbitrary", "arbitrary", "arbitrary"),
        name="gqa_attention",
    )(*args)


_CONV_HALO = 16 * SEG
_CONV_ROWS = 32


def _conv_fill(up_ref, u_ref, un_ref, w_ref, pad_ref, wb_ref):
    tm = u_ref.shape[0]
    t = pl.program_id(1)
    for j in range(CONV_K):
        wb_ref[j * SEG:(j + 1) * SEG, :] = jnp.broadcast_to(w_ref[j:j + 1, :], (SEG, D_CONV))
    pad_ref[0:_CONV_HALO, :] = up_ref[...]
    pad_ref[_CONV_HALO:_CONV_HALO + tm, :] = u_ref[...]
    pad_ref[_CONV_HALO + tm:, :] = un_ref[...]

    @pl.when(t == 0)
    def _():
        pad_ref[0:_CONV_HALO, :] = _seam_prev(up_ref[...])

    @pl.when(t == pl.num_programs(1) - 1)
    def _():
        pad_ref[_CONV_HALO + tm:, :] = _seam_next(un_ref[...])


def _conv_rows(pad_ref, wb_ref, b_ref, lg_ref, lb_ref, o_ref, row_lo, row_hi):
    first = _CONV_HALO // SEG - CONV_K // 2
    for r in range(row_lo, row_hi, _CONV_ROWS):
        acc = jnp.zeros((_CONV_ROWS, D_CONV), F32)
        for j in range(CONV_K):
            lo = r + (first + j) * SEG
            w_tile = wb_ref[j * SEG:(j + 1) * SEG, :]
            acc = acc + pad_ref[lo:lo + _CONV_ROWS, :] * jnp.concatenate([w_tile] * (_CONV_ROWS // SEG), axis=0)
        acc = acc + b_ref[...]
        mu = jnp.mean(acc, axis=-1, keepdims=True)
        cen = acc - mu
        var = jnp.mean(cen * cen, axis=-1, keepdims=True)
        y = cen * lax.rsqrt(var + EPS) * lg_ref[...] + lb_ref[...]
        o_ref[r:r + _CONV_ROWS, :] = (y * jax.nn.sigmoid(y)).astype(BF16)


_RNN_HALO = 4 * SEG


_RNN_OUT_ROWS = 2 * SUBLANES


def _rnn_kernel(zx_ref, zr_ref, h0_ref, cw_ref, cb_ref, w_ref, b_ref, lam_ref, o_ref, hl_ref,
                xp_ref, a_ref, bb_ref, ac_ref, h_ref, hf_ref, *, chunk):
    T = zx_ref.shape[0]
    n_chunks = T // chunk
    steps = chunk // SEG

    xp_ref[0:_RNN_HALO, :] = _seam_prev(zx_ref[T - _RNN_HALO:T, :])
    xp_ref[_RNN_HALO + T:, :] = _seam_next(zx_ref[0:_RNN_HALO, :])

    def copy(i, carry):
        r = pl.multiple_of(i * chunk, chunk)
        xp_ref[pl.ds(_RNN_HALO + r, chunk), :] = zx_ref[pl.ds(r, chunk), :]
        return carry

    lax.fori_loop(0, n_chunks, copy, 0)

    lam = lam_ref[...]
    log_sig = jnp.minimum(lam, 0.0) - jnp.log1p(jnp.exp(-jnp.abs(lam)))
    half_c = (0.5 * LRU_C * math.log2(math.e)) * log_sig
    row = lax.broadcasted_iota(jnp.int32, (SEG, D_RNN), 0)

    def run(reverse):
        def chunk_body(c, carry):
            cc = (n_chunks - 1 - c) if reverse else c
            r0 = pl.multiple_of(cc * chunk, chunk)
            xc = jnp.zeros((chunk, D_RNN), F32)
            for j in range(RNN_CONV_K):
                step = (RNN_CONV_K - 1 - j) if reverse else (j - RNN_CONV_K + 1)
                xc = xc + xp_ref[pl.ds(r0 + _RNN_HALO + step * SEG, chunk), :] * cw_ref[j:j + 1, :]
            xc = xc + cb_ref[...]
            gates = _mm(xc.astype(BF16), w_ref[...]) + b_ref[...]
            tanh_r = jnp.tanh(0.5 * gates[:, :D_RNN])
            i_gate = 0.5 + 0.5 * jnp.tanh(0.5 * gates[:, D_RNN:])
            a = jnp.exp2(half_c + half_c * tanh_r)
            a_ref[...] = a
            bb_ref[...] = jnp.sqrt((1.0 - a) * (1.0 + a)) * (i_gate * xc)

            def step_body(k, state):
                a_cum, h_zero = state
                kk = (steps - 1 - k) if reverse else k
                t0 = pl.multiple_of(kk * SEG, SEG)
                at = a_ref[pl.ds(t0, SEG), :]
                h_zero = at * h_zero + bb_ref[pl.ds(t0, SEG), :]
                a_cum = at * a_cum
                h_ref[pl.ds(r0 + t0, SEG), :] = h_zero
                ac_ref[pl.ds(r0 + t0, SEG), :] = a_cum
                return a_cum, h_zero

            return lax.fori_loop(0, steps, step_body, carry, unroll=8)

        a_end, h_end = lax.fori_loop(0, n_chunks, chunk_body,
                                     (jnp.ones((SEG, D_RNN), F32), jnp.zeros((SEG, D_RNN), F32)))

        start = SEG - 1 if reverse else 0
        state_in = jnp.where(row == start, h0_ref[...], 0.0)
        for i in range(1, SEG):
            s = start - i if reverse else start + i
            leaving = h_end + a_end * state_in
            state_in = jnp.where(row == s, pltpu.roll(leaving, SEG - 1 if reverse else 1, 0), state_in)

        state_rows = jnp.concatenate([state_in] * (_RNN_OUT_ROWS // SEG), axis=0)

        def fix(k, carry):
            t0 = pl.multiple_of(k * _RNN_OUT_ROWS, _RNN_OUT_ROWS)
            rows = pl.ds(t0, _RNN_OUT_ROWS)
            h = h_ref[rows, :] + ac_ref[rows, :] * state_rows
            if reverse:
                o_ref[rows, :] = ((hf_ref[rows, :] + h) * jax.nn.gelu(zr_ref[rows, :])).astype(BF16)
            else:
                hf_ref[rows, :] = h
            return carry

        lax.fori_loop(0, T // _RNN_OUT_ROWS, fix, 0, unroll=4)
        edge = slice(0, SEG) if reverse else slice(T - SEG, T)
        last_rows = h_ref[edge, :] + ac_ref[edge, :] * state_in
        final = last_rows[0:1, :] if reverse else last_rows[SEG - 1:SEG, :]
        hl_ref[...] = jnp.broadcast_to(final, (SEG, D_RNN))

    d = pl.program_id(1)
    pl.when(d == 0)(lambda: run(False))
    pl.when(d == 1)(lambda: run(True))


def _rnn_call(zx, zr, h0, l, cw, cb, w, b, lam, *, chunk):
    B, T, C = zx.shape
    per_dir = lambda a: pl.BlockSpec((None, None) + tuple(a.shape[2:]), lambda b_, d: (l, d, 0, 0))
    state = pl.BlockSpec((None, None, SEG, C), lambda b_, d: (b_, d, 0, 0))
    seq = pl.BlockSpec((None, T, C), lambda b_, d: (b_, 0, 0))
    return pl.pallas_call(
        functools.partial(_rnn_kernel, chunk=chunk),
        grid=(B, 2),
        in_specs=[seq, seq, state, per_dir(cw), per_dir(cb), per_dir(w), per_dir(b), per_dir(lam)],
        out_specs=[seq, state],
        out_shape=[jax.ShapeDtypeStruct((B, T, C), BF16),
                   jax.ShapeDtypeStruct((B, 2, SEG, C), F32)],
        scratch_shapes=[pltpu.VMEM((T + 2 * _RNN_HALO, C), F32),
                        pltpu.VMEM((chunk, C), F32), pltpu.VMEM((chunk, C), F32),
                        pltpu.VMEM((T, C), F32), pltpu.VMEM((T, C), F32), pltpu.VMEM((T, C), F32)],
        compiler_params=_params("arbitrary", "arbitrary"),
        name="rglru",
    )(zx, zr, h0, cw, cb, w, b, lam)


_MERGE_COLS = 256


def _merge_kernel(x_ref, h_ref, gt_ref, gpost_ref, wg_ref, at_ref,
                  up_ref, u_ref, un_ref, cw_ref, cb_ref, lg_ref, lb_ref,
                  rn_ref, wao_ref, wco_ref, wro_ref, wout_ref, o_ref, pad_ref, wb_ref, cv_ref, m_ref, cp_ref,
                  mb_ref, *slab):
    D = x_ref.shape[-1]
    _conv_fill(up_ref, u_ref, un_ref, cw_ref, pad_ref, wb_ref)
    tm = u_ref.shape[0]
    conv = functools.partial(_conv_rows, pad_ref, wb_ref, cb_ref, lg_ref, lb_ref, cv_ref)

    def gate(branch, cols):
        first = _OFF_GATE + branch * D
        return jax.nn.sigmoid(_mm(h_ref[...], wg_ref[:, first + cols.start:first + cols.stop]))

    quarter = tm // 4
    half = D // 2
    groups = ((at_ref, wao_ref, 0, 0), (at_ref, wao_ref, 0, half), (rn_ref, wro_ref, 2, 0),
              (rn_ref, wro_ref, 2, half))
    for i, (act_ref, w_ref, branch, col0) in enumerate(groups):
        for c in range(col0, col0 + half, _MERGE_COLS):
            cols = slice(c, c + _MERGE_COLS)
            term = gate(branch, cols) * _mm(act_ref[...], w_ref[:, cols])
            m_ref[:, cols] = term if branch == 0 else m_ref[:, cols] + term
        rows = slice(i * quarter, (i + 1) * quarter)
        conv(rows.start, rows.stop)
        cp_ref[rows, :] = _mm(cv_ref[rows, :], wco_ref[...])
    for c in range(0, D, _MERGE_COLS):
        cols = slice(c, c + _MERGE_COLS)
        mb_ref[:, cols] = (m_ref[:, cols] + gate(1, cols) * cp_ref[:, cols]).astype(BF16)
    o = _mm(mb_ref[...], wout_ref[...])
    if slab:
        _load_interleaved(x_ref, slab[0])
        x = _read_slabs(slab[0])
    else:
        x = x_ref[...]
    o_ref[...] = x + gt_ref[...] * _rms(o, gpost_ref[...])


def _merge_call(x, h, mods, l, mod_row, gpost, wg, attn, u, cw, cb, lg, lb, rnn, wao, wco, wro, wout, *, tm):
    time_order = x.ndim == 4
    B, D = x.shape[0], x.shape[-1]
    T = attn.shape[1]
    tok = lambda w_: pl.BlockSpec((None, tm, w_), lambda b, t: (b, t, 0))
    u_prev, u_next = _halo_specs(tm, _CONV_HALO, D_CONV, T)
    return pl.pallas_call(
        _merge_kernel,
        grid=(B, T // tm),
        in_specs=[_time_order_spec(tm, D) if time_order else tok(D), tok(D),
                  _mod_spec(l, 2, mod_row)(D), _layer_spec(gpost, l),
                  _layer_spec(wg, l), tok(D_Q),
                  u_prev, tok(D_CONV), u_next, _layer_spec(cw, l), _layer_spec(cb, l),
                  _layer_spec(lg, l), _layer_spec(lb, l), tok(D_RNN),
                  _layer_spec(wao, l), _layer_spec(wco, l), _layer_spec(wro, l), _layer_spec(wout, l)],
        out_specs=tok(D),
        out_shape=jax.ShapeDtypeStruct((B, T, D), F32),
        scratch_shapes=[pltpu.VMEM((tm + 2 * _CONV_HALO, D_CONV), F32),
                        pltpu.VMEM((CONV_K * SEG, D_CONV), F32), pltpu.VMEM((tm, D_CONV), BF16),
                        pltpu.VMEM((tm, D), F32), pltpu.VMEM((tm, D), F32), pltpu.VMEM((tm, D), BF16)]
        + ([_slab_scratch(tm, D)] if time_order else []),
        compiler_params=_params("arbitrary", "arbitrary"),
        name="merge",
    )(x, h, mods, gpost, wg, attn, u, u, u, cw, cb, lg, lb, rnn, wao, wco, wro, wout)


_FFN_CHUNK = 512


def _ffn_kernel(xp_ref, x_ref, xn_ref, sh_ref, sc_ref, gt_ref, gpre_ref, gpost_ref, up_ref,
                dw_ref, db_ref, down_ref, o_ref, hp_ref, hn_ref, *slab):
    tm = x_ref.shape[0]
    t = pl.program_id(1)
    mod = lambda v: _modulated(v, gpre_ref[...], sh_ref[...], sc_ref[...])
    hp_ref[...] = mod(xp_ref[...])
    hn_ref[...] = mod(xn_ref[...])

    @pl.when(t == 0)
    def _():
        hp_ref[...] = _seam_prev(hp_ref[...])

    @pl.when(t == pl.num_programs(1) - 1)
    def _():
        hn_ref[...] = _seam_next(hn_ref[...])

    h = jnp.concatenate([hp_ref[...], mod(x_ref[...]), hn_ref[...]], axis=0).astype(BF16)

    acc = jnp.zeros(x_ref.shape, F32)
    for c in range(D_FF // _FFN_CHUNK):
        cg = slice(c * _FFN_CHUNK, (c + 1) * _FFN_CHUNK)
        cv = slice(D_FF + c * _FFN_CHUNK, D_FF + (c + 1) * _FFN_CHUNK)
        ug = _mm(h, up_ref[:, cg])
        uv = _mm(h, up_ref[:, cv])
        yg = db_ref[:, cg]
        yv = db_ref[:, cv]
        for j in range(3):
            rows = slice(j * SEG, j * SEG + tm)
            yg = yg + ug[rows, :] * dw_ref[j:j + 1, cg]
            yv = yv + uv[rows, :] * dw_ref[j:j + 1, cv]
        act = (jax.nn.gelu(yg) * yv).astype(BF16)
        acc = acc + _mm(act, down_ref[cg, :])
    y = x_ref[...] + gt_ref[...] * _rms(acc, gpost_ref[...])
    if slab:
        _store_time_order(y, o_ref, slab[0])
    else:
        o_ref[...] = y


def _ffn_call(x, mods, l, mod_row, gpre, gpost, up, dw, db, down, *, tm, time_order_out=False):
    B, T, D = x.shape
    prev, nxt = _halo_specs(tm, SEG, D, T)
    if time_order_out:
        out_spec = _time_order_spec(tm, D)
        out_shape = jax.ShapeDtypeStruct((B, SEG, T // SEG, D), F32)
    else:
        out_spec = pl.BlockSpec((None, tm, D), lambda b, t: (b, t, 0))
        out_shape = jax.ShapeDtypeStruct((B, T, D), F32)
    return pl.pallas_call(
        _ffn_kernel,
        grid=(B, T // tm),
        in_specs=[prev, pl.BlockSpec((None, tm, D), lambda b, t: (b, t, 0)), nxt,
                  _mod_spec(l, 3, mod_row)(D), _mod_spec(l, 4, mod_row)(D), _mod_spec(l, 5, mod_row)(D),
                  _layer_spec(gpre, l), _layer_spec(gpost, l), _layer_spec(up, l),
                  _layer_spec(dw, l), _layer_spec(db, l), _layer_spec(down, l)],
        out_specs=out_spec,
        out_shape=out_shape,
        scratch_shapes=[pltpu.VMEM((SEG, D), F32), pltpu.VMEM((SEG, D), F32)]
        + ([_slab_scratch(tm, D)] if time_order_out else []),
        compiler_params=_params("arbitrary", "arbitrary"),
        name="conv_ffn",
    )(x, x, x, mods, mods, mods, gpre, gpost, up, dw, db, down)


def _seg_interleave(a):
    *lead, T, W = a.shape
    return a.reshape(*lead, SEG, T // SEG, W).swapaxes(-3, -2).reshape(*lead, T, W)


def _rope_tables(n):
    rows = n // GRID_W
    row = jnp.repeat(jnp.arange(rows), GRID_W).astype(F32)
    col = jnp.tile(jnp.arange(GRID_W), rows).astype(F32)
    n_freq = HEAD_DIM // 4
    freq = ROPE_THETA ** (-jnp.arange(n_freq, dtype=F32) / n_freq)
    ang = jnp.concatenate([row[:, None] * freq, col[:, None] * freq], axis=-1)
    cos, sin = jnp.cos(ang), jnp.sin(ang)
    cos_t = jnp.tile(jnp.concatenate([cos, cos], axis=-1), (1, N_Q_HEADS))
    sin_t = jnp.tile(jnp.concatenate([-sin, sin], axis=-1), (1, N_Q_HEADS))
    return cos_t, sin_t


def _block_diag(w):
    *lead, nb, bi, bj = w.shape
    eye = jnp.eye(nb, dtype=w.dtype)
    return jnp.einsum('...hij,hg->...higj', w, eye).reshape(*lead, nb * bi, nb * bj)


def kernel(x, c, ctx, c_ctx, w_mod, b_mod, norm_pre_mix, norm_post_mix, norm_pre_ffn, norm_post_ffn,
           w_in, q_norm, k_norm, w_attn_out, conv_dw, conv_dw_b, conv_ln_g, conv_ln_b, w_conv_out,
           rnn_conv_w, rnn_conv_b, rnn_wa, rnn_ba, rnn_wx, rnn_bx, rnn_lambda, w_rnn_out, w_out,
           ffn_up, ffn_dw, ffn_dw_b, ffn_down):
    B, N, D = x.shape
    M = ctx.shape[1]
    L = w_mod.shape[0]
    tm_lat = min(N, 512)
    tm_ctx = min(M, 256)
    chunk_lat = min(N, 256)
    chunk_ctx = min(M, 256)

    mod_rows = 2 * SUBLANES
    cc = jnp.concatenate([c, c_ctx[None, :], jnp.zeros((mod_rows - B - 1, D), F32)], axis=0)
    mods = _mod_call(cc, w_mod, b_mod[:, None, :]).reshape(L, mod_rows, N_MOD, 1, D)

    cos_t, sin_t = (_seg_interleave(t) for t in _rope_tables(N))
    head_id = jnp.arange(D_Q) // HEAD_DIM
    bd = (head_id[:, None] == head_id[None, :]).astype(BF16)
    h0 = jnp.zeros((B, 2, SEG, D_RNN), F32)

    vec = lambda a: a[:, None, :]
    w_proj = w_gate = w_in.astype(BF16)
    qg = vec(jnp.tile(q_norm, (1, N_Q_HEADS)))
    kg = vec(jnp.tile(k_norm, (1, N_KV_HEADS)))
    rnn_w = jnp.concatenate([_block_diag(rnn_wa), _block_diag(rnn_wx)], axis=-1).astype(BF16)
    rnn_b = jnp.concatenate([rnn_ba, rnn_bx], axis=-1)[:, :, None, :]
    rnn_cb = rnn_conv_b[:, :, None, :]
    lam = rnn_lambda[:, :, None, :]
    wao, wco, wro, wout = (w_attn_out.astype(BF16), w_conv_out.astype(BF16), w_rnn_out.astype(BF16),
                           w_out.astype(BF16))
    up, down = ffn_up.astype(BF16), ffn_down.astype(BF16)
    gpre_mix, gpost_mix, gpre_ffn, gpost_ffn = (vec(norm_pre_mix), vec(norm_post_mix), vec(norm_pre_ffn),
                                                vec(norm_post_ffn))
    cdb, clg, clb, fdb = vec(conv_dw_b), vec(conv_ln_g), vec(conv_ln_b), vec(ffn_dw_b)

    x = x.reshape(B, SEG, N // SEG, D)
    ctx = ctx.reshape(B, SEG, M // SEG, D)
    for l in range(L):
        need_ctx = l < L - 1
        hcx, qc, kc, vc, uc, zxc, zrc = _inproj_call(ctx, mods, l, B, gpre_mix, w_proj, qg, kg, bd, None, None,
                                                     tm=tm_ctx)
        rnn_c, hc_last = _rnn_call(zxc, zrc, h0, l, rnn_conv_w, rnn_cb, rnn_w, rnn_b, lam, chunk=chunk_ctx)

        hx, q, k, v, u, zx, zr = _inproj_call(x, mods, l, None, gpre_mix, w_proj, qg, kg, bd, cos_t, sin_t,
                                              tm=tm_lat)
        attn = _attn_call(q, [(kc, vc), (k, v)], tq=tm_lat)
        rnn, _ = _rnn_call(zx, zr, hc_last, l, rnn_conv_w, rnn_cb, rnn_w, rnn_b, lam, chunk=chunk_lat)
        x = _merge_call(x, hx, mods, l, None, gpost_mix, w_gate, attn, u, conv_dw, cdb, clg, clb,
                        rnn, wao, wco, wro, wout, tm=tm_lat)
        x = _ffn_call(x, mods, l, None, gpre_ffn, gpost_ffn, up, ffn_dw, fdb, down, tm=tm_lat,
                      time_order_out=not need_ctx)

        if need_ctx:
            attn_c = _attn_call(qc, [(kc, vc)], tq=tm_ctx)
            ctx = _merge_call(ctx, hcx, mods, l, B, gpost_mix, w_gate, attn_c, uc, conv_dw, cdb, clg, clb,
                              rnn_c, wao, wco, wro, wout, tm=tm_ctx)
            ctx = _ffn_call(ctx, mods, l, B, gpre_ffn, gpost_ffn, up, ffn_dw, fdb, down, tm=tm_ctx)
    return x.reshape(B, N, D)
```

```python
import functools
import math

import jax
import jax.numpy as jnp
from jax import lax
from jax.experimental import pallas as pl
from jax.experimental.pallas import tpu as pltpu

F32 = jnp.float32
BF16 = jnp.bfloat16

EPS = 1e-6
HEAD_DIM = 64
N_Q_HEADS = 8
N_KV_HEADS = 2
Q_GROUP = N_Q_HEADS // N_KV_HEADS
D_Q = N_Q_HEADS * HEAD_DIM
D_KV = N_KV_HEADS * HEAD_DIM
D_CONV = 512
CONV_K = 31
D_RNN = 512
RNN_CONV_K = 4
LRU_C = 8.0
D_FF = 3072
GRID_W = 64
ROPE_THETA = 10000.0
N_MOD = 6

LANES = 128
SUBLANES = 8
SEG = SUBLANES
VMEM_LIMIT = 56 * 1024 * 1024

_OFF_Q = 0
_OFF_KV = D_Q
_OFF_CONV = _OFF_KV + 2 * D_KV
_OFF_ZX = _OFF_CONV + 2 * D_CONV
_OFF_ZR = _OFF_ZX + D_RNN
_OFF_GATE = _OFF_ZR + D_RNN

_Q_SCALE = HEAD_DIM ** -0.5 * math.log2(math.e)


def _const_spec(shape):
    zeros = (0,) * len(shape)
    return pl.BlockSpec(tuple(shape), lambda *_: zeros, pipeline_mode=pl.Buffered(1))


def _layer_spec(arr, l):
    tail = (0,) * (arr.ndim - 1)
    return pl.BlockSpec((None,) + tuple(arr.shape[1:]), lambda *_: (l,) + tail,
                        pipeline_mode=pl.Buffered(1))


def _mod_spec(l, k, row):
    def index(b, *_):
        return (l, b if row is None else row, k, 0, 0)
    return lambda d: pl.BlockSpec((None, None, None, 1, d), index)


def _params(*sem, flags=None):
    return pltpu.CompilerParams(dimension_semantics=sem, vmem_limit_bytes=VMEM_LIMIT, flags=flags)


def _rms(x, g):
    return x * lax.rsqrt(jnp.mean(x * x, axis=-1, keepdims=True) + EPS) * g


def _mm(a, b):
    return jnp.dot(a, b, preferred_element_type=F32)


def _modulated(x, g, shift, scale):
    return _rms(x, g) * (1.0 + scale) + shift


def _seam_prev(a):
    sub = lax.broadcasted_iota(jnp.int32, a.shape, 0) % SEG
    return jnp.where(sub == 0, 0.0, pltpu.roll(a, 1, 0))


def _seam_next(a):
    sub = lax.broadcasted_iota(jnp.int32, a.shape, 0) % SEG
    return jnp.where(sub == SEG - 1, 0.0, pltpu.roll(a, a.shape[0] - 1, 0))


def _time_order_spec(tm, width):
    return pl.BlockSpec((None, SEG, tm // SEG, width), lambda b, t: (b, 0, t, 0))


def _slab_scratch(tm, width):
    return pltpu.VMEM((width // LANES, tm, LANES), F32)


def _load_interleaved(x_ref, slab_ref):
    seg, steps, width = x_ref.shape
    for j in range(width // LANES):
        for s in range(seg):
            slab_ref[j, pl.ds(s, steps, stride=seg), :] = x_ref[s, :, j * LANES:(j + 1) * LANES]


def _read_slabs(slab_ref):
    return jnp.concatenate([slab_ref[j] for j in range(slab_ref.shape[0])], axis=1)


def _store_time_order(y, o_ref, slab_ref):
    seg, steps, width = o_ref.shape
    for j in range(width // LANES):
        slab_ref[j] = y[:, j * LANES:(j + 1) * LANES]
    for j in range(width // LANES):
        for s in range(seg):
            o_ref[s, :, j * LANES:(j + 1) * LANES] = slab_ref[j, pl.ds(s, steps, stride=seg), :]


def _halo_specs(tm, halo, width, n_rows):
    per = tm // halo
    n_blocks = n_rows // halo
    prev = pl.BlockSpec((None, halo, width), lambda b, t: (b, (t * per + n_blocks - 1) % n_blocks, 0))
    nxt = pl.BlockSpec((None, halo, width), lambda b, t: (b, ((t + 1) * per) % n_blocks, 0))
    return prev, nxt


def _mod_kernel(c_ref, w_ref, b_ref, o_ref):
    c = c_ref[...]
    a = (c * jax.nn.sigmoid(c)).astype(BF16)
    o_ref[...] = _mm(a, w_ref[...].astype(BF16)) + b_ref[...]


def _mod_call(cc, w_mod, b_mod):
    L, D, D6 = w_mod.shape
    R = cc.shape[0]
    bn = D6 // 4
    return pl.pallas_call(
        _mod_kernel,
        grid=(L, D6 // bn),
        in_specs=[
            pl.BlockSpec((R, D), lambda l, j: (0, 0)),
            pl.BlockSpec((None, D, bn), lambda l, j: (l, 0, j)),
            pl.BlockSpec((None, 1, bn), lambda l, j: (l, 0, j)),
        ],
        out_specs=pl.BlockSpec((None, R, bn), lambda l, j: (l, 0, j)),
        out_shape=jax.ShapeDtypeStruct((L, R, D6), F32),
        compiler_params=_params("arbitrary", "arbitrary"),
        name="adaln_mod",
    )(cc, w_mod, b_mod)


def _head_rms(z, gain, bd):
    sq = z * z
    hi = sq.astype(BF16)
    lo = (sq - hi.astype(F32)).astype(BF16)
    ms = (_mm(hi, bd) + _mm(lo, bd)) * (1.0 / HEAD_DIM)
    return z * lax.rsqrt(ms + EPS) * gain


def _rope(x, c, s):
    rows, width = x.shape
    lane = lax.broadcasted_iota(jnp.int32, (rows, LANES), 1)
    first_half = (lane & (HEAD_DIM // 2)) == 0
    cols = []
    for j in range(width // LANES):
        sl = slice(j * LANES, (j + 1) * LANES)
        xc = x[:, sl]
        partner = jnp.where(first_half,
                            pltpu.roll(xc, LANES - HEAD_DIM // 2, 1),
                            pltpu.roll(xc, HEAD_DIM // 2, 1))
        cols.append(xc * c[:, sl] + partner * s[:, sl])
    return jnp.concatenate(cols, axis=1) if len(cols) > 1 else cols[0]


def _inproj_kernel(*refs, rope, time_order):
    x_ref, sh_ref, sc_ref, g_ref, w_ref, qg_ref, kg_ref, bd_ref = refs[:8]
    refs = refs[8:]
    if rope:
        cos_ref, sin_ref = refs[:2]
        refs = refs[2:]
    h_ref, q_ref, k_ref, v_ref, u_ref, zx_ref, zr_ref = refs[:7]
    if time_order:
        _load_interleaved(x_ref, refs[7])
        x = _read_slabs(refs[7])
    else:
        x = x_ref[...]
    h = _modulated(x, g_ref[...], sh_ref[...], sc_ref[...]).astype(BF16)
    h_ref[...] = h

    zq = _mm(h, w_ref[:, _OFF_Q:_OFF_Q + D_Q])
    qn = _head_rms(zq, qg_ref[...], bd_ref[...])
    if rope:
        qn = _rope(qn, cos_ref[...], sin_ref[...])
    q_ref[...] = (qn * _Q_SCALE).astype(BF16)

    zkv = _mm(h, w_ref[:, _OFF_KV:_OFF_KV + 2 * D_KV])
    kn = _head_rms(zkv[:, :D_KV], kg_ref[...], bd_ref[:D_KV, :D_KV])
    if rope:
        kn = _rope(kn, cos_ref[:, :D_KV], sin_ref[:, :D_KV])
    k_ref[...] = kn
    v_ref[...] = zkv[:, D_KV:]

    zc = _mm(h, w_ref[:, _OFF_CONV:_OFF_CONV + 2 * D_CONV])
    u_ref[...] = zc[:, :D_CONV] * jax.nn.sigmoid(zc[:, D_CONV:])

    zx_ref[...] = _mm(h, w_ref[:, _OFF_ZX:_OFF_ZX + D_RNN])
    zr_ref[...] = _mm(h, w_ref[:, _OFF_ZR:_OFF_ZR + D_RNN])


def _inproj_call(x, mods, l, mod_row, gain, w, qg, kg, bd, cos, sin, *, tm):
    time_order = x.ndim == 4
    B, D = x.shape[0], x.shape[-1]
    T = x.shape[1] * x.shape[2] if time_order else x.shape[1]
    rope = cos is not None
    tok = lambda w_: pl.BlockSpec((None, tm, w_), lambda b, t: (b, t, 0))
    x_spec = _time_order_spec(tm, D) if time_order else tok(D)
    in_specs = [x_spec, _mod_spec(l, 0, mod_row)(D), _mod_spec(l, 1, mod_row)(D),
                _layer_spec(gain, l), _layer_spec(w, l), _layer_spec(qg, l), _layer_spec(kg, l),
                _const_spec(bd.shape)]
    args = [x, mods, mods, gain, w, qg, kg, bd]
    if rope:
        tab = pl.BlockSpec((tm, D_Q), lambda b, t: (t, 0))
        in_specs += [tab, tab]
        args += [cos, sin]
    out_shape = [jax.ShapeDtypeStruct((B, T, D), BF16),
                 jax.ShapeDtypeStruct((B, T, D_Q), BF16),
                 jax.ShapeDtypeStruct((B, T, D_KV), F32),
                 jax.ShapeDtypeStruct((B, T, D_KV), F32),
                 jax.ShapeDtypeStruct((B, T, D_CONV), F32),
                 jax.ShapeDtypeStruct((B, T, D_RNN), F32),
                 jax.ShapeDtypeStruct((B, T, D_RNN), F32)]
    out_specs = [tok(D), tok(D_Q), tok(D_KV), tok(D_KV), tok(D_CONV), tok(D_RNN), tok(D_RNN)]
    return pl.pallas_call(
        functools.partial(_inproj_kernel, rope=rope, time_order=time_order),
        grid=(B, T // tm),
        in_specs=in_specs, out_specs=out_specs, out_shape=out_shape,
        scratch_shapes=[_slab_scratch(tm, D)] if time_order else [],
        compiler_params=_params("arbitrary", "arbitrary"),
        name="inproj_rope" if rope else "inproj",
    )(*args)


_VT_ROWS = HEAD_DIM + 2 * SUBLANES
_ATTN_KEYS = 256


def _attn_kernel(*refs, seg_lens):
    nseg = len(seg_lens)
    q_ref = refs[0]
    kv_refs = refs[1:1 + 2 * nseg]
    o_ref = refs[1 + 2 * nseg]
    kq_ref, vt_ref = refs[2 + 2 * nseg:]
    m_tot = sum(seg_lens)
    g = pl.program_id(1)

    @pl.when(pl.program_id(2) == 0)
    def _():
        off = 0
        for i, ms in enumerate(seg_lens):
            lane_half = lax.broadcasted_iota(jnp.int32, (ms, LANES), 1) // HEAD_DIM
            k = kv_refs[2 * i][...]
            kd = jnp.where(lane_half == g, k, pltpu.roll(k, HEAD_DIM, 1)).astype(BF16)
            kq_ref[off:off + ms, :] = jnp.concatenate([kd, kd], axis=1)
            v = kv_refs[2 * i + 1][...]
            vd = jnp.where(lane_half == g, v, pltpu.roll(v, HEAD_DIM, 1))
            vt_ref[0:HEAD_DIM, off:off + ms] = vd.T[0:HEAD_DIM, :].astype(BF16)
            off += ms
        vt_ref[HEAD_DIM:, :] = jnp.ones((_VT_ROWS - HEAD_DIM, m_tot), BF16)

    q = q_ref[...]
    head_lane = lax.broadcasted_iota(jnp.int32, q.shape, 1) // HEAD_DIM
    q_heads = [jnp.where(head_lane == h, q, jnp.zeros_like(q)) for h in range(Q_GROUP)]
    keys = min(_ATTN_KEYS, m_tot)
    n_chunks = m_tot // keys

    def scores(h):
        return lax.dot_general(kq_ref[...], q_heads[h], (((1,), (1,)), ((), ())),
                               preferred_element_type=F32)

    outs = []
    st_next = scores(0)
    for h in range(Q_GROUP):
        st_all = st_next
        if h + 1 < Q_GROUP:
            st_next = scores(h + 1)
        for c in range(n_chunks):
            rows = slice(c * keys, (c + 1) * keys)
            st = st_all[rows, :]
            vt = vt_ref[:, rows]
            m_chunk = jnp.max(st, axis=0, keepdims=True)
            if c == 0:
                m_run = m_chunk
                acc = _mm(vt, jnp.exp2(st - m_run).astype(BF16))
            else:
                m_new = jnp.maximum(m_run, m_chunk)
                acc = acc * jnp.exp2(m_run - m_new) + _mm(vt, jnp.exp2(st - m_new).astype(BF16))
                m_run = m_new
        outs.append(acc[0:HEAD_DIM, :] * (1.0 / acc[HEAD_DIM:HEAD_DIM + 1, :]))
    for c in range(Q_GROUP // 2):
        pair = jnp.concatenate([outs[2 * c], outs[2 * c + 1]], axis=0)
        o_ref[:, c * LANES:(c + 1) * LANES] = pair.T.astype(BF16)


def _attn_call(q, segs, *, tq):
    B, T, _ = q.shape
    seg_lens = tuple(k.shape[1] for k, _ in segs)
    m_tot = sum(seg_lens)
    gw = Q_GROUP * HEAD_DIM
    in_specs = [pl.BlockSpec((None, tq, gw), lambda b, g, t: (b, t, g))]
    args = [q]
    for k, v in segs:
        spec = pl.BlockSpec((None, k.shape[1], D_KV), lambda b, g, t: (b, 0, 0))
        in_specs += [spec, spec]
        args += [k, v]
    return pl.pallas_call(
        functools.partial(_attn_kernel, seg_lens=seg_lens),
        grid=(B, N_KV_HEADS, T // tq),
        in_specs=in_specs,
        out_specs=pl.BlockSpec((None, tq, gw), lambda b, g, t: (b, t, g)),
        out_shape=jax.ShapeDtypeStruct((B, T, D_Q), BF16),
        scratch_shapes=[pltpu.VMEM((m_tot, gw), BF16),
                        pltpu.VMEM((_VT_ROWS, m_tot), BF16)],
        compiler_params=_params("arbitrary", "arbitrary", "arbitrary"),
        name="gqa_attention",
    )(*args)


_CONV_HALO = 16 * SEG
_CONV_ROWS = 32


def _conv_fill(up_ref, u_ref, un_ref, w_ref, pad_ref, wb_ref):
    tm = u_ref.shape[0]
    t = pl.program_id(1)
    for j in range(CONV_K):
        wb_ref[j * SEG:(j + 1) * SEG, :] = jnp.broadcast_to(w_ref[j:j + 1, :], (SEG, D_CONV))
    pad_ref[0:_CONV_HALO, :] = up_ref[...]
    pad_ref[_CONV_HALO:_CONV_HALO + tm, :] = u_ref[...]
    pad_ref[_CONV_HALO + tm:, :] = un_ref[...]

    @pl.when(t == 0)
    def _():
        pad_ref[0:_CONV_HALO, :] = _seam_prev(up_ref[...])

    @pl.when(t == pl.num_programs(1) - 1)
    def _():
        pad_ref[_CONV_HALO + tm:, :] = _seam_next(un_ref[...])


def _conv_rows(pad_ref, wb_ref, b_ref, lg_ref, lb_ref, o_ref, row_lo, row_hi):
    first = _CONV_HALO // SEG - CONV_K // 2
    for r in range(row_lo, row_hi, _CONV_ROWS):
        acc = jnp.zeros((_CONV_ROWS, D_CONV), F32)
        for j in range(CONV_K):
            lo = r + (first + j) * SEG
            w_tile = wb_ref[j * SEG:(j + 1) * SEG, :]
            acc = acc + pad_ref[lo:lo + _CONV_ROWS, :] * jnp.concatenate([w_tile] * (_CONV_ROWS // SEG), axis=0)
        acc = acc + b_ref[...]
        mu = jnp.mean(acc, axis=-1, keepdims=True)
        cen = acc - mu
        var = jnp.mean(cen * cen, axis=-1, keepdims=True)
        y = cen * lax.rsqrt(var + EPS) * lg_ref[...] + lb_ref[...]
        o_ref[r:r + _CONV_ROWS, :] = (y * jax.nn.sigmoid(y)).astype(BF16)


_RNN_HALO = 4 * SEG


_RNN_OUT_ROWS = 2 * SUBLANES


def _rnn_kernel(zx_ref, zr_ref, h0_ref, cw_ref, cb_ref, w_ref, b_ref, lam_ref, o_ref, hl_ref,
                xp_ref, a_ref, bb_ref, ac_ref, h_ref, hf_ref, *, chunk):
    T = zx_ref.shape[0]
    n_chunks = T // chunk
    steps = chunk // SEG

    xp_ref[0:_RNN_HALO, :] = _seam_prev(zx_ref[T - _RNN_HALO:T, :])
    xp_ref[_RNN_HALO + T:, :] = _seam_next(zx_ref[0:_RNN_HALO, :])

    def copy(i, carry):
        r = pl.multiple_of(i * chunk, chunk)
        xp_ref[pl.ds(_RNN_HALO + r, chunk), :] = zx_ref[pl.ds(r, chunk), :]
        return carry

    lax.fori_loop(0, n_chunks, copy, 0)

    lam = lam_ref[...]
    log_sig = jnp.minimum(lam, 0.0) - jnp.log1p(jnp.exp(-jnp.abs(lam)))
    half_c = (0.5 * LRU_C * math.log2(math.e)) * log_sig
    row = lax.broadcasted_iota(jnp.int32, (SEG, D_RNN), 0)

    def run(reverse):
        def chunk_body(c, carry):
            cc = (n_chunks - 1 - c) if reverse else c
            r0 = pl.multiple_of(cc * chunk, chunk)
            xc = jnp.zeros((chunk, D_RNN), F32)
            for j in range(RNN_CONV_K):
                step = (RNN_CONV_K - 1 - j) if reverse else (j - RNN_CONV_K + 1)
                xc = xc + xp_ref[pl.ds(r0 + _RNN_HALO + step * SEG, chunk), :] * cw_ref[j:j + 1, :]
            xc = xc + cb_ref[...]
            gates = _mm(xc.astype(BF16), w_ref[...]) + b_ref[...]
            tanh_r = jnp.tanh(0.5 * gates[:, :D_RNN])
            i_gate = 0.5 + 0.5 * jnp.tanh(0.5 * gates[:, D_RNN:])
            a = jnp.exp2(half_c + half_c * tanh_r)
            a_ref[...] = a
            bb_ref[...] = jnp.sqrt((1.0 - a) * (1.0 + a)) * (i_gate * xc)

            def step_body(k, state):
                a_cum, h_zero = state
                kk = (steps - 1 - k) if reverse else k
                t0 = pl.multiple_of(kk * SEG, SEG)
                at = a_ref[pl.ds(t0, SEG), :]
                h_zero = at * h_zero + bb_ref[pl.ds(t0, SEG), :]
                a_cum = at * a_cum
                h_ref[pl.ds(r0 + t0, SEG), :] = h_zero
                ac_ref[pl.ds(r0 + t0, SEG), :] = a_cum
                return a_cum, h_zero

            return lax.fori_loop(0, steps, step_body, carry, unroll=8)

        a_end, h_end = lax.fori_loop(0, n_chunks, chunk_body,
                                     (jnp.ones((SEG, D_RNN), F32), jnp.zeros((SEG, D_RNN), F32)))

        start = SEG - 1 if reverse else 0
        state_in = jnp.where(row == start, h0_ref[...], 0.0)
        for i in range(1, SEG):
            s = start - i if reverse else start + i
            leaving = h_end + a_end * state_in
            state_in = jnp.where(row == s, pltpu.roll(leaving, SEG - 1 if reverse else 1, 0), state_in)

        state_rows = jnp.concatenate([state_in] * (_RNN_OUT_ROWS // SEG), axis=0)

        def fix(k, carry):
            t0 = pl.multiple_of(k * _RNN_OUT_ROWS, _RNN_OUT_ROWS)
            rows = pl.ds(t0, _RNN_OUT_ROWS)
            h = h_ref[rows, :] + ac_ref[rows, :] * state_rows
            if reverse:
                o_ref[rows, :] = ((hf_ref[rows, :] + h) * jax.nn.gelu(zr_ref[rows, :])).astype(BF16)
            else:
                hf_ref[rows, :] = h
            return carry

        lax.fori_loop(0, T // _RNN_OUT_ROWS, fix, 0, unroll=4)
        edge = slice(0, SEG) if reverse else slice(T - SEG, T)
        last_rows = h_ref[edge, :] + ac_ref[edge, :] * state_in
        final = last_rows[0:1, :] if reverse else last_rows[SEG - 1:SEG, :]
        hl_ref[...] = jnp.broadcast_to(final, (SEG, D_RNN))

    d = pl.program_id(1)
    pl.when(d == 0)(lambda: run(False))
    pl.when(d == 1)(lambda: run(True))


def _rnn_call(zx, zr, h0, l, cw, cb, w, b, lam, *, chunk):
    B, T, C = zx.shape
    per_dir = lambda a: pl.BlockSpec((None, None) + tuple(a.shape[2:]), lambda b_, d: (l, d, 0, 0))
    state = pl.BlockSpec((None, None, SEG, C), lambda b_, d: (b_, d, 0, 0))
    seq = pl.BlockSpec((None, T, C), lambda b_, d: (b_, 0, 0))
    return pl.pallas_call(
        functools.partial(_rnn_kernel, chunk=chunk),
        grid=(B, 2),
        in_specs=[seq, seq, state, per_dir(cw), per_dir(cb), per_dir(w), per_dir(b), per_dir(lam)],
        out_specs=[seq, state],
        out_shape=[jax.ShapeDtypeStruct((B, T, C), BF16),
                   jax.ShapeDtypeStruct((B, 2, SEG, C), F32)],
        scratch_shapes=[pltpu.VMEM((T + 2 * _RNN_HALO, C), F32),
                        pltpu.VMEM((chunk, C), F32), pltpu.VMEM((chunk, C), F32),
                        pltpu.VMEM((T, C), F32), pltpu.VMEM((T, C), F32), pltpu.VMEM((T, C), F32)],
        compiler_params=_params("arbitrary", "arbitrary"),
        name="rglru",
    )(zx, zr, h0, cw, cb, w, b, lam)


_MERGE_COLS = 256


def _merge_kernel(x_ref, h_ref, gt_ref, gpost_ref, wg_ref, at_ref,
                  up_ref, u_ref, un_ref, cw_ref, cb_ref, lg_ref, lb_ref,
                  rn_ref, wao_ref, wco_ref, wro_ref, wout_ref, o_ref, pad_ref, wb_ref, cv_ref, m_ref, cp_ref,
                  mb_ref, *slab):
    D = x_ref.shape[-1]
    _conv_fill(up_ref, u_ref, un_ref, cw_ref, pad_ref, wb_ref)
    tm = u_ref.shape[0]
    conv = functools.partial(_conv_rows, pad_ref, wb_ref, cb_ref, lg_ref, lb_ref, cv_ref)

    def gate(branch, cols):
        first = _OFF_GATE + branch * D
        return jax.nn.sigmoid(_mm(h_ref[...], wg_ref[:, first + cols.start:first + cols.stop]))

    quarter = tm // 4
    half = D // 2
    groups = ((at_ref, wao_ref, 0, 0), (at_ref, wao_ref, 0, half), (rn_ref, wro_ref, 2, 0),
              (rn_ref, wro_ref, 2, half))
    for i, (act_ref, w_ref, branch, col0) in enumerate(groups):
        for c in range(col0, col0 + half, _MERGE_COLS):
            cols = slice(c, c + _MERGE_COLS)
            term = gate(branch, cols) * _mm(act_ref[...], w_ref[:, cols])
            m_ref[:, cols] = term if branch == 0 else m_ref[:, cols] + term
        rows = slice(i * quarter, (i + 1) * quarter)
        conv(rows.start, rows.stop)
        cp_ref[rows, :] = _mm(cv_ref[rows, :], wco_ref[...])
    for c in range(0, D, _MERGE_COLS):
        cols = slice(c, c + _MERGE_COLS)
        mb_ref[:, cols] = (m_ref[:, cols] + gate(1, cols) * cp_ref[:, cols]).astype(BF16)
    o = _mm(mb_ref[...], wout_ref[...])
    if slab:
        _load_interleaved(x_ref, slab[0])
        x = _read_slabs(slab[0])
    else:
        x = x_ref[...]
    o_ref[...] = x + gt_ref[...] * _rms(o, gpost_ref[...])


def _merge_call(x, h, mods, l, mod_row, gpost, wg, attn, u, cw, cb, lg, lb, rnn, wao, wco, wro, wout, *, tm):
    time_order = x.ndim == 4
    B, D = x.shape[0], x.shape[-1]
    T = attn.shape[1]
    tok = lambda w_: pl.BlockSpec((None, tm, w_), lambda b, t: (b, t, 0))
    u_prev, u_next = _halo_specs(tm, _CONV_HALO, D_CONV, T)
    return pl.pallas_call(
        _merge_kernel,
        grid=(B, T // tm),
        in_specs=[_time_order_spec(tm, D) if time_order else tok(D), tok(D),
                  _mod_spec(l, 2, mod_row)(D), _layer_spec(gpost, l),
                  _layer_spec(wg, l), tok(D_Q),
                  u_prev, tok(D_CONV), u_next, _layer_spec(cw, l), _layer_spec(cb, l),
                  _layer_spec(lg, l), _layer_spec(lb, l), tok(D_RNN),
                  _layer_spec(wao, l), _layer_spec(wco, l), _layer_spec(wro, l), _layer_spec(wout, l)],
        out_specs=tok(D),
        out_shape=jax.ShapeDtypeStruct((B, T, D), F32),
        scratch_shapes=[pltpu.VMEM((tm + 2 * _CONV_HALO, D_CONV), F32),
                        pltpu.VMEM((CONV_K * SEG, D_CONV), F32), pltpu.VMEM((tm, D_CONV), BF16),
                        pltpu.VMEM((tm, D), F32), pltpu.VMEM((tm, D), F32), pltpu.VMEM((tm, D), BF16)]
        + ([_slab_scratch(tm, D)] if time_order else []),
        compiler_params=_params("arbitrary", "arbitrary"),
        name="merge",
    )(x, h, mods, gpost, wg, attn, u, u, u, cw, cb, lg, lb, rnn, wao, wco, wro, wout)


_FFN_CHUNK = 3072


def _ffn_kernel(xp_ref, x_ref, xn_ref, sh_ref, sc_ref, gt_ref, gpre_ref, gpost_ref, up_ref,
                dw_ref, db_ref, down_ref, o_ref, hp_ref, hn_ref, *slab):
    tm = x_ref.shape[0]
    t = pl.program_id(1)
    mod = lambda v: _modulated(v, gpre_ref[...], sh_ref[...], sc_ref[...])
    hp_ref[...] = mod(xp_ref[...])
    hn_ref[...] = mod(xn_ref[...])

    @pl.when(t == 0)
    def _():
        hp_ref[...] = _seam_prev(hp_ref[...])

    @pl.when(t == pl.num_programs(1) - 1)
    def _():
        hn_ref[...] = _seam_next(hn_ref[...])

    h = jnp.concatenate([hp_ref[...], mod(x_ref[...]), hn_ref[...]], axis=0).astype(BF16)

    acc = jnp.zeros(x_ref.shape, F32)
    for c in range(D_FF // _FFN_CHUNK):
        cg = slice(c * _FFN_CHUNK, (c + 1) * _FFN_CHUNK)
        cv = slice(D_FF + c * _FFN_CHUNK, D_FF + (c + 1) * _FFN_CHUNK)
        ug = _mm(h, up_ref[:, cg])
        uv = _mm(h, up_ref[:, cv])
        yg = db_ref[:, cg]
        yv = db_ref[:, cv]
        for j in range(3):
            rows = slice(j * SEG, j * SEG + tm)
            yg = yg + ug[rows, :] * dw_ref[j:j + 1, cg]
            yv = yv + uv[rows, :] * dw_ref[j:j + 1, cv]
        act = (jax.nn.gelu(yg) * yv).astype(BF16)
        acc = acc + _mm(act, down_ref[cg, :])
    y = x_ref[...] + gt_ref[...] * _rms(acc, gpost_ref[...])
    if slab:
        _store_time_order(y, o_ref, slab[0])
    else:
        o_ref[...] = y


def _ffn_call(x, mods, l, mod_row, gpre, gpost, up, dw, db, down, *, tm, time_order_out=False):
    B, T, D = x.shape
    prev, nxt = _halo_specs(tm, SEG, D, T)
    if time_order_out:
        out_spec = _time_order_spec(tm, D)
        out_shape = jax.ShapeDtypeStruct((B, SEG, T // SEG, D), F32)
    else:
        out_spec = pl.BlockSpec((None, tm, D), lambda b, t: (b, t, 0))
        out_shape = jax.ShapeDtypeStruct((B, T, D), F32)
    return pl.pallas_call(
        _ffn_kernel,
        grid=(B, T // tm),
        in_specs=[prev, pl.BlockSpec((None, tm, D), lambda b, t: (b, t, 0)), nxt,
                  _mod_spec(l, 3, mod_row)(D), _mod_spec(l, 4, mod_row)(D), _mod_spec(l, 5, mod_row)(D),
                  _layer_spec(gpre, l), _layer_spec(gpost, l), _layer_spec(up, l),
                  _layer_spec(dw, l), _layer_spec(db, l), _layer_spec(down, l)],
        out_specs=out_spec,
        out_shape=out_shape,
        scratch_shapes=[pltpu.VMEM((SEG, D), F32), pltpu.VMEM((SEG, D), F32)]
        + ([_slab_scratch(tm, D)] if time_order_out else []),
        compiler_params=_params("arbitrary", "arbitrary"),
        name="conv_ffn",
    )(x, x, x, mods, mods, mods, gpre, gpost, up, dw, db, down)


def _seg_interleave(a):
    *lead, T, W = a.shape
    return a.reshape(*lead, SEG, T // SEG, W).swapaxes(-3, -2).reshape(*lead, T, W)


def _rope_tables(n):
    rows = n // GRID_W
    row = jnp.repeat(jnp.arange(rows), GRID_W).astype(F32)
    col = jnp.tile(jnp.arange(GRID_W), rows).astype(F32)
    n_freq = HEAD_DIM // 4
    freq = ROPE_THETA ** (-jnp.arange(n_freq, dtype=F32) / n_freq)
    ang = jnp.concatenate([row[:, None] * freq, col[:, None] * freq], axis=-1)
    cos, sin = jnp.cos(ang), jnp.sin(ang)
    cos_t = jnp.tile(jnp.concatenate([cos, cos], axis=-1), (1, N_Q_HEADS))
    sin_t = jnp.tile(jnp.concatenate([-sin, sin], axis=-1), (1, N_Q_HEADS))
    return cos_t, sin_t


def _block_diag(w):
    *lead, nb, bi, bj = w.shape
    eye = jnp.eye(nb, dtype=w.dtype)
    return jnp.einsum('...hij,hg->...higj', w, eye).reshape(*lead, nb * bi, nb * bj)


def kernel(x, c, ctx, c_ctx, w_mod, b_mod, norm_pre_mix, norm_post_mix, norm_pre_ffn, norm_post_ffn,
           w_in, q_norm, k_norm, w_attn_out, conv_dw, conv_dw_b, conv_ln_g, conv_ln_b, w_conv_out,
           rnn_conv_w, rnn_conv_b, rnn_wa, rnn_ba, rnn_wx, rnn_bx, rnn_lambda, w_rnn_out, w_out,
           ffn_up, ffn_dw, ffn_dw_b, ffn_down):
    B, N, D = x.shape
    M = ctx.shape[1]
    L = w_mod.shape[0]
    tm_lat = min(N, 512)
    tq_lat = min(N, 1024)
    tm_ctx = min(M, 256)
    chunk_lat = min(N, 256)
    chunk_ctx = min(M, 256)

    mod_rows = 2 * SUBLANES
    cc = jnp.concatenate([c, c_ctx[None, :], jnp.zeros((mod_rows - B - 1, D), F32)], axis=0)
    mods = _mod_call(cc, w_mod, b_mod[:, None, :]).reshape(L, mod_rows, N_MOD, 1, D)

    cos_t, sin_t = (_seg_interleave(t) for t in _rope_tables(N))
    head_id = jnp.arange(D_Q) // HEAD_DIM
    bd = (head_id[:, None] == head_id[None, :]).astype(BF16)
    h0 = jnp.zeros((B, 2, SEG, D_RNN), F32)

    vec = lambda a: a[:, None, :]
    w_proj = w_gate = w_in.astype(BF16)
    qg = vec(jnp.tile(q_norm, (1, N_Q_HEADS)))
    kg = vec(jnp.tile(k_norm, (1, N_KV_HEADS)))
    rnn_w = jnp.concatenate([_block_diag(rnn_wa), _block_diag(rnn_wx)], axis=-1).astype(BF16)
    rnn_b = jnp.concatenate([rnn_ba, rnn_bx], axis=-1)[:, :, None, :]
    rnn_cb = rnn_conv_b[:, :, None, :]
    lam = rnn_lambda[:, :, None, :]
    wao, wco, wro, wout = (w_attn_out.astype(BF16), w_conv_out.astype(BF16), w_rnn_out.astype(BF16),
                           w_out.astype(BF16))
    up, down = ffn_up.astype(BF16), ffn_down.astype(BF16)
    gpre_mix, gpost_mix, gpre_ffn, gpost_ffn = (vec(norm_pre_mix), vec(norm_post_mix), vec(norm_pre_ffn),
                                                vec(norm_post_ffn))
    cdb, clg, clb, fdb = vec(conv_dw_b), vec(conv_ln_g), vec(conv_ln_b), vec(ffn_dw_b)

    x = x.reshape(B, SEG, N // SEG, D)
    ctx = ctx.reshape(B, SEG, M // SEG, D)
    for l in range(L):
        need_ctx = l < L - 1
        hcx, qc, kc, vc, uc, zxc, zrc = _inproj_call(ctx, mods, l, B, gpre_mix, w_proj, qg, kg, bd, None, None,
                                                     tm=tm_ctx)
        rnn_c, hc_last = _rnn_call(zxc, zrc, h0, l, rnn_conv_w, rnn_cb, rnn_w, rnn_b, lam, chunk=chunk_ctx)

        hx, q, k, v, u, zx, zr = _inproj_call(x, mods, l, None, gpre_mix, w_proj, qg, kg, bd, cos_t, sin_t,
                                              tm=tm_lat)
        attn = _attn_call(q, [(kc, vc), (k, v)], tq=tq_lat)
        rnn, _ = _rnn_call(zx, zr, hc_last, l, rnn_conv_w, rnn_cb, rnn_w, rnn_b, lam, chunk=chunk_lat)
        x = _merge_call(x, hx, mods, l, None, gpost_mix, w_gate, attn, u, conv_dw, cdb, clg, clb,
                        rnn, wao, wco, wro, wout, tm=tm_lat)
        x = _ffn_call(x, mods, l, None, gpre_ffn, gpost_ffn, up, ffn_dw, fdb, down, tm=tm_lat,
                      time_order_out=not need_ctx)

        if need_ctx:
            attn_c = _attn_call(qc, [(kc, vc)], tq=tm_ctx)
            ctx = _merge_call(ctx, hcx, mods, l, B, gpost_mix, w_gate, attn_c, uc, conv_dw, cdb, clg, clb,
                              rnn_c, wao, wco, wro, wout, tm=tm_ctx)
            ctx = _ffn_call(ctx, mods, l, B, gpre_ffn, gpost_ffn, up, ffn_dw, fdb, down, tm=tm_ctx)
    return x.reshape(B, N, D)
```

```python
import functools
import math

import jax
import jax.numpy as jnp
from jax import lax
from jax.experimental import pallas as pl
from jax.experimental.pallas import tpu as pltpu

F32 = jnp.float32
BF16 = jnp.bfloat16

EPS = 1e-6
HEAD_DIM = 64
N_Q_HEADS = 8
N_KV_HEADS = 2
Q_GROUP = N_Q_HEADS // N_KV_HEADS
D_Q = N_Q_HEADS * HEAD_DIM
D_KV = N_KV_HEADS * HEAD_DIM
D_CONV = 512
CONV_K = 31
D_RNN = 512
RNN_CONV_K = 4
LRU_C = 8.0
D_FF = 3072
GRID_W = 64
ROPE_THETA = 10000.0
N_MOD = 6

LANES = 128
SUBLANES = 8
SEG = SUBLANES
VMEM_LIMIT = 56 * 1024 * 1024

_OFF_Q = 0
_OFF_KV = D_Q
_OFF_CONV = _OFF_KV + 2 * D_KV
_OFF_ZX = _OFF_CONV + 2 * D_CONV
_OFF_ZR = _OFF_ZX + D_RNN
_OFF_GATE = _OFF_ZR + D_RNN

_Q_SCALE = HEAD_DIM ** -0.5 * math.log2(math.e)


def _const_spec(shape):
    zeros = (0,) * len(shape)
    return pl.BlockSpec(tuple(shape), lambda *_: zeros, pipeline_mode=pl.Buffered(1))


def _layer_spec(arr, l):
    tail = (0,) * (arr.ndim - 1)
    return pl.BlockSpec((None,) + tuple(arr.shape[1:]), lambda *_: (l,) + tail,
                        pipeline_mode=pl.Buffered(1))


def _mod_spec(l, k, row):
    def index(b, *_):
        return (l, b if row is None else row, k, 0, 0)
    return lambda d: pl.BlockSpec((None, None, None, 1, d), index)


def _params(*sem, flags=None):
    return pltpu.CompilerParams(dimension_semantics=sem, vmem_limit_bytes=VMEM_LIMIT, flags=flags)


def _rms(x, g):
    return x * lax.rsqrt(jnp.mean(x * x, axis=-1, keepdims=True) + EPS) * g


def _mm(a, b):
    return jnp.dot(a, b, preferred_element_type=F32)


def _modulated(x, g, shift, scale):
    return _rms(x, g) * (1.0 + scale) + shift


def _seam_prev(a):
    sub = lax.broadcasted_iota(jnp.int32, a.shape, 0) % SEG
    return jnp.where(sub == 0, 0.0, pltpu.roll(a, 1, 0))


def _seam_next(a):
    sub = lax.broadcasted_iota(jnp.int32, a.shape, 0) % SEG
    return jnp.where(sub == SEG - 1, 0.0, pltpu.roll(a, a.shape[0] - 1, 0))


def _time_order_spec(tm, width):
    return pl.BlockSpec((None, SEG, tm // SEG, width), lambda b, t: (b, 0, t, 0))


def _slab_scratch(tm, width):
    return pltpu.VMEM((width // LANES, tm, LANES), F32)


def _load_interleaved(x_ref, slab_ref):
    seg, steps, width = x_ref.shape
    for j in range(width // LANES):
        for s in range(seg):
            slab_ref[j, pl.ds(s, steps, stride=seg), :] = x_ref[s, :, j * LANES:(j + 1) * LANES]


def _read_slabs(slab_ref):
    return jnp.concatenate([slab_ref[j] for j in range(slab_ref.shape[0])], axis=1)


def _store_time_order(y, o_ref, slab_ref):
    seg, steps, width = o_ref.shape
    for j in range(width // LANES):
        slab_ref[j] = y[:, j * LANES:(j + 1) * LANES]
    for j in range(width // LANES):
        for s in range(seg):
            o_ref[s, :, j * LANES:(j + 1) * LANES] = slab_ref[j, pl.ds(s, steps, stride=seg), :]


def _halo_specs(tm, halo, width, n_rows):
    per = tm // halo
    n_blocks = n_rows // halo
    prev = pl.BlockSpec((None, halo, width), lambda b, t: (b, (t * per + n_blocks - 1) % n_blocks, 0))
    nxt = pl.BlockSpec((None, halo, width), lambda b, t: (b, ((t + 1) * per) % n_blocks, 0))
    return prev, nxt


def _mod_kernel(c_ref, w_ref, b_ref, o_ref):
    c = c_ref[...]
    a = (c * jax.nn.sigmoid(c)).astype(BF16)
    o_ref[...] = _mm(a, w_ref[...].astype(BF16)) + b_ref[...]


def _mod_call(cc, w_mod, b_mod):
    L, D, D6 = w_mod.shape
    R = cc.shape[0]
    bn = D6 // 4
    return pl.pallas_call(
        _mod_kernel,
        grid=(L, D6 // bn),
        in_specs=[
            pl.BlockSpec((R, D), lambda l, j: (0, 0)),
            pl.BlockSpec((None, D, bn), lambda l, j: (l, 0, j)),
            pl.BlockSpec((None, 1, bn), lambda l, j: (l, 0, j)),
        ],
        out_specs=pl.BlockSpec((None, R, bn), lambda l, j: (l, 0, j)),
        out_shape=jax.ShapeDtypeStruct((L, R, D6), F32),
        compiler_params=_params("arbitrary", "arbitrary"),
        name="adaln_mod",
    )(cc, w_mod, b_mod)


def _head_rms(z, gain, bd):
    sq = z * z
    hi = sq.astype(BF16)
    lo = (sq - hi.astype(F32)).astype(BF16)
    ms = (_mm(hi, bd) + _mm(lo, bd)) * (1.0 / HEAD_DIM)
    return z * lax.rsqrt(ms + EPS) * gain


def _rope(x, c, s):
    rows, width = x.shape
    lane = lax.broadcasted_iota(jnp.int32, (rows, LANES), 1)
    first_half = (lane & (HEAD_DIM // 2)) == 0
    cols = []
    for j in range(width // LANES):
        sl = slice(j * LANES, (j + 1) * LANES)
        xc = x[:, sl]
        partner = jnp.where(first_half,
                            pltpu.roll(xc, LANES - HEAD_DIM // 2, 1),
                            pltpu.roll(xc, HEAD_DIM // 2, 1))
        cols.append(xc * c[:, sl] + partner * s[:, sl])
    return jnp.concatenate(cols, axis=1) if len(cols) > 1 else cols[0]


def _inproj_kernel(*refs, rope, time_order):
    x_ref, sh_ref, sc_ref, g_ref, w_ref, qg_ref, kg_ref, bd_ref = refs[:8]
    refs = refs[8:]
    if rope:
        cos_ref, sin_ref = refs[:2]
        refs = refs[2:]
    h_ref, q_ref, k_ref, v_ref, u_ref, zx_ref, zr_ref = refs[:7]
    if time_order:
        _load_interleaved(x_ref, refs[7])
        x = _read_slabs(refs[7])
    else:
        x = x_ref[...]
    h = _modulated(x, g_ref[...], sh_ref[...], sc_ref[...]).astype(BF16)
    h_ref[...] = h

    zq = _mm(h, w_ref[:, _OFF_Q:_OFF_Q + D_Q])
    qn = _head_rms(zq, qg_ref[...], bd_ref[...])
    if rope:
        qn = _rope(qn, cos_ref[...], sin_ref[...])
    q_ref[...] = (qn * _Q_SCALE).astype(BF16)

    zkv = _mm(h, w_ref[:, _OFF_KV:_OFF_KV + 2 * D_KV])
    kn = _head_rms(zkv[:, :D_KV], kg_ref[...], bd_ref[:D_KV, :D_KV])
    if rope:
        kn = _rope(kn, cos_ref[:, :D_KV], sin_ref[:, :D_KV])
    k_ref[...] = kn
    v_ref[...] = zkv[:, D_KV:]

    zc = _mm(h, w_ref[:, _OFF_CONV:_OFF_CONV + 2 * D_CONV])
    u_ref[...] = zc[:, :D_CONV] * jax.nn.sigmoid(zc[:, D_CONV:])

    zx_ref[...] = _mm(h, w_ref[:, _OFF_ZX:_OFF_ZX + D_RNN])
    zr_ref[...] = _mm(h, w_ref[:, _OFF_ZR:_OFF_ZR + D_RNN])


def _inproj_call(x, mods, l, mod_row, gain, w, qg, kg, bd, cos, sin, *, tm):
    time_order = x.ndim == 4
    B, D = x.shape[0], x.shape[-1]
    T = x.shape[1] * x.shape[2] if time_order else x.shape[1]
    rope = cos is not None
    tok = lambda w_: pl.BlockSpec((None, tm, w_), lambda b, t: (b, t, 0))
    x_spec = _time_order_spec(tm, D) if time_order else tok(D)
    in_specs = [x_spec, _mod_spec(l, 0, mod_row)(D), _mod_spec(l, 1, mod_row)(D),
                _layer_spec(gain, l), _layer_spec(w, l), _layer_spec(qg, l), _layer_spec(kg, l),
                _const_spec(bd.shape)]
    args = [x, mods, mods, gain, w, qg, kg, bd]
    if rope:
        tab = pl.BlockSpec((tm, D_Q), lambda b, t: (t, 0))
        in_specs += [tab, tab]
        args += [cos, sin]
    out_shape = [jax.ShapeDtypeStruct((B, T, D), BF16),
                 jax.ShapeDtypeStruct((B, T, D_Q), BF16),
                 jax.ShapeDtypeStruct((B, T, D_KV), F32),
                 jax.ShapeDtypeStruct((B, T, D_KV), F32),
                 jax.ShapeDtypeStruct((B, T, D_CONV), F32),
                 jax.ShapeDtypeStruct((B, T, D_RNN), F32),
                 jax.ShapeDtypeStruct((B, T, D_RNN), F32)]
    out_specs = [tok(D), tok(D_Q), tok(D_KV), tok(D_KV), tok(D_CONV), tok(D_RNN), tok(D_RNN)]
    return pl.pallas_call(
        functools.partial(_inproj_kernel, rope=rope, time_order=time_order),
        grid=(B, T // tm),
        in_specs=in_specs, out_specs=out_specs, out_shape=out_shape,
        scratch_shapes=[_slab_scratch(tm, D)] if time_order else [],
        compiler_params=_params("arbitrary", "arbitrary"),
        name="inproj_rope" if rope else "inproj",
    )(*args)


_VT_ROWS = HEAD_DIM + 2 * SUBLANES
_ATTN_KEYS = 256
_ATTN_SAFE_BOUND = 60.0


def _attn_kernel(*refs, seg_lens):
    nseg = len(seg_lens)
    q_ref = refs[0]
    kv_refs = refs[1:1 + 2 * nseg]
    o_ref = refs[1 + 2 * nseg]
    kq_ref, vt_ref, kn_ref = refs[2 + 2 * nseg:]
    m_tot = sum(seg_lens)
    g = pl.program_id(1)

    @pl.when(pl.program_id(2) == 0)
    def _():
        off = 0
        k_norm2 = jnp.zeros((1, 1), F32)
        for i, ms in enumerate(seg_lens):
            lane_half = lax.broadcasted_iota(jnp.int32, (ms, LANES), 1) // HEAD_DIM
            k = kv_refs[2 * i][...]
            kd = jnp.where(lane_half == g, k, pltpu.roll(k, HEAD_DIM, 1)).astype(BF16)
            kq_ref[off:off + ms, :] = jnp.concatenate([kd, kd], axis=1)
            own = jnp.where(lane_half == g, k, 0.0)
            k_norm2 = jnp.maximum(k_norm2, jnp.max(jnp.sum(own * own, axis=1, keepdims=True),
                                                   axis=0, keepdims=True))
            v = kv_refs[2 * i + 1][...]
            vd = jnp.where(lane_half == g, v, pltpu.roll(v, HEAD_DIM, 1))
            vt_ref[0:HEAD_DIM, off:off + ms] = vd.T[0:HEAD_DIM, :].astype(BF16)
            off += ms
        vt_ref[HEAD_DIM:, :] = jnp.ones((_VT_ROWS - HEAD_DIM, m_tot), BF16)
        kn_ref[...] = jnp.broadcast_to(k_norm2, kn_ref.shape)

    q = q_ref[...]
    head_lane = lax.broadcasted_iota(jnp.int32, q.shape, 1) // HEAD_DIM
    q_heads = [jnp.where(head_lane == h, q, jnp.zeros_like(q)) for h in range(Q_GROUP)]

    def scores(h):
        return lax.dot_general(kq_ref[...], q_heads[h], (((1,), (1,)), ((), ())),
                               preferred_element_type=F32)

    def finish(outs):
        for c in range(Q_GROUP // 2):
            pair = jnp.concatenate([outs[2 * c], outs[2 * c + 1]], axis=0)
            o_ref[:, c * LANES:(c + 1) * LANES] = pair.T.astype(BF16)

    qf = q.astype(F32)
    row_id = lax.broadcasted_iota(jnp.int32, (2 * SUBLANES, q.shape[1]), 0)
    lane_id = lax.broadcasted_iota(jnp.int32, (2 * SUBLANES, q.shape[1]), 1) // HEAD_DIM
    head_rows = jnp.where(row_id == lane_id, 1.0, 0.0).astype(BF16)
    q_norm2 = lax.dot_general(head_rows, (qf * qf).astype(BF16), (((1,), (1,)), ((), ())),
                              preferred_element_type=F32)
    bound = jnp.sqrt(q_norm2 * kn_ref[0:1, 0:1] * (1.0 + 2.0 ** -7))
    safe = jnp.max(bound) <= _ATTN_SAFE_BOUND

    @pl.when(safe)
    def _():
        outs = []
        st_next = scores(0)
        for h in range(Q_GROUP):
            st = st_next
            if h + 1 < Q_GROUP:
                st_next = scores(h + 1)
            p = jnp.exp2(st - bound[h:h + 1, :]).astype(BF16)
            acc = _mm(vt_ref[...], p)
            outs.append(acc[0:HEAD_DIM, :] * (1.0 / acc[HEAD_DIM:HEAD_DIM + 1, :]))
        finish(outs)

    @pl.when(jnp.logical_not(safe))
    def _():
        keys = min(_ATTN_KEYS, m_tot)
        n_chunks = m_tot // keys
        outs = []
        st_next = scores(0)
        for h in range(Q_GROUP):
            st_all = st_next
            if h + 1 < Q_GROUP:
                st_next = scores(h + 1)
            for c in range(n_chunks):
                rows = slice(c * keys, (c + 1) * keys)
                st = st_all[rows, :]
                vt = vt_ref[:, rows]
                m_chunk = jnp.max(st, axis=0, keepdims=True)
                if c == 0:
                    m_run = m_chunk
                    acc = _mm(vt, jnp.exp2(st - m_run).astype(BF16))
                else:
                    m_new = jnp.maximum(m_run, m_chunk)
                    acc = acc * jnp.exp2(m_run - m_new) + _mm(vt, jnp.exp2(st - m_new).astype(BF16))
                    m_run = m_new
            outs.append(acc[0:HEAD_DIM, :] * (1.0 / acc[HEAD_DIM:HEAD_DIM + 1, :]))
        finish(outs)


def _attn_call(q, segs, *, tq):
    B, T, _ = q.shape
    seg_lens = tuple(k.shape[1] for k, _ in segs)
    m_tot = sum(seg_lens)
    gw = Q_GROUP * HEAD_DIM
    in_specs = [pl.BlockSpec((None, tq, gw), lambda b, g, t: (b, t, g))]
    args = [q]
    for k, v in segs:
        spec = pl.BlockSpec((None, k.shape[1], D_KV), lambda b, g, t: (b, 0, 0))
        in_specs += [spec, spec]
        args += [k, v]
    return pl.pallas_call(
        functools.partial(_attn_kernel, seg_lens=seg_lens),
        grid=(B, N_KV_HEADS, T // tq),
        in_specs=in_specs,
        out_specs=pl.BlockSpec((None, tq, gw), lambda b, g, t: (b, t, g)),
        out_shape=jax.ShapeDtypeStruct((B, T, D_Q), BF16),
        scratch_shapes=[pltpu.VMEM((m_tot, gw), BF16),
                        pltpu.VMEM((_VT_ROWS, m_tot), BF16),
                        pltpu.VMEM((SUBLANES, LANES), F32)],
        compiler_params=_params("arbitrary", "arbitrary", "arbitrary"),
        name="gqa_attention",
    )(*args)


_CONV_HALO = 16 * SEG
_CONV_ROWS = 32


def _conv_fill(up_ref, u_ref, un_ref, w_ref, pad_ref, wb_ref):
    tm = u_ref.shape[0]
    t = pl.program_id(1)
    for j in range(CONV_K):
        wb_ref[j * SEG:(j + 1) * SEG, :] = jnp.broadcast_to(w_ref[j:j + 1, :], (SEG, D_CONV))
    pad_ref[0:_CONV_HALO, :] = up_ref[...]
    pad_ref[_CONV_HALO:_CONV_HALO + tm, :] = u_ref[...]
    pad_ref[_CONV_HALO + tm:, :] = un_ref[...]

    @pl.when(t == 0)
    def _():
        pad_ref[0:_CONV_HALO, :] = _seam_prev(up_ref[...])

    @pl.when(t == pl.num_programs(1) - 1)
    def _():
        pad_ref[_CONV_HALO + tm:, :] = _seam_next(un_ref[...])


def _conv_rows(pad_ref, wb_ref, b_ref, lg_ref, lb_ref, o_ref, row_lo, row_hi):
    first = _CONV_HALO // SEG - CONV_K // 2
    for r in range(row_lo, row_hi, _CONV_ROWS):
        acc = jnp.zeros((_CONV_ROWS, D_CONV), F32)
        for j in range(CONV_K):
            lo = r + (first + j) * SEG
            w_tile = wb_ref[j * SEG:(j + 1) * SEG, :]
            acc = acc + pad_ref[lo:lo + _CONV_ROWS, :] * jnp.concatenate([w_tile] * (_CONV_ROWS // SEG), axis=0)
        acc = acc + b_ref[...]
        mu = jnp.mean(acc, axis=-1, keepdims=True)
        cen = acc - mu
        var = jnp.mean(cen * cen, axis=-1, keepdims=True)
        y = cen * lax.rsqrt(var + EPS) * lg_ref[...] + lb_ref[...]
        o_ref[r:r + _CONV_ROWS, :] = (y * jax.nn.sigmoid(y)).astype(BF16)


_RNN_HALO = 4 * SEG


_RNN_OUT_ROWS = 2 * SUBLANES


def _rnn_kernel(zx_ref, zr_ref, h0_ref, cw_ref, cb_ref, w_ref, b_ref, lam_ref, o_ref, hl_ref,
                xp_ref, a_ref, bb_ref, ac_ref, h_ref, hf_ref, *, chunk):
    T = zx_ref.shape[0]
    n_chunks = T // chunk
    steps = chunk // SEG

    xp_ref[0:_RNN_HALO, :] = _seam_prev(zx_ref[T - _RNN_HALO:T, :])
    xp_ref[_RNN_HALO + T:, :] = _seam_next(zx_ref[0:_RNN_HALO, :])

    def copy(i, carry):
        r = pl.multiple_of(i * chunk, chunk)
        xp_ref[pl.ds(_RNN_HALO + r, chunk), :] = zx_ref[pl.ds(r, chunk), :]
        return carry

    lax.fori_loop(0, n_chunks, copy, 0)

    lam = lam_ref[...]
    log_sig = jnp.minimum(lam, 0.0) - jnp.log1p(jnp.exp(-jnp.abs(lam)))
    half_c = (0.5 * LRU_C * math.log2(math.e)) * log_sig
    row = lax.broadcasted_iota(jnp.int32, (SEG, D_RNN), 0)

    def run(reverse):
        def chunk_body(c, carry):
            cc = (n_chunks - 1 - c) if reverse else c
            r0 = pl.multiple_of(cc * chunk, chunk)
            xc = jnp.zeros((chunk, D_RNN), F32)
            for j in range(RNN_CONV_K):
                step = (RNN_CONV_K - 1 - j) if reverse else (j - RNN_CONV_K + 1)
                xc = xc + xp_ref[pl.ds(r0 + _RNN_HALO + step * SEG, chunk), :] * cw_ref[j:j + 1, :]
            xc = xc + cb_ref[...]
            gates = _mm(xc.astype(BF16), w_ref[...]) + b_ref[...]
            tanh_r = jnp.tanh(0.5 * gates[:, :D_RNN])
            i_gate = 0.5 + 0.5 * jnp.tanh(0.5 * gates[:, D_RNN:])
            a = jnp.exp2(half_c + half_c * tanh_r)
            a_ref[...] = a
            bb_ref[...] = jnp.sqrt((1.0 - a) * (1.0 + a)) * (i_gate * xc)

            def step_body(k, state):
                a_cum, h_zero = state
                kk = (steps - 1 - k) if reverse else k
                t0 = pl.multiple_of(kk * SEG, SEG)
                at = a_ref[pl.ds(t0, SEG), :]
                h_zero = at * h_zero + bb_ref[pl.ds(t0, SEG), :]
                a_cum = at * a_cum
                h_ref[pl.ds(r0 + t0, SEG), :] = h_zero
                ac_ref[pl.ds(r0 + t0, SEG), :] = a_cum
                return a_cum, h_zero

            return lax.fori_loop(0, steps, step_body, carry, unroll=8)

        a_end, h_end = lax.fori_loop(0, n_chunks, chunk_body,
                                     (jnp.ones((SEG, D_RNN), F32), jnp.zeros((SEG, D_RNN), F32)))

        start = SEG - 1 if reverse else 0
        state_in = jnp.where(row == start, h0_ref[...], 0.0)
        for i in range(1, SEG):
            s = start - i if reverse else start + i
            leaving = h_end + a_end * state_in
            state_in = jnp.where(row == s, pltpu.roll(leaving, SEG - 1 if reverse else 1, 0), state_in)

        state_rows = jnp.concatenate([state_in] * (_RNN_OUT_ROWS // SEG), axis=0)

        def fix(k, carry):
            t0 = pl.multiple_of(k * _RNN_OUT_ROWS, _RNN_OUT_ROWS)
            rows = pl.ds(t0, _RNN_OUT_ROWS)
            h = h_ref[rows, :] + ac_ref[rows, :] * state_rows
            if reverse:
                o_ref[rows, :] = ((hf_ref[rows, :] + h) * jax.nn.gelu(zr_ref[rows, :])).astype(BF16)
            else:
                hf_ref[rows, :] = h
            return carry

        lax.fori_loop(0, T // _RNN_OUT_ROWS, fix, 0, unroll=4)
        edge = slice(0, SEG) if reverse else slice(T - SEG, T)
        last_rows = h_ref[edge, :] + ac_ref[edge, :] * state_in
        final = last_rows[0:1, :] if reverse else last_rows[SEG - 1:SEG, :]
        hl_ref[...] = jnp.broadcast_to(final, (SEG, D_RNN))

    d = pl.program_id(1)
    pl.when(d == 0)(lambda: run(False))
    pl.when(d == 1)(lambda: run(True))


def _rnn_call(zx, zr, h0, l, cw, cb, w, b, lam, *, chunk):
    B, T, C = zx.shape
    per_dir = lambda a: pl.BlockSpec((None, None) + tuple(a.shape[2:]), lambda b_, d: (l, d, 0, 0))
    state = pl.BlockSpec((None, None, SEG, C), lambda b_, d: (b_, d, 0, 0))
    seq = pl.BlockSpec((None, T, C), lambda b_, d: (b_, 0, 0))
    return pl.pallas_call(
        functools.partial(_rnn_kernel, chunk=chunk),
        grid=(B, 2),
        in_specs=[seq, seq, state, per_dir(cw), per_dir(cb), per_dir(w), per_dir(b), per_dir(lam)],
        out_specs=[seq, state],
        out_shape=[jax.ShapeDtypeStruct((B, T, C), BF16),
                   jax.ShapeDtypeStruct((B, 2, SEG, C), F32)],
        scratch_shapes=[pltpu.VMEM((T + 2 * _RNN_HALO, C), F32),
                        pltpu.VMEM((chunk, C), F32), pltpu.VMEM((chunk, C), F32),
                        pltpu.VMEM((T, C), F32), pltpu.VMEM((T, C), F32), pltpu.VMEM((T, C), F32)],
        compiler_params=_params("arbitrary", "arbitrary"),
        name="rglru",
    )(zx, zr, h0, cw, cb, w, b, lam)


_MERGE_COLS = 256


def _merge_kernel(x_ref, h_ref, gt_ref, gpost_ref, wg_ref, at_ref,
                  up_ref, u_ref, un_ref, cw_ref, cb_ref, lg_ref, lb_ref,
                  rn_ref, wao_ref, wco_ref, wro_ref, wout_ref, o_ref, pad_ref, wb_ref, cv_ref, m_ref, cp_ref,
                  mb_ref, *slab):
    D = x_ref.shape[-1]
    _conv_fill(up_ref, u_ref, un_ref, cw_ref, pad_ref, wb_ref)
    tm = u_ref.shape[0]
    conv = functools.partial(_conv_rows, pad_ref, wb_ref, cb_ref, lg_ref, lb_ref, cv_ref)

    def gate(branch, cols):
        first = _OFF_GATE + branch * D
        return jax.nn.sigmoid(_mm(h_ref[...], wg_ref[:, first + cols.start:first + cols.stop]))

    quarter = tm // 4
    half = D // 2
    groups = ((at_ref, wao_ref, 0, 0), (at_ref, wao_ref, 0, half), (rn_ref, wro_ref, 2, 0),
              (rn_ref, wro_ref, 2, half))
    for i, (act_ref, w_ref, branch, col0) in enumerate(groups):
        for c in range(col0, col0 + half, _MERGE_COLS):
            cols = slice(c, c + _MERGE_COLS)
            term = gate(branch, cols) * _mm(act_ref[...], w_ref[:, cols])
            m_ref[:, cols] = term if branch == 0 else m_ref[:, cols] + term
        rows = slice(i * quarter, (i + 1) * quarter)
        conv(rows.start, rows.stop)
        cp_ref[rows, :] = _mm(cv_ref[rows, :], wco_ref[...])
    for c in range(0, D, _MERGE_COLS):
        cols = slice(c, c + _MERGE_COLS)
        mb_ref[:, cols] = (m_ref[:, cols] + gate(1, cols) * cp_ref[:, cols]).astype(BF16)
    o = _mm(mb_ref[...], wout_ref[...])
    if slab:
        _load_interleaved(x_ref, slab[0])
        x = _read_slabs(slab[0])
    else:
        x = x_ref[...]
    o_ref[...] = x + gt_ref[...] * _rms(o, gpost_ref[...])


def _merge_call(x, h, mods, l, mod_row, gpost, wg, attn, u, cw, cb, lg, lb, rnn, wao, wco, wro, wout, *, tm):
    time_order = x.ndim == 4
    B, D = x.shape[0], x.shape[-1]
    T = attn.shape[1]
    tok = lambda w_: pl.BlockSpec((None, tm, w_), lambda b, t: (b, t, 0))
    u_prev, u_next = _halo_specs(tm, _CONV_HALO, D_CONV, T)
    return pl.pallas_call(
        _merge_kernel,
        grid=(B, T // tm),
        in_specs=[_time_order_spec(tm, D) if time_order else tok(D), tok(D),
                  _mod_spec(l, 2, mod_row)(D), _layer_spec(gpost, l),
                  _layer_spec(wg, l), tok(D_Q),
                  u_prev, tok(D_CONV), u_next, _layer_spec(cw, l), _layer_spec(cb, l),
                  _layer_spec(lg, l), _layer_spec(lb, l), tok(D_RNN),
                  _layer_spec(wao, l), _layer_spec(wco, l), _layer_spec(wro, l), _layer_spec(wout, l)],
        out_specs=tok(D),
        out_shape=jax.ShapeDtypeStruct((B, T, D), F32),
        scratch_shapes=[pltpu.VMEM((tm + 2 * _CONV_HALO, D_CONV), F32),
                        pltpu.VMEM((CONV_K * SEG, D_CONV), F32), pltpu.VMEM((tm, D_CONV), BF16),
                        pltpu.VMEM((tm, D), F32), pltpu.VMEM((tm, D), F32), pltpu.VMEM((tm, D), BF16)]
        + ([_slab_scratch(tm, D)] if time_order else []),
        compiler_params=_params("arbitrary", "arbitrary"),
        name="merge",
    )(x, h, mods, gpost, wg, attn, u, u, u, cw, cb, lg, lb, rnn, wao, wco, wro, wout)


_FFN_CHUNK = 3072


def _ffn_kernel(xp_ref, x_ref, xn_ref, sh_ref, sc_ref, gt_ref, gpre_ref, gpost_ref, up_ref,
                dw_ref, db_ref, down_ref, o_ref, hp_ref, hn_ref, *slab):
    tm = x_ref.shape[0]
    t = pl.program_id(1)
    mod = lambda v: _modulated(v, gpre_ref[...], sh_ref[...], sc_ref[...])
    hp_ref[...] = mod(xp_ref[...])
    hn_ref[...] = mod(xn_ref[...])

    @pl.when(t == 0)
    def _():
        hp_ref[...] = _seam_prev(hp_ref[...])

    @pl.when(t == pl.num_programs(1) - 1)
    def _():
        hn_ref[...] = _seam_next(hn_ref[...])

    h = jnp.concatenate([hp_ref[...], mod(x_ref[...]), hn_ref[...]], axis=0).astype(BF16)

    acc = jnp.zeros(x_ref.shape, F32)
    for c in range(D_FF // _FFN_CHUNK):
        cg = slice(c * _FFN_CHUNK, (c + 1) * _FFN_CHUNK)
        cv = slice(D_FF + c * _FFN_CHUNK, D_FF + (c + 1) * _FFN_CHUNK)
        ug = _mm(h, up_ref[:, cg])
        uv = _mm(h, up_ref[:, cv])
        yg = db_ref[:, cg]
        yv = db_ref[:, cv]
        for j in range(3):
            rows = slice(j * SEG, j * SEG + tm)
            yg = yg + ug[rows, :] * dw_ref[j:j + 1, cg]
            yv = yv + uv[rows, :] * dw_ref[j:j + 1, cv]
        act = (jax.nn.gelu(yg) * yv).astype(BF16)
        acc = acc + _mm(act, down_ref[cg, :])
    y = x_ref[...] + gt_ref[...] * _rms(acc, gpost_ref[...])
    if slab:
        _store_time_order(y, o_ref, slab[0])
    else:
        o_ref[...] = y


def _ffn_call(x, mods, l, mod_row, gpre, gpost, up, dw, db, down, *, tm, time_order_out=False):
    B, T, D = x.shape
    prev, nxt = _halo_specs(tm, SEG, D, T)
    if time_order_out:
        out_spec = _time_order_spec(tm, D)
        out_shape = jax.ShapeDtypeStruct((B, SEG, T // SEG, D), F32)
    else:
        out_spec = pl.BlockSpec((None, tm, D), lambda b, t: (b, t, 0))
        out_shape = jax.ShapeDtypeStruct((B, T, D), F32)
    return pl.pallas_call(
        _ffn_kernel,
        grid=(B, T // tm),
        in_specs=[prev, pl.BlockSpec((None, tm, D), lambda b, t: (b, t, 0)), nxt,
                  _mod_spec(l, 3, mod_row)(D), _mod_spec(l, 4, mod_row)(D), _mod_spec(l, 5, mod_row)(D),
                  _layer_spec(gpre, l), _layer_spec(gpost, l), _layer_spec(up, l),
                  _layer_spec(dw, l), _layer_spec(db, l), _layer_spec(down, l)],
        out_specs=out_spec,
        out_shape=out_shape,
        scratch_shapes=[pltpu.VMEM((SEG, D), F32), pltpu.VMEM((SEG, D), F32)]
        + ([_slab_scratch(tm, D)] if time_order_out else []),
        compiler_params=_params("arbitrary", "arbitrary"),
        name="conv_ffn",
    )(x, x, x, mods, mods, mods, gpre, gpost, up, dw, db, down)


def _seg_interleave(a):
    *lead, T, W = a.shape
    return a.reshape(*lead, SEG, T // SEG, W).swapaxes(-3, -2).reshape(*lead, T, W)


def _rope_tables(n):
    rows = n // GRID_W
    row = jnp.repeat(jnp.arange(rows), GRID_W).astype(F32)
    col = jnp.tile(jnp.arange(GRID_W), rows).astype(F32)
    n_freq = HEAD_DIM // 4
    freq = ROPE_THETA ** (-jnp.arange(n_freq, dtype=F32) / n_freq)
    ang = jnp.concatenate([row[:, None] * freq, col[:, None] * freq], axis=-1)
    cos, sin = jnp.cos(ang), jnp.sin(ang)
    cos_t = jnp.tile(jnp.concatenate([cos, cos], axis=-1), (1, N_Q_HEADS))
    sin_t = jnp.tile(jnp.concatenate([-sin, sin], axis=-1), (1, N_Q_HEADS))
    return cos_t, sin_t


def _block_diag(w):
    *lead, nb, bi, bj = w.shape
    eye = jnp.eye(nb, dtype=w.dtype)
    return jnp.einsum('...hij,hg->...higj', w, eye).reshape(*lead, nb * bi, nb * bj)


def kernel(x, c, ctx, c_ctx, w_mod, b_mod, norm_pre_mix, norm_post_mix, norm_pre_ffn, norm_post_ffn,
           w_in, q_norm, k_norm, w_attn_out, conv_dw, conv_dw_b, conv_ln_g, conv_ln_b, w_conv_out,
           rnn_conv_w, rnn_conv_b, rnn_wa, rnn_ba, rnn_wx, rnn_bx, rnn_lambda, w_rnn_out, w_out,
           ffn_up, ffn_dw, ffn_dw_b, ffn_down):
    B, N, D = x.shape
    M = ctx.shape[1]
    L = w_mod.shape[0]
    tm_lat = min(N, 512)
    tq_lat = min(N, 1024)
    tm_ctx = min(M, 256)
    chunk_lat = min(N, 512)
    chunk_ctx = min(M, 256)

    mod_rows = 2 * SUBLANES
    cc = jnp.concatenate([c, c_ctx[None, :], jnp.zeros((mod_rows - B - 1, D), F32)], axis=0)
    mods = _mod_call(cc, w_mod, b_mod[:, None, :]).reshape(L, mod_rows, N_MOD, 1, D)

    cos_t, sin_t = (_seg_interleave(t) for t in _rope_tables(N))
    head_id = jnp.arange(D_Q) // HEAD_DIM
    bd = (head_id[:, None] == head_id[None, :]).astype(BF16)
    h0 = jnp.zeros((B, 2, SEG, D_RNN), F32)

    vec = lambda a: a[:, None, :]
    w_proj = w_gate = w_in.astype(BF16)
    qg = vec(jnp.tile(q_norm, (1, N_Q_HEADS)))
    kg = vec(jnp.tile(k_norm, (1, N_KV_HEADS)))
    rnn_w = jnp.concatenate([_block_diag(rnn_wa), _block_diag(rnn_wx)], axis=-1).astype(BF16)
    rnn_b = jnp.concatenate([rnn_ba, rnn_bx], axis=-1)[:, :, None, :]
    rnn_cb = rnn_conv_b[:, :, None, :]
    lam = rnn_lambda[:, :, None, :]
    wao, wco, wro, wout = (w_attn_out.astype(BF16), w_conv_out.astype(BF16), w_rnn_out.astype(BF16),
                           w_out.astype(BF16))
    up, down = ffn_up.astype(BF16), ffn_down.astype(BF16)
    gpre_mix, gpost_mix, gpre_ffn, gpost_ffn = (vec(norm_pre_mix), vec(norm_post_mix), vec(norm_pre_ffn),
                                                vec(norm_post_ffn))
    cdb, clg, clb, fdb = vec(conv_dw_b), vec(conv_ln_g), vec(conv_ln_b), vec(ffn_dw_b)

    x = x.reshape(B, SEG, N // SEG, D)
    ctx = ctx.reshape(B, SEG, M // SEG, D)
    for l in range(L):
        need_ctx = l < L - 1
        hcx, qc, kc, vc, uc, zxc, zrc = _inproj_call(ctx, mods, l, B, gpre_mix, w_proj, qg, kg, bd, None, None,
                                                     tm=tm_ctx)
        rnn_c, hc_last = _rnn_call(zxc, zrc, h0, l, rnn_conv_w, rnn_cb, rnn_w, rnn_b, lam, chunk=chunk_ctx)

        hx, q, k, v, u, zx, zr = _inproj_call(x, mods, l, None, gpre_mix, w_proj, qg, kg, bd, cos_t, sin_t,
                                              tm=tm_lat)
        attn = _attn_call(q, [(kc, vc), (k, v)], tq=tq_lat)
        rnn, _ = _rnn_call(zx, zr, hc_last, l, rnn_conv_w, rnn_cb, rnn_w, rnn_b, lam, chunk=chunk_lat)
        x = _merge_call(x, hx, mods, l, None, gpost_mix, w_gate, attn, u, conv_dw, cdb, clg, clb,
                        rnn, wao, wco, wro, wout, tm=tm_lat)
        x = _ffn_call(x, mods, l, None, gpre_ffn, gpost_ffn, up, ffn_dw, fdb, down, tm=tm_lat,
                      time_order_out=not need_ctx)

        if need_ctx:
            attn_c = _attn_call(qc, [(kc, vc)], tq=tm_ctx)
            ctx = _merge_call(ctx, hcx, mods, l, B, gpost_mix, w_gate, attn_c, uc, conv_dw, cdb, clg, clb,
                              rnn_c, wao, wco, wro, wout, tm=tm_ctx)
            ctx = _ffn_call(ctx, mods, l, B, gpre_ffn, gpost_ffn, up, ffn_dw, fdb, down, tm=tm_ctx)
    return x.reshape(B, N, D)
```

```python
import functools
import math

import jax
import jax.numpy as jnp
from jax import lax
from jax.experimental import pallas as pl
from jax.experimental.pallas import tpu as pltpu

F32 = jnp.float32
BF16 = jnp.bfloat16

EPS = 1e-6
HEAD_DIM = 64
N_Q_HEADS = 8
N_KV_HEADS = 2
Q_GROUP = N_Q_HEADS // N_KV_HEADS
D_Q = N_Q_HEADS * HEAD_DIM
D_KV = N_KV_HEADS * HEAD_DIM
D_CONV = 512
CONV_K = 31
D_RNN = 512
RNN_CONV_K = 4
LRU_C = 8.0
D_FF = 3072
GRID_W = 64
ROPE_THETA = 10000.0
N_MOD = 6

LANES = 128
SUBLANES = 8
SEG = SUBLANES
VMEM_LIMIT = 56 * 1024 * 1024

_OFF_Q = 0
_OFF_KV = D_Q
_OFF_CONV = _OFF_KV + 2 * D_KV
_OFF_ZX = _OFF_CONV + 2 * D_CONV
_OFF_ZR = _OFF_ZX + D_RNN
_OFF_GATE = _OFF_ZR + D_RNN

_Q_SCALE = HEAD_DIM ** -0.5 * math.log2(math.e)


def _const_spec(shape):
    zeros = (0,) * len(shape)
    return pl.BlockSpec(tuple(shape), lambda *_: zeros, pipeline_mode=pl.Buffered(1))


def _layer_spec(arr, l):
    tail = (0,) * (arr.ndim - 1)
    return pl.BlockSpec((None,) + tuple(arr.shape[1:]), lambda *_: (l,) + tail,
                        pipeline_mode=pl.Buffered(1))


def _mod_spec(l, k, row):
    def index(b, *_):
        return (l, b if row is None else row, k, 0, 0)
    return lambda d: pl.BlockSpec((None, None, None, 1, d), index)


def _params(*sem, flags=None):
    return pltpu.CompilerParams(dimension_semantics=sem, vmem_limit_bytes=VMEM_LIMIT, flags=flags)


def _rms(x, g):
    return x * lax.rsqrt(jnp.mean(x * x, axis=-1, keepdims=True) + EPS) * g


def _mm(a, b):
    return jnp.dot(a, b, preferred_element_type=F32)


def _modulated(x, g, shift, scale):
    return _rms(x, g) * (1.0 + scale) + shift


def _seam_prev(a):
    sub = lax.broadcasted_iota(jnp.int32, a.shape, 0) % SEG
    return jnp.where(sub == 0, 0.0, pltpu.roll(a, 1, 0))


def _seam_next(a):
    sub = lax.broadcasted_iota(jnp.int32, a.shape, 0) % SEG
    return jnp.where(sub == SEG - 1, 0.0, pltpu.roll(a, a.shape[0] - 1, 0))


def _time_order_spec(tm, width):
    return pl.BlockSpec((None, SEG, tm // SEG, width), lambda b, t: (b, 0, t, 0))


def _slab_scratch(tm, width):
    return pltpu.VMEM((width // LANES, tm, LANES), F32)


def _load_interleaved(x_ref, slab_ref):
    seg, steps, width = x_ref.shape
    for j in range(width // LANES):
        for s in range(seg):
            slab_ref[j, pl.ds(s, steps, stride=seg), :] = x_ref[s, :, j * LANES:(j + 1) * LANES]


def _read_slabs(slab_ref):
    return jnp.concatenate([slab_ref[j] for j in range(slab_ref.shape[0])], axis=1)


def _store_time_order(y, o_ref, slab_ref):
    seg, steps, width = o_ref.shape
    for j in range(width // LANES):
        slab_ref[j] = y[:, j * LANES:(j + 1) * LANES]
    for j in range(width // LANES):
        for s in range(seg):
            o_ref[s, :, j * LANES:(j + 1) * LANES] = slab_ref[j, pl.ds(s, steps, stride=seg), :]


def _halo_specs(tm, halo, width, n_rows):
    per = tm // halo
    n_blocks = n_rows // halo
    prev = pl.BlockSpec((None, halo, width), lambda b, t: (b, (t * per + n_blocks - 1) % n_blocks, 0))
    nxt = pl.BlockSpec((None, halo, width), lambda b, t: (b, ((t + 1) * per) % n_blocks, 0))
    return prev, nxt


def _mod_kernel(c_ref, w_ref, b_ref, o_ref):
    c = c_ref[...]
    a = (c * jax.nn.sigmoid(c)).astype(BF16)
    o_ref[...] = _mm(a, w_ref[...].astype(BF16)) + b_ref[...]


def _mod_call(cc, w_mod, b_mod):
    L, D, D6 = w_mod.shape
    R = cc.shape[0]
    bn = D6 // 4
    return pl.pallas_call(
        _mod_kernel,
        grid=(L, D6 // bn),
        in_specs=[
            pl.BlockSpec((R, D), lambda l, j: (0, 0)),
            pl.BlockSpec((None, D, bn), lambda l, j: (l, 0, j)),
            pl.BlockSpec((None, 1, bn), lambda l, j: (l, 0, j)),
        ],
        out_specs=pl.BlockSpec((None, R, bn), lambda l, j: (l, 0, j)),
        out_shape=jax.ShapeDtypeStruct((L, R, D6), F32),
        compiler_params=_params("arbitrary", "arbitrary"),
        name="adaln_mod",
    )(cc, w_mod, b_mod)


def _head_rms(z, gain, bd):
    ms = _mm((z * z).astype(BF16), bd) * (1.0 / HEAD_DIM)
    return z * lax.rsqrt(ms + EPS) * gain


def _rope(x, c, s):
    rows, width = x.shape
    lane = lax.broadcasted_iota(jnp.int32, (rows, LANES), 1)
    first_half = (lane & (HEAD_DIM // 2)) == 0
    cols = []
    for j in range(width // LANES):
        sl = slice(j * LANES, (j + 1) * LANES)
        xc = x[:, sl]
        partner = jnp.where(first_half,
                            pltpu.roll(xc, LANES - HEAD_DIM // 2, 1),
                            pltpu.roll(xc, HEAD_DIM // 2, 1))
        cols.append(xc * c[:, sl] + partner * s[:, sl])
    return jnp.concatenate(cols, axis=1) if len(cols) > 1 else cols[0]


def _inproj_kernel(*refs, rope, time_order):
    x_ref, sh_ref, sc_ref, g_ref, w_ref, qg_ref, kg_ref, bd_ref = refs[:8]
    refs = refs[8:]
    if rope:
        cos_ref, sin_ref = refs[:2]
        refs = refs[2:]
    h_ref, q_ref, k_ref, v_ref, u_ref, zx_ref, zr_ref = refs[:7]
    if time_order:
        _load_interleaved(x_ref, refs[7])
        x = _read_slabs(refs[7])
    else:
        x = x_ref[...]
    h = _modulated(x, g_ref[...], sh_ref[...], sc_ref[...]).astype(BF16)
    h_ref[...] = h

    zq = _mm(h, w_ref[:, _OFF_Q:_OFF_Q + D_Q])
    qn = _head_rms(zq, qg_ref[...], bd_ref[...])
    if rope:
        qn = _rope(qn, cos_ref[...], sin_ref[...])
    q_ref[...] = (qn * _Q_SCALE).astype(BF16)

    zkv = _mm(h, w_ref[:, _OFF_KV:_OFF_KV + 2 * D_KV])
    kn = _head_rms(zkv[:, :D_KV], kg_ref[...], bd_ref[:D_KV, :D_KV])
    if rope:
        kn = _rope(kn, cos_ref[:, :D_KV], sin_ref[:, :D_KV])
    k_ref[...] = kn
    v_ref[...] = zkv[:, D_KV:]

    zc = _mm(h, w_ref[:, _OFF_CONV:_OFF_CONV + 2 * D_CONV])
    u_ref[...] = zc[:, :D_CONV] * jax.nn.sigmoid(zc[:, D_CONV:])

    zx_ref[...] = _mm(h, w_ref[:, _OFF_ZX:_OFF_ZX + D_RNN])
    zr_ref[...] = _mm(h, w_ref[:, _OFF_ZR:_OFF_ZR + D_RNN])


def _inproj_call(x, mods, l, mod_row, gain, w, qg, kg, bd, cos, sin, *, tm):
    time_order = x.ndim == 4
    B, D = x.shape[0], x.shape[-1]
    T = x.shape[1] * x.shape[2] if time_order else x.shape[1]
    rope = cos is not None
    tok = lambda w_: pl.BlockSpec((None, tm, w_), lambda b, t: (b, t, 0))
    x_spec = _time_order_spec(tm, D) if time_order else tok(D)
    in_specs = [x_spec, _mod_spec(l, 0, mod_row)(D), _mod_spec(l, 1, mod_row)(D),
                _layer_spec(gain, l), _layer_spec(w, l), _layer_spec(qg, l), _layer_spec(kg, l),
                _const_spec(bd.shape)]
    args = [x, mods, mods, gain, w, qg, kg, bd]
    if rope:
        tab = pl.BlockSpec((tm, D_Q), lambda b, t: (t, 0))
        in_specs += [tab, tab]
        args += [cos, sin]
    out_shape = [jax.ShapeDtypeStruct((B, T, D), BF16),
                 jax.ShapeDtypeStruct((B, T, D_Q), BF16),
                 jax.ShapeDtypeStruct((B, T, D_KV), F32),
                 jax.ShapeDtypeStruct((B, T, D_KV), F32),
                 jax.ShapeDtypeStruct((B, T, D_CONV), F32),
                 jax.ShapeDtypeStruct((B, T, D_RNN), F32),
                 jax.ShapeDtypeStruct((B, T, D_RNN), F32)]
    out_specs = [tok(D), tok(D_Q), tok(D_KV), tok(D_KV), tok(D_CONV), tok(D_RNN), tok(D_RNN)]
    return pl.pallas_call(
        functools.partial(_inproj_kernel, rope=rope, time_order=time_order),
        grid=(B, T // tm),
        in_specs=in_specs, out_specs=out_specs, out_shape=out_shape,
        scratch_shapes=[_slab_scratch(tm, D)] if time_order else [],
        compiler_params=_params("arbitrary", "arbitrary"),
        name="inproj_rope" if rope else "inproj",
    )(*args)


_VT_ROWS = HEAD_DIM + 2 * SUBLANES
_ATTN_KEYS = 256
_ATTN_SAFE_BOUND = 60.0


def _attn_kernel(*refs, seg_lens):
    nseg = len(seg_lens)
    q_ref = refs[0]
    kv_refs = refs[1:1 + 2 * nseg]
    o_ref = refs[1 + 2 * nseg]
    kq_ref, vt_ref, kn_ref = refs[2 + 2 * nseg:]
    m_tot = sum(seg_lens)
    g = pl.program_id(1)

    @pl.when(pl.program_id(2) == 0)
    def _():
        off = 0
        k_norm2 = jnp.zeros((1, 1), F32)
        for i, ms in enumerate(seg_lens):
            lane_half = lax.broadcasted_iota(jnp.int32, (ms, LANES), 1) // HEAD_DIM
            k = kv_refs[2 * i][...]
            kd = jnp.where(lane_half == g, k, pltpu.roll(k, HEAD_DIM, 1)).astype(BF16)
            kq_ref[off:off + ms, :] = jnp.concatenate([kd, kd], axis=1)
            own = jnp.where(lane_half == g, k, 0.0)
            k_norm2 = jnp.maximum(k_norm2, jnp.max(jnp.sum(own * own, axis=1, keepdims=True),
                                                   axis=0, keepdims=True))
            v = kv_refs[2 * i + 1][...]
            vd = jnp.where(lane_half == g, v, pltpu.roll(v, HEAD_DIM, 1))
            vt_ref[0:HEAD_DIM, off:off + ms] = vd.T[0:HEAD_DIM, :].astype(BF16)
            off += ms
        vt_ref[HEAD_DIM:, :] = jnp.ones((_VT_ROWS - HEAD_DIM, m_tot), BF16)
        kn_ref[...] = jnp.broadcast_to(k_norm2, kn_ref.shape)

    q = q_ref[...]
    head_lane = lax.broadcasted_iota(jnp.int32, q.shape, 1) // HEAD_DIM
    q_heads = [jnp.where(head_lane == h, q, jnp.zeros_like(q)) for h in range(Q_GROUP)]

    def scores(h):
        return lax.dot_general(kq_ref[...], q_heads[h], (((1,), (1,)), ((), ())),
                               preferred_element_type=F32)

    def finish(outs):
        for c in range(Q_GROUP // 2):
            pair = jnp.concatenate([outs[2 * c], outs[2 * c + 1]], axis=0)
            o_ref[:, c * LANES:(c + 1) * LANES] = pair.T.astype(BF16)

    qf = q.astype(F32)
    row_id = lax.broadcasted_iota(jnp.int32, (2 * SUBLANES, q.shape[1]), 0)
    lane_id = lax.broadcasted_iota(jnp.int32, (2 * SUBLANES, q.shape[1]), 1) // HEAD_DIM
    head_rows = jnp.where(row_id == lane_id, 1.0, 0.0).astype(BF16)
    q_norm2 = lax.dot_general(head_rows, (qf * qf).astype(BF16), (((1,), (1,)), ((), ())),
                              preferred_element_type=F32)
    bound = jnp.sqrt(q_norm2 * kn_ref[0:1, 0:1] * (1.0 + 2.0 ** -7))
    safe = jnp.max(bound) <= _ATTN_SAFE_BOUND

    @pl.when(safe)
    def _():
        outs = []
        st_next = scores(0)
        for h in range(Q_GROUP):
            st = st_next
            if h + 1 < Q_GROUP:
                st_next = scores(h + 1)
            p = jnp.exp2(st - bound[h:h + 1, :]).astype(BF16)
            acc = _mm(vt_ref[...], p)
            outs.append(acc[0:HEAD_DIM, :] * (1.0 / acc[HEAD_DIM:HEAD_DIM + 1, :]))
        finish(outs)

    @pl.when(jnp.logical_not(safe))
    def _():
        keys = min(_ATTN_KEYS, m_tot)
        n_chunks = m_tot // keys
        outs = []
        st_next = scores(0)
        for h in range(Q_GROUP):
            st_all = st_next
            if h + 1 < Q_GROUP:
                st_next = scores(h + 1)
            for c in range(n_chunks):
                rows = slice(c * keys, (c + 1) * keys)
                st = st_all[rows, :]
                vt = vt_ref[:, rows]
                m_chunk = jnp.max(st, axis=0, keepdims=True)
                if c == 0:
                    m_run = m_chunk
                    acc = _mm(vt, jnp.exp2(st - m_run).astype(BF16))
                else:
                    m_new = jnp.maximum(m_run, m_chunk)
                    acc = acc * jnp.exp2(m_run - m_new) + _mm(vt, jnp.exp2(st - m_new).astype(BF16))
                    m_run = m_new
            outs.append(acc[0:HEAD_DIM, :] * (1.0 / acc[HEAD_DIM:HEAD_DIM + 1, :]))
        finish(outs)


def _attn_call(q, segs, *, tq):
    B, T, _ = q.shape
    seg_lens = tuple(k.shape[1] for k, _ in segs)
    m_tot = sum(seg_lens)
    gw = Q_GROUP * HEAD_DIM
    in_specs = [pl.BlockSpec((None, tq, gw), lambda b, g, t: (b, t, g))]
    args = [q]
    for k, v in segs:
        spec = pl.BlockSpec((None, k.shape[1], D_KV), lambda b, g, t: (b, 0, 0))
        in_specs += [spec, spec]
        args += [k, v]
    return pl.pallas_call(
        functools.partial(_attn_kernel, seg_lens=seg_lens),
        grid=(B, N_KV_HEADS, T // tq),
        in_specs=in_specs,
        out_specs=pl.BlockSpec((None, tq, gw), lambda b, g, t: (b, t, g)),
        out_shape=jax.ShapeDtypeStruct((B, T, D_Q), BF16),
        scratch_shapes=[pltpu.VMEM((m_tot, gw), BF16),
                        pltpu.VMEM((_VT_ROWS, m_tot), BF16),
                        pltpu.VMEM((SUBLANES, LANES), F32)],
        compiler_params=_params("arbitrary", "arbitrary", "arbitrary"),
        name="gqa_attention",
    )(*args)


_CONV_HALO = 16 * SEG
_CONV_ROWS = 32


def _conv_fill(up_ref, u_ref, un_ref, w_ref, pad_ref, wb_ref):
    tm = u_ref.shape[0]
    t = pl.program_id(1)
    for j in range(CONV_K):
        wb_ref[j * SEG:(j + 1) * SEG, :] = jnp.broadcast_to(w_ref[j:j + 1, :], (SEG, D_CONV))
    pad_ref[0:_CONV_HALO, :] = up_ref[...]
    pad_ref[_CONV_HALO:_CONV_HALO + tm, :] = u_ref[...]
    pad_ref[_CONV_HALO + tm:, :] = un_ref[...]

    @pl.when(t == 0)
    def _():
        pad_ref[0:_CONV_HALO, :] = _seam_prev(up_ref[...])

    @pl.when(t == pl.num_programs(1) - 1)
    def _():
        pad_ref[_CONV_HALO + tm:, :] = _seam_next(un_ref[...])


def _conv_rows(pad_ref, wb_ref, b_ref, lg_ref, lb_ref, o_ref, row_lo, row_hi):
    first = _CONV_HALO // SEG - CONV_K // 2
    for r in range(row_lo, row_hi, _CONV_ROWS):
        acc = jnp.zeros((_CONV_ROWS, D_CONV), F32)
        for j in range(CONV_K):
            lo = r + (first + j) * SEG
            w_tile = wb_ref[j * SEG:(j + 1) * SEG, :]
            acc = acc + pad_ref[lo:lo + _CONV_ROWS, :] * jnp.concatenate([w_tile] * (_CONV_ROWS // SEG), axis=0)
        acc = acc + b_ref[...]
        mu = jnp.mean(acc, axis=-1, keepdims=True)
        cen = acc - mu
        var = jnp.mean(cen * cen, axis=-1, keepdims=True)
        y = cen * lax.rsqrt(var + EPS) * lg_ref[...] + lb_ref[...]
        o_ref[r:r + _CONV_ROWS, :] = (y * jax.nn.sigmoid(y)).astype(BF16)


_RNN_HALO = 4 * SEG


_RNN_OUT_ROWS = 2 * SUBLANES


def _rnn_kernel(zx_ref, zr_ref, h0_ref, cw_ref, cb_ref, w_ref, b_ref, lam_ref, o_ref, hl_ref,
                xp_ref, a_ref, bb_ref, ac_ref, h_ref, hf_ref, *, chunk):
    T = zx_ref.shape[0]
    n_chunks = T // chunk
    steps = chunk // SEG

    xp_ref[0:_RNN_HALO, :] = _seam_prev(zx_ref[T - _RNN_HALO:T, :])
    xp_ref[_RNN_HALO + T:, :] = _seam_next(zx_ref[0:_RNN_HALO, :])

    def copy(i, carry):
        r = pl.multiple_of(i * chunk, chunk)
        xp_ref[pl.ds(_RNN_HALO + r, chunk), :] = zx_ref[pl.ds(r, chunk), :]
        return carry

    lax.fori_loop(0, n_chunks, copy, 0)

    lam = lam_ref[...]
    log_sig = jnp.minimum(lam, 0.0) - jnp.log1p(jnp.exp(-jnp.abs(lam)))
    half_c = (0.5 * LRU_C * math.log2(math.e)) * log_sig
    row = lax.broadcasted_iota(jnp.int32, (SEG, D_RNN), 0)

    def run(reverse):
        def chunk_body(c, carry):
            cc = (n_chunks - 1 - c) if reverse else c
            r0 = pl.multiple_of(cc * chunk, chunk)
            xc = jnp.zeros((chunk, D_RNN), F32)
            for j in range(RNN_CONV_K):
                step = (RNN_CONV_K - 1 - j) if reverse else (j - RNN_CONV_K + 1)
                xc = xc + xp_ref[pl.ds(r0 + _RNN_HALO + step * SEG, chunk), :] * cw_ref[j:j + 1, :]
            xc = xc + cb_ref[...]
            gates = _mm(xc.astype(BF16), w_ref[...]) + b_ref[...]
            tanh_r = jnp.tanh(gates[:, :D_RNN])
            i_gate = 0.5 + 0.5 * jnp.tanh(gates[:, D_RNN:])
            a = jnp.exp2(half_c + half_c * tanh_r)
            a_ref[...] = a
            bb_ref[...] = jnp.sqrt((1.0 - a) * (1.0 + a)) * (i_gate * xc)

            def step_body(k, state):
                a_cum, h_zero = state
                kk = (steps - 1 - k) if reverse else k
                t0 = pl.multiple_of(kk * SEG, SEG)
                at = a_ref[pl.ds(t0, SEG), :]
                h_zero = at * h_zero + bb_ref[pl.ds(t0, SEG), :]
                a_cum = at * a_cum
                h_ref[pl.ds(r0 + t0, SEG), :] = h_zero
                ac_ref[pl.ds(r0 + t0, SEG), :] = a_cum
                return a_cum, h_zero

            return lax.fori_loop(0, steps, step_body, carry, unroll=8)

        a_end, h_end = lax.fori_loop(0, n_chunks, chunk_body,
                                     (jnp.ones((SEG, D_RNN), F32), jnp.zeros((SEG, D_RNN), F32)))

        start = SEG - 1 if reverse else 0
        state_in = jnp.where(row == start, h0_ref[...], 0.0)
        for i in range(1, SEG):
            s = start - i if reverse else start + i
            leaving = h_end + a_end * state_in
            state_in = jnp.where(row == s, pltpu.roll(leaving, SEG - 1 if reverse else 1, 0), state_in)

        state_rows = jnp.concatenate([state_in] * (_RNN_OUT_ROWS // SEG), axis=0)

        def fix(k, carry):
            t0 = pl.multiple_of(k * _RNN_OUT_ROWS, _RNN_OUT_ROWS)
            rows = pl.ds(t0, _RNN_OUT_ROWS)
            h = h_ref[rows, :] + ac_ref[rows, :] * state_rows
            if reverse:
                o_ref[rows, :] = ((hf_ref[rows, :] + h) * jax.nn.gelu(zr_ref[rows, :])).astype(BF16)
            else:
                hf_ref[rows, :] = h
            return carry

        lax.fori_loop(0, T // _RNN_OUT_ROWS, fix, 0, unroll=4)
        edge = slice(0, SEG) if reverse else slice(T - SEG, T)
        last_rows = h_ref[edge, :] + ac_ref[edge, :] * state_in
        final = last_rows[0:1, :] if reverse else last_rows[SEG - 1:SEG, :]
        hl_ref[...] = jnp.broadcast_to(final, (SEG, D_RNN))

    d = pl.program_id(1)
    pl.when(d == 0)(lambda: run(False))
    pl.when(d == 1)(lambda: run(True))


def _rnn_call(zx, zr, h0, l, cw, cb, w, b, lam, *, chunk):
    B, T, C = zx.shape
    per_dir = lambda a: pl.BlockSpec((None, None) + tuple(a.shape[2:]), lambda b_, d: (l, d, 0, 0))
    state = pl.BlockSpec((None, None, SEG, C), lambda b_, d: (b_, d, 0, 0))
    seq = pl.BlockSpec((None, T, C), lambda b_, d: (b_, 0, 0))
    return pl.pallas_call(
        functools.partial(_rnn_kernel, chunk=chunk),
        grid=(B, 2),
        in_specs=[seq, seq, state, per_dir(cw), per_dir(cb), per_dir(w), per_dir(b), per_dir(lam)],
        out_specs=[seq, state],
        out_shape=[jax.ShapeDtypeStruct((B, T, C), BF16),
                   jax.ShapeDtypeStruct((B, 2, SEG, C), F32)],
        scratch_shapes=[pltpu.VMEM((T + 2 * _RNN_HALO, C), F32),
                        pltpu.VMEM((chunk, C), F32), pltpu.VMEM((chunk, C), F32),
                        pltpu.VMEM((T, C), F32), pltpu.VMEM((T, C), F32), pltpu.VMEM((T, C), F32)],
        compiler_params=_params("arbitrary", "arbitrary"),
        name="rglru",
    )(zx, zr, h0, cw, cb, w, b, lam)


_MERGE_COLS = 256


def _merge_kernel(x_ref, h_ref, gt_ref, gpost_ref, wg_ref, at_ref,
                  up_ref, u_ref, un_ref, cw_ref, cb_ref, lg_ref, lb_ref,
                  rn_ref, wao_ref, wco_ref, wro_ref, wout_ref, o_ref, pad_ref, wb_ref, cv_ref, m_ref, cp_ref,
                  mb_ref, *slab):
    D = x_ref.shape[-1]
    _conv_fill(up_ref, u_ref, un_ref, cw_ref, pad_ref, wb_ref)
    tm = u_ref.shape[0]
    conv = functools.partial(_conv_rows, pad_ref, wb_ref, cb_ref, lg_ref, lb_ref, cv_ref)

    def gate(branch, cols):
        first = _OFF_GATE + branch * D
        return jax.nn.sigmoid(_mm(h_ref[...], wg_ref[:, first + cols.start:first + cols.stop]))

    quarter = tm // 4
    half = D // 2
    groups = ((at_ref, wao_ref, 0, 0), (at_ref, wao_ref, 0, half), (rn_ref, wro_ref, 2, 0),
              (rn_ref, wro_ref, 2, half))
    for i, (act_ref, w_ref, branch, col0) in enumerate(groups):
        for c in range(col0, col0 + half, _MERGE_COLS):
            cols = slice(c, c + _MERGE_COLS)
            term = gate(branch, cols) * _mm(act_ref[...], w_ref[:, cols])
            m_ref[:, cols] = term if branch == 0 else m_ref[:, cols] + term
        rows = slice(i * quarter, (i + 1) * quarter)
        conv(rows.start, rows.stop)
        cp_ref[rows, :] = _mm(cv_ref[rows, :], wco_ref[...])
    for c in range(0, D, _MERGE_COLS):
        cols = slice(c, c + _MERGE_COLS)
        mb_ref[:, cols] = (m_ref[:, cols] + gate(1, cols) * cp_ref[:, cols]).astype(BF16)
    o = _mm(mb_ref[...], wout_ref[...])
    if slab:
        _load_interleaved(x_ref, slab[0])
        x = _read_slabs(slab[0])
    else:
        x = x_ref[...]
    o_ref[...] = x + gt_ref[...] * _rms(o, gpost_ref[...])


def _merge_call(x, h, mods, l, mod_row, gpost, wg, attn, u, cw, cb, lg, lb, rnn, wao, wco, wro, wout, *, tm):
    time_order = x.ndim == 4
    B, D = x.shape[0], x.shape[-1]
    T = attn.shape[1]
    tok = lambda w_: pl.BlockSpec((None, tm, w_), lambda b, t: (b, t, 0))
    u_prev, u_next = _halo_specs(tm, _CONV_HALO, D_CONV, T)
    return pl.pallas_call(
        _merge_kernel,
        grid=(B, T // tm),
        in_specs=[_time_order_spec(tm, D) if time_order else tok(D), tok(D),
                  _mod_spec(l, 2, mod_row)(D), _layer_spec(gpost, l),
                  _layer_spec(wg, l), tok(D_Q),
                  u_prev, tok(D_CONV), u_next, _layer_spec(cw, l), _layer_spec(cb, l),
                  _layer_spec(lg, l), _layer_spec(lb, l), tok(D_RNN),
                  _layer_spec(wao, l), _layer_spec(wco, l), _layer_spec(wro, l), _layer_spec(wout, l)],
        out_specs=tok(D),
        out_shape=jax.ShapeDtypeStruct((B, T, D), F32),
        scratch_shapes=[pltpu.VMEM((tm + 2 * _CONV_HALO, D_CONV), F32),
                        pltpu.VMEM((CONV_K * SEG, D_CONV), F32), pltpu.VMEM((tm, D_CONV), BF16),
                        pltpu.VMEM((tm, D), F32), pltpu.VMEM((tm, D), F32), pltpu.VMEM((tm, D), BF16)]
        + ([_slab_scratch(tm, D)] if time_order else []),
        compiler_params=_params("arbitrary", "arbitrary"),
        name="merge",
    )(x, h, mods, gpost, wg, attn, u, u, u, cw, cb, lg, lb, rnn, wao, wco, wro, wout)


_FFN_CHUNK = 3072


def _ffn_kernel(xp_ref, x_ref, xn_ref, sh_ref, sc_ref, gt_ref, gpre_ref, gpost_ref, up_ref,
                dw_ref, db_ref, down_ref, o_ref, hp_ref, hn_ref, *slab):
    tm = x_ref.shape[0]
    t = pl.program_id(1)
    mod = lambda v: _modulated(v, gpre_ref[...], sh_ref[...], sc_ref[...])
    hp_ref[...] = mod(xp_ref[...])
    hn_ref[...] = mod(xn_ref[...])

    @pl.when(t == 0)
    def _():
        hp_ref[...] = _seam_prev(hp_ref[...])

    @pl.when(t == pl.num_programs(1) - 1)
    def _():
        hn_ref[...] = _seam_next(hn_ref[...])

    h = jnp.concatenate([hp_ref[...], mod(x_ref[...]), hn_ref[...]], axis=0).astype(BF16)

    acc = jnp.zeros(x_ref.shape, F32)
    for c in range(D_FF // _FFN_CHUNK):
        cg = slice(c * _FFN_CHUNK, (c + 1) * _FFN_CHUNK)
        cv = slice(D_FF + c * _FFN_CHUNK, D_FF + (c + 1) * _FFN_CHUNK)
        ug = _mm(h, up_ref[:, cg])
        uv = _mm(h, up_ref[:, cv])
        yg = db_ref[:, cg]
        yv = db_ref[:, cv]
        for j in range(3):
            rows = slice(j * SEG, j * SEG + tm)
            yg = yg + ug[rows, :] * dw_ref[j:j + 1, cg]
            yv = yv + uv[rows, :] * dw_ref[j:j + 1, cv]
        act = (jax.nn.gelu(yg) * yv).astype(BF16)
        acc = acc + _mm(act, down_ref[cg, :])
    y = x_ref[...] + gt_ref[...] * _rms(acc, gpost_ref[...])
    if slab:
        _store_time_order(y, o_ref, slab[0])
    else:
        o_ref[...] = y


def _ffn_call(x, mods, l, mod_row, gpre, gpost, up, dw, db, down, *, tm, time_order_out=False):
    B, T, D = x.shape
    prev, nxt = _halo_specs(tm, SEG, D, T)
    if time_order_out:
        out_spec = _time_order_spec(tm, D)
        out_shape = jax.ShapeDtypeStruct((B, SEG, T // SEG, D), F32)
    else:
        out_spec = pl.BlockSpec((None, tm, D), lambda b, t: (b, t, 0))
        out_shape = jax.ShapeDtypeStruct((B, T, D), F32)
    return pl.pallas_call(
        _ffn_kernel,
        grid=(B, T // tm),
        in_specs=[prev, pl.BlockSpec((None, tm, D), lambda b, t: (b, t, 0)), nxt,
                  _mod_spec(l, 3, mod_row)(D), _mod_spec(l, 4, mod_row)(D), _mod_spec(l, 5, mod_row)(D),
                  _layer_spec(gpre, l), _layer_spec(gpost, l), _layer_spec(up, l),
                  _layer_spec(dw, l), _layer_spec(db, l), _layer_spec(down, l)],
        out_specs=out_spec,
        out_shape=out_shape,
        scratch_shapes=[pltpu.VMEM((SEG, D), F32), pltpu.VMEM((SEG, D), F32)]
        + ([_slab_scratch(tm, D)] if time_order_out else []),
        compiler_params=_params("arbitrary", "arbitrary"),
        name="conv_ffn",
    )(x, x, x, mods, mods, mods, gpre, gpost, up, dw, db, down)


def _seg_interleave(a):
    *lead, T, W = a.shape
    return a.reshape(*lead, SEG, T // SEG, W).swapaxes(-3, -2).reshape(*lead, T, W)


def _rope_tables(n):
    rows = n // GRID_W
    row = jnp.repeat(jnp.arange(rows), GRID_W).astype(F32)
    col = jnp.tile(jnp.arange(GRID_W), rows).astype(F32)
    n_freq = HEAD_DIM // 4
    freq = ROPE_THETA ** (-jnp.arange(n_freq, dtype=F32) / n_freq)
    ang = jnp.concatenate([row[:, None] * freq, col[:, None] * freq], axis=-1)
    cos, sin = jnp.cos(ang), jnp.sin(ang)
    cos_t = jnp.tile(jnp.concatenate([cos, cos], axis=-1), (1, N_Q_HEADS))
    sin_t = jnp.tile(jnp.concatenate([-sin, sin], axis=-1), (1, N_Q_HEADS))
    return cos_t, sin_t


def _block_diag(w):
    *lead, nb, bi, bj = w.shape
    eye = jnp.eye(nb, dtype=w.dtype)
    return jnp.einsum('...hij,hg->...higj', w, eye).reshape(*lead, nb * bi, nb * bj)


def kernel(x, c, ctx, c_ctx, w_mod, b_mod, norm_pre_mix, norm_post_mix, norm_pre_ffn, norm_post_ffn,
           w_in, q_norm, k_norm, w_attn_out, conv_dw, conv_dw_b, conv_ln_g, conv_ln_b, w_conv_out,
           rnn_conv_w, rnn_conv_b, rnn_wa, rnn_ba, rnn_wx, rnn_bx, rnn_lambda, w_rnn_out, w_out,
           ffn_up, ffn_dw, ffn_dw_b, ffn_down):
    B, N, D = x.shape
    M = ctx.shape[1]
    L = w_mod.shape[0]
    tm_lat = min(N, 512)
    tq_lat = min(N, 1024)
    tm_ctx = min(M, 256)
    chunk_lat = min(N, 512)
    chunk_ctx = min(M, 256)

    mod_rows = 2 * SUBLANES
    cc = jnp.concatenate([c, c_ctx[None, :], jnp.zeros((mod_rows - B - 1, D), F32)], axis=0)
    mods = _mod_call(cc, w_mod, b_mod[:, None, :]).reshape(L, mod_rows, N_MOD, 1, D)

    cos_t, sin_t = (_seg_interleave(t) for t in _rope_tables(N))
    head_id = jnp.arange(D_Q) // HEAD_DIM
    bd = (head_id[:, None] == head_id[None, :]).astype(BF16)
    h0 = jnp.zeros((B, 2, SEG, D_RNN), F32)

    vec = lambda a: a[:, None, :]
    w_proj = w_gate = w_in.astype(BF16)
    qg = vec(jnp.tile(q_norm, (1, N_Q_HEADS)))
    kg = vec(jnp.tile(k_norm, (1, N_KV_HEADS)))
    rnn_w = (0.5 * jnp.concatenate([_block_diag(rnn_wa), _block_diag(rnn_wx)], axis=-1)).astype(BF16)
    rnn_b = 0.5 * jnp.concatenate([rnn_ba, rnn_bx], axis=-1)[:, :, None, :]
    rnn_cb = rnn_conv_b[:, :, None, :]
    lam = rnn_lambda[:, :, None, :]
    wao, wco, wro, wout = (w_attn_out.astype(BF16), w_conv_out.astype(BF16), w_rnn_out.astype(BF16),
                           w_out.astype(BF16))
    up, down = ffn_up.astype(BF16), ffn_down.astype(BF16)
    gpre_mix, gpost_mix, gpre_ffn, gpost_ffn = (vec(norm_pre_mix), vec(norm_post_mix), vec(norm_pre_ffn),
                                                vec(norm_post_ffn))
    cdb, clg, clb, fdb = vec(conv_dw_b), vec(conv_ln_g), vec(conv_ln_b), vec(ffn_dw_b)

    x = x.reshape(B, SEG, N // SEG, D)
    ctx = ctx.reshape(B, SEG, M // SEG, D)
    for l in range(L):
        need_ctx = l < L - 1
        hcx, qc, kc, vc, uc, zxc, zrc = _inproj_call(ctx, mods, l, B, gpre_mix, w_proj, qg, kg, bd, None, None,
                                                     tm=tm_ctx)
        rnn_c, hc_last = _rnn_call(zxc, zrc, h0, l, rnn_conv_w, rnn_cb, rnn_w, rnn_b, lam, chunk=chunk_ctx)

        hx, q, k, v, u, zx, zr = _inproj_call(x, mods, l, None, gpre_mix, w_proj, qg, kg, bd, cos_t, sin_t,
                                              tm=tm_lat)
        attn = _attn_call(q, [(kc, vc), (k, v)], tq=tq_lat)
        rnn, _ = _rnn_call(zx, zr, hc_last, l, rnn_conv_w, rnn_cb, rnn_w, rnn_b, lam, chunk=chunk_lat)
        x = _merge_call(x, hx, mods, l, None, gpost_mix, w_gate, attn, u, conv_dw, cdb, clg, clb,
                        rnn, wao, wco, wro, wout, tm=tm_lat)
        x = _ffn_call(x, mods, l, None, gpre_ffn, gpost_ffn, up, ffn_dw, fdb, down, tm=tm_lat,
                      time_order_out=not need_ctx)

        if need_ctx:
            attn_c = _attn_call(qc, [(kc, vc)], tq=tm_ctx)
            ctx = _merge_call(ctx, hcx, mods, l, B, gpost_mix, w_gate, attn_c, uc, conv_dw, cdb, clg, clb,
                              rnn_c, wao, wco, wro, wout, tm=tm_ctx)
            ctx = _ffn_call(ctx, mods, l, B, gpre_ffn, gpost_ffn, up, ffn_dw, fdb, down, tm=tm_ctx)
    return x.reshape(B, N, D)
```

```python
import functools
import math

import jax
import jax.numpy as jnp
from jax import lax
from jax.experimental import pallas as pl
from jax.experimental.pallas import tpu as pltpu

F32 = jnp.float32
BF16 = jnp.bfloat16

EPS = 1e-6
HEAD_DIM = 64
N_Q_HEADS = 8
N_KV_HEADS = 2
Q_GROUP = N_Q_HEADS // N_KV_HEADS
D_Q = N_Q_HEADS * HEAD_DIM
D_KV = N_KV_HEADS * HEAD_DIM
D_CONV = 512
CONV_K = 31
D_RNN = 512
RNN_CONV_K = 4
LRU_C = 8.0
D_FF = 3072
GRID_W = 64
ROPE_THETA = 10000.0
N_MOD = 6

LANES = 128
SUBLANES = 8
SEG = SUBLANES
VMEM_LIMIT = 56 * 1024 * 1024

_OFF_Q = 0
_OFF_KV = D_Q
_OFF_CONV = _OFF_KV + 2 * D_KV
_OFF_ZX = _OFF_CONV + 2 * D_CONV
_OFF_ZR = _OFF_ZX + D_RNN
_OFF_GATE = _OFF_ZR + D_RNN

_Q_SCALE = HEAD_DIM ** -0.5 * math.log2(math.e)


def _const_spec(shape):
    zeros = (0,) * len(shape)
    return pl.BlockSpec(tuple(shape), lambda *_: zeros, pipeline_mode=pl.Buffered(1))


def _layer_spec(arr, l):
    tail = (0,) * (arr.ndim - 1)
    return pl.BlockSpec((None,) + tuple(arr.shape[1:]), lambda *_: (l,) + tail,
                        pipeline_mode=pl.Buffered(1))


def _mod_spec(l, k, row):
    def index(b, *_):
        return (l, b if row is None else row, k, 0, 0)
    return lambda d: pl.BlockSpec((None, None, None, 1, d), index)


def _params(*sem):
    return pltpu.CompilerParams(dimension_semantics=sem, vmem_limit_bytes=VMEM_LIMIT)


def _rms(x, g):
    return x * lax.rsqrt(jnp.mean(x * x, axis=-1, keepdims=True) + EPS) * g


def _mm(a, b):
    return jnp.dot(a, b, preferred_element_type=F32)


def _modulated(x, g, shift, scale):
    return _rms(x, g) * (1.0 + scale) + shift


def _seam_prev(a):
    sub = lax.broadcasted_iota(jnp.int32, a.shape, 0) % SEG
    return jnp.where(sub == 0, 0.0, pltpu.roll(a, 1, 0))


def _seam_next(a):
    sub = lax.broadcasted_iota(jnp.int32, a.shape, 0) % SEG
    return jnp.where(sub == SEG - 1, 0.0, pltpu.roll(a, a.shape[0] - 1, 0))


def _time_order_spec(tm, width):
    return pl.BlockSpec((None, SEG, tm // SEG, width), lambda b, t: (b, 0, t, 0))


def _slab_scratch(tm, width):
    return pltpu.VMEM((width // LANES, tm, LANES), F32)


def _load_interleaved(x_ref, slab_ref):
    seg, steps, width = x_ref.shape
    for j in range(width // LANES):
        for s in range(seg):
            slab_ref[j, pl.ds(s, steps, stride=seg), :] = x_ref[s, :, j * LANES:(j + 1) * LANES]


def _read_slabs(slab_ref):
    return jnp.concatenate([slab_ref[j] for j in range(slab_ref.shape[0])], axis=1)


def _store_time_order(y, o_ref, slab_ref):
    seg, steps, width = o_ref.shape
    for j in range(width // LANES):
        slab_ref[j] = y[:, j * LANES:(j + 1) * LANES]
    for j in range(width // LANES):
        for s in range(seg):
            o_ref[s, :, j * LANES:(j + 1) * LANES] = slab_ref[j, pl.ds(s, steps, stride=seg), :]


def _halo_specs(tm, halo, width, n_rows):
    per = tm // halo
    n_blocks = n_rows // halo
    prev = pl.BlockSpec((None, halo, width), lambda b, t: (b, (t * per + n_blocks - 1) % n_blocks, 0))
    nxt = pl.BlockSpec((None, halo, width), lambda b, t: (b, ((t + 1) * per) % n_blocks, 0))
    return prev, nxt


def _mod_kernel(c_ref, w_ref, b_ref, o_ref):
    c = c_ref[...]
    a = (c * jax.nn.sigmoid(c)).astype(BF16)
    o_ref[...] = _mm(a, w_ref[...].astype(BF16)) + b_ref[...]


def _mod_call(cc, w_mod, b_mod):
    L, D, D6 = w_mod.shape
    R = cc.shape[0]
    bn = D6 // 4
    return pl.pallas_call(
        _mod_kernel,
        grid=(L, D6 // bn),
        in_specs=[
            pl.BlockSpec((R, D), lambda l, j: (0, 0)),
            pl.BlockSpec((None, D, bn), lambda l, j: (l, 0, j)),
            pl.BlockSpec((None, 1, bn), lambda l, j: (l, 0, j)),
        ],
        out_specs=pl.BlockSpec((None, R, bn), lambda l, j: (l, 0, j)),
        out_shape=jax.ShapeDtypeStruct((L, R, D6), F32),
        compiler_params=_params("arbitrary", "arbitrary"),
        name="adaln_mod",
    )(cc, w_mod, b_mod)


def _head_rms(z, gain, bd):
    ms = _mm((z * z).astype(BF16), bd) * (1.0 / HEAD_DIM)
    return z * lax.rsqrt(ms + EPS) * gain


def _rope(x, c, s):
    rows, width = x.shape
    lane = lax.broadcasted_iota(jnp.int32, (rows, LANES), 1)
    first_half = (lane & (HEAD_DIM // 2)) == 0
    cols = []
    for j in range(width // LANES):
        sl = slice(j * LANES, (j + 1) * LANES)
        xc = x[:, sl]
        partner = jnp.where(first_half,
                            pltpu.roll(xc, LANES - HEAD_DIM // 2, 1),
                            pltpu.roll(xc, HEAD_DIM // 2, 1))
        cols.append(xc * c[:, sl] + partner * s[:, sl])
    return jnp.concatenate(cols, axis=1) if len(cols) > 1 else cols[0]


def _inproj_kernel(*refs, rope, time_order):
    x_ref, sh_ref, sc_ref, g_ref, w_ref, qg_ref, kg_ref, bd_ref = refs[:8]
    refs = refs[8:]
    if rope:
        cos_ref, sin_ref = refs[:2]
        refs = refs[2:]
    h_ref, q_ref, k_ref, v_ref, u_ref, zx_ref, zr_ref = refs[:7]
    if time_order:
        _load_interleaved(x_ref, refs[7])
        x = _read_slabs(refs[7])
    else:
        x = x_ref[...]
    h = _modulated(x, g_ref[...], sh_ref[...], sc_ref[...]).astype(BF16)
    h_ref[...] = h

    zq = _mm(h, w_ref[:, _OFF_Q:_OFF_Q + D_Q])
    qn = _head_rms(zq, qg_ref[...], bd_ref[...])
    if rope:
        qn = _rope(qn, cos_ref[...], sin_ref[...])
    q_ref[...] = (qn * _Q_SCALE).astype(BF16)

    zkv = _mm(h, w_ref[:, _OFF_KV:_OFF_KV + 2 * D_KV])
    kn = _head_rms(zkv[:, :D_KV], kg_ref[...], bd_ref[:D_KV, :D_KV])
    if rope:
        kn = _rope(kn, cos_ref[:, :D_KV], sin_ref[:, :D_KV])
    k_ref[...] = kn
    v_ref[...] = zkv[:, D_KV:]

    zc = _mm(h, w_ref[:, _OFF_CONV:_OFF_CONV + 2 * D_CONV])
    u_ref[...] = zc[:, :D_CONV] * jax.nn.sigmoid(zc[:, D_CONV:])

    zx_ref[...] = _mm(h, w_ref[:, _OFF_ZX:_OFF_ZX + D_RNN])
    zr_ref[...] = _mm(h, w_ref[:, _OFF_ZR:_OFF_ZR + D_RNN])


def _inproj_call(x, mods, l, mod_row, gain, w, qg, kg, bd, cos, sin, *, tm):
    time_order = x.ndim == 4
    B, D = x.shape[0], x.shape[-1]
    T = x.shape[1] * x.shape[2] if time_order else x.shape[1]
    rope = cos is not None
    tok = lambda w_: pl.BlockSpec((None, tm, w_), lambda b, t: (b, t, 0))
    x_spec = _time_order_spec(tm, D) if time_order else tok(D)
    in_specs = [x_spec, _mod_spec(l, 0, mod_row)(D), _mod_spec(l, 1, mod_row)(D),
                _layer_spec(gain, l), _layer_spec(w, l), _layer_spec(qg, l), _layer_spec(kg, l),
                _const_spec(bd.shape)]
    args = [x, mods, mods, gain, w, qg, kg, bd]
    if rope:
        tab = pl.BlockSpec((tm, D_Q), lambda b, t: (t, 0))
        in_specs += [tab, tab]
        args += [cos, sin]
    out_shape = [jax.ShapeDtypeStruct((B, T, D), BF16),
                 jax.ShapeDtypeStruct((B, T, D_Q), BF16),
                 jax.ShapeDtypeStruct((B, T, D_KV), F32),
                 jax.ShapeDtypeStruct((B, T, D_KV), F32),
                 jax.ShapeDtypeStruct((B, T, D_CONV), F32),
                 jax.ShapeDtypeStruct((B, T, D_RNN), F32),
                 jax.ShapeDtypeStruct((B, T, D_RNN), F32)]
    out_specs = [tok(D), tok(D_Q), tok(D_KV), tok(D_KV), tok(D_CONV), tok(D_RNN), tok(D_RNN)]
    return pl.pallas_call(
        functools.partial(_inproj_kernel, rope=rope, time_order=time_order),
        grid=(B, T // tm),
        in_specs=in_specs, out_specs=out_specs, out_shape=out_shape,
        scratch_shapes=[_slab_scratch(tm, D)] if time_order else [],
        compiler_params=_params("arbitrary", "arbitrary"),
        name="inproj_rope" if rope else "inproj",
    )(*args)


_VT_ROWS = HEAD_DIM + 2 * SUBLANES
_ATTN_KEYS = 256
_ATTN_SAFE_BOUND = 60.0


def _attn_kernel(*refs, seg_lens):
    nseg = len(seg_lens)
    q_ref = refs[0]
    kv_refs = refs[1:1 + 2 * nseg]
    o_ref = refs[1 + 2 * nseg]
    kq_ref, vt_ref, kn_ref = refs[2 + 2 * nseg:]
    m_tot = sum(seg_lens)
    g = pl.program_id(1)

    @pl.when(pl.program_id(2) == 0)
    def _():
        off = 0
        k_norm2 = jnp.zeros((1, 1), F32)
        for i, ms in enumerate(seg_lens):
            lane_half = lax.broadcasted_iota(jnp.int32, (ms, LANES), 1) // HEAD_DIM
            k = kv_refs[2 * i][...]
            kd = jnp.where(lane_half == g, k, pltpu.roll(k, HEAD_DIM, 1)).astype(BF16)
            kq_ref[off:off + ms, :] = jnp.concatenate([kd, kd], axis=1)
            own = jnp.where(lane_half == g, k, 0.0)
            k_norm2 = jnp.maximum(k_norm2, jnp.max(jnp.sum(own * own, axis=1, keepdims=True),
                                                   axis=0, keepdims=True))
            v = kv_refs[2 * i + 1][...]
            vd = jnp.where(lane_half == g, v, pltpu.roll(v, HEAD_DIM, 1))
            vt_ref[0:HEAD_DIM, off:off + ms] = vd.T[0:HEAD_DIM, :].astype(BF16)
            off += ms
        vt_ref[HEAD_DIM:, :] = jnp.ones((_VT_ROWS - HEAD_DIM, m_tot), BF16)
        kn_ref[...] = jnp.broadcast_to(k_norm2, kn_ref.shape)

    q = q_ref[...]
    head_lane = lax.broadcasted_iota(jnp.int32, q.shape, 1) // HEAD_DIM
    q_heads = [jnp.where(head_lane == h, q, jnp.zeros_like(q)) for h in range(Q_GROUP)]

    def scores(h):
        return lax.dot_general(kq_ref[...], q_heads[h], (((1,), (1,)), ((), ())),
                               preferred_element_type=F32)

    def finish(outs):
        for c in range(Q_GROUP // 2):
            pair = jnp.concatenate([outs[2 * c], outs[2 * c + 1]], axis=0)
            o_ref[:, c * LANES:(c + 1) * LANES] = pair.T.astype(BF16)

    qf = q.astype(F32)
    row_id = lax.broadcasted_iota(jnp.int32, (2 * SUBLANES, q.shape[1]), 0)
    lane_id = lax.broadcasted_iota(jnp.int32, (2 * SUBLANES, q.shape[1]), 1) // HEAD_DIM
    head_rows = jnp.where(row_id == lane_id, 1.0, 0.0).astype(BF16)
    q_norm2 = lax.dot_general(head_rows, (qf * qf).astype(BF16), (((1,), (1,)), ((), ())),
                              preferred_element_type=F32)
    bound = jnp.sqrt(q_norm2 * kn_ref[0:1, 0:1] * (1.0 + 2.0 ** -7))
    safe = jnp.max(bound) <= _ATTN_SAFE_BOUND

    @pl.when(safe)
    def _():
        outs = []
        st_next = scores(0)
        for h in range(Q_GROUP):
            st = st_next
            if h + 1 < Q_GROUP:
                st_next = scores(h + 1)
            p = jnp.exp2(st - bound[h:h + 1, :]).astype(BF16)
            acc = _mm(vt_ref[...], p)
            outs.append(acc[0:HEAD_DIM, :] * (1.0 / acc[HEAD_DIM:HEAD_DIM + 1, :]))
        finish(outs)

    @pl.when(jnp.logical_not(safe))
    def _():
        keys = min(_ATTN_KEYS, m_tot)
        n_chunks = m_tot // keys
        outs = []
        st_next = scores(0)
        for h in range(Q_GROUP):
            st_all = st_next
            if h + 1 < Q_GROUP:
                st_next = scores(h + 1)
            for c in range(n_chunks):
                rows = slice(c * keys, (c + 1) * keys)
                st = st_all[rows, :]
                vt = vt_ref[:, rows]
                m_chunk = jnp.max(st, axis=0, keepdims=True)
                if c == 0:
                    m_run = m_chunk
                    acc = _mm(vt, jnp.exp2(st - m_run).astype(BF16))
                else:
                    m_new = jnp.maximum(m_run, m_chunk)
                    acc = acc * jnp.exp2(m_run - m_new) + _mm(vt, jnp.exp2(st - m_new).astype(BF16))
                    m_run = m_new
            outs.append(acc[0:HEAD_DIM, :] * (1.0 / acc[HEAD_DIM:HEAD_DIM + 1, :]))
        finish(outs)


def _attn_call(q, segs, *, tq):
    B, T, _ = q.shape
    seg_lens = tuple(k.shape[1] for k, _ in segs)
    m_tot = sum(seg_lens)
    gw = Q_GROUP * HEAD_DIM
    in_specs = [pl.BlockSpec((None, tq, gw), lambda b, g, t: (b, t, g))]
    args = [q]
    for k, v in segs:
        spec = pl.BlockSpec((None, k.shape[1], D_KV), lambda b, g, t: (b, 0, 0))
        in_specs += [spec, spec]
        args += [k, v]
    return pl.pallas_call(
        functools.partial(_attn_kernel, seg_lens=seg_lens),
        grid=(B, N_KV_HEADS, T // tq),
        in_specs=in_specs,
        out_specs=pl.BlockSpec((None, tq, gw), lambda b, g, t: (b, t, g)),
        out_shape=jax.ShapeDtypeStruct((B, T, D_Q), BF16),
        scratch_shapes=[pltpu.VMEM((m_tot, gw), BF16),
                        pltpu.VMEM((_VT_ROWS, m_tot), BF16),
                        pltpu.VMEM((SUBLANES, LANES), F32)],
        compiler_params=_params("arbitrary", "arbitrary", "arbitrary"),
        name="gqa_attention",
    )(*args)


_CONV_HALO = 16 * SEG
_CONV_ROWS = 32


def _conv_fill(up_ref, u_ref, un_ref, w_ref, pad_ref, wb_ref):
    tm = u_ref.shape[0]
    t = pl.program_id(1)
    for j in range(CONV_K):
        wb_ref[j * SEG:(j + 1) * SEG, :] = jnp.broadcast_to(w_ref[j:j + 1, :], (SEG, D_CONV))
    pad_ref[0:_CONV_HALO, :] = up_ref[...]
    pad_ref[_CONV_HALO:_CONV_HALO + tm, :] = u_ref[...]
    pad_ref[_CONV_HALO + tm:, :] = un_ref[...]

    @pl.when(t == 0)
    def _():
        pad_ref[0:_CONV_HALO, :] = _seam_prev(up_ref[...])

    @pl.when(t == pl.num_programs(1) - 1)
    def _():
        pad_ref[_CONV_HALO + tm:, :] = _seam_next(un_ref[...])


def _conv_rows(pad_ref, wb_ref, b_ref, lg_ref, lb_ref, o_ref, row_lo, row_hi):
    first = _CONV_HALO // SEG - CONV_K // 2
    for r in range(row_lo, row_hi, _CONV_ROWS):
        acc = jnp.zeros((_CONV_ROWS, D_CONV), F32)
        for j in range(CONV_K):
            lo = r + (first + j) * SEG
            w_tile = wb_ref[j * SEG:(j + 1) * SEG, :]
            acc = acc + pad_ref[lo:lo + _CONV_ROWS, :] * jnp.concatenate([w_tile] * (_CONV_ROWS // SEG), axis=0)
        acc = acc + b_ref[...]
        mu = jnp.mean(acc, axis=-1, keepdims=True)
        cen = acc - mu
        var = jnp.mean(cen * cen, axis=-1, keepdims=True)
        y = cen * lax.rsqrt(var + EPS) * lg_ref[...] + lb_ref[...]
        o_ref[r:r + _CONV_ROWS, :] = (y * jax.nn.sigmoid(y)).astype(BF16)


_RNN_HALO = 4 * SEG


_RNN_OUT_ROWS = 2 * SUBLANES


def _rnn_kernel(zx_ref, zr_ref, h0_ref, cw_ref, cb_ref, w_ref, b_ref, lam_ref, o_ref, hl_ref,
                xp_ref, a_ref, bb_ref, ac_ref, h_ref, hf_ref, *, chunk):
    T = zx_ref.shape[0]
    n_chunks = T // chunk
    steps = chunk // SEG

    xp_ref[0:_RNN_HALO, :] = _seam_prev(zx_ref[T - _RNN_HALO:T, :])
    xp_ref[_RNN_HALO + T:, :] = _seam_next(zx_ref[0:_RNN_HALO, :])

    def copy(i, carry):
        r = pl.multiple_of(i * chunk, chunk)
        xp_ref[pl.ds(_RNN_HALO + r, chunk), :] = zx_ref[pl.ds(r, chunk), :]
        return carry

    lax.fori_loop(0, n_chunks, copy, 0)

    lam = lam_ref[...]
    log_sig = jnp.minimum(lam, 0.0) - jnp.log1p(jnp.exp(-jnp.abs(lam)))
    half_c = (0.5 * LRU_C * math.log2(math.e)) * log_sig
    row = lax.broadcasted_iota(jnp.int32, (SEG, D_RNN), 0)

    def run(reverse):
        def chunk_body(c, carry):
            cc = (n_chunks - 1 - c) if reverse else c
            r0 = pl.multiple_of(cc * chunk, chunk)
            xc = jnp.zeros((chunk, D_RNN), F32)
            for j in range(RNN_CONV_K):
                step = (RNN_CONV_K - 1 - j) if reverse else (j - RNN_CONV_K + 1)
                xc = xc + xp_ref[pl.ds(r0 + _RNN_HALO + step * SEG, chunk), :] * cw_ref[j:j + 1, :]
            xc = xc + cb_ref[...]
            gates = _mm(xc.astype(BF16), w_ref[...]) + b_ref[...]
            tanh_r = jnp.tanh(gates[:, :D_RNN])
            i_gate = 0.5 + 0.5 * jnp.tanh(gates[:, D_RNN:])
            a = jnp.exp2(half_c + half_c * tanh_r)
            a_ref[...] = a
            bb_ref[...] = jnp.sqrt((1.0 - a) * (1.0 + a)) * (i_gate * xc)

            def step_body(k, state):
                a_cum, h_zero = state
                kk = (steps - 1 - k) if reverse else k
                t0 = pl.multiple_of(kk * SEG, SEG)
                at = a_ref[pl.ds(t0, SEG), :]
                h_zero = at * h_zero + bb_ref[pl.ds(t0, SEG), :]
                a_cum = at * a_cum
                h_ref[pl.ds(r0 + t0, SEG), :] = h_zero
                ac_ref[pl.ds(r0 + t0, SEG), :] = a_cum
                return a_cum, h_zero

            return lax.fori_loop(0, steps, step_body, carry, unroll=8)

        a_end, h_end = lax.fori_loop(0, n_chunks, chunk_body,
                                     (jnp.ones((SEG, D_RNN), F32), jnp.zeros((SEG, D_RNN), F32)))

        start = SEG - 1 if reverse else 0
        state_in = jnp.where(row == start, h0_ref[...], 0.0)
        for i in range(1, SEG):
            s = start - i if reverse else start + i
            leaving = h_end + a_end * state_in
            state_in = jnp.where(row == s, pltpu.roll(leaving, SEG - 1 if reverse else 1, 0), state_in)

        state_rows = jnp.concatenate([state_in] * (_RNN_OUT_ROWS // SEG), axis=0)

        def fix(k, carry):
            t0 = pl.multiple_of(k * _RNN_OUT_ROWS, _RNN_OUT_ROWS)
            rows = pl.ds(t0, _RNN_OUT_ROWS)
            h = h_ref[rows, :] + ac_ref[rows, :] * state_rows
            if reverse:
                o_ref[rows, :] = ((hf_ref[rows, :] + h) * jax.nn.gelu(zr_ref[rows, :])).astype(BF16)
            else:
                hf_ref[rows, :] = h
            return carry

        lax.fori_loop(0, T // _RNN_OUT_ROWS, fix, 0, unroll=4)
        edge = slice(0, SEG) if reverse else slice(T - SEG, T)
        last_rows = h_ref[edge, :] + ac_ref[edge, :] * state_in
        final = last_rows[0:1, :] if reverse else last_rows[SEG - 1:SEG, :]
        hl_ref[...] = jnp.broadcast_to(final, (SEG, D_RNN))

    d = pl.program_id(1)
    pl.when(d == 0)(lambda: run(False))
    pl.when(d == 1)(lambda: run(True))


def _rnn_call(zx, zr, h0, l, cw, cb, w, b, lam, *, chunk):
    B, T, C = zx.shape
    per_dir = lambda a: pl.BlockSpec((None, None) + tuple(a.shape[2:]), lambda b_, d: (l, d, 0, 0))
    state = pl.BlockSpec((None, None, SEG, C), lambda b_, d: (b_, d, 0, 0))
    seq = pl.BlockSpec((None, T, C), lambda b_, d: (b_, 0, 0))
    return pl.pallas_call(
        functools.partial(_rnn_kernel, chunk=chunk),
        grid=(B, 2),
        in_specs=[seq, seq, state, per_dir(cw), per_dir(cb), per_dir(w), per_dir(b), per_dir(lam)],
        out_specs=[seq, state],
        out_shape=[jax.ShapeDtypeStruct((B, T, C), BF16),
                   jax.ShapeDtypeStruct((B, 2, SEG, C), F32)],
        scratch_shapes=[pltpu.VMEM((T + 2 * _RNN_HALO, C), F32),
                        pltpu.VMEM((chunk, C), F32), pltpu.VMEM((chunk, C), F32),
                        pltpu.VMEM((T, C), F32), pltpu.VMEM((T, C), F32), pltpu.VMEM((T, C), F32)],
        compiler_params=_params("arbitrary", "arbitrary"),
        name="rglru",
    )(zx, zr, h0, cw, cb, w, b, lam)


_MERGE_COLS = 256


def _merge_kernel(x_ref, h_ref, gt_ref, gpost_ref, wg_ref, at_ref,
                  up_ref, u_ref, un_ref, cw_ref, cb_ref, lg_ref, lb_ref,
                  rn_ref, wao_ref, wco_ref, wro_ref, wout_ref, o_ref, pad_ref, wb_ref, cv_ref, m_ref, cp_ref,
                  mb_ref, *slab):
    D = x_ref.shape[-1]
    _conv_fill(up_ref, u_ref, un_ref, cw_ref, pad_ref, wb_ref)
    tm = u_ref.shape[0]
    conv = functools.partial(_conv_rows, pad_ref, wb_ref, cb_ref, lg_ref, lb_ref, cv_ref)

    def gate(branch, cols):
        first = _OFF_GATE + branch * D
        return jax.nn.sigmoid(_mm(h_ref[...], wg_ref[:, first + cols.start:first + cols.stop]))

    quarter = tm // 4
    half = D // 2
    groups = ((at_ref, wao_ref, 0, 0), (at_ref, wao_ref, 0, half), (rn_ref, wro_ref, 2, 0),
              (rn_ref, wro_ref, 2, half))
    for i, (act_ref, w_ref, branch, col0) in enumerate(groups):
        for c in range(col0, col0 + half, _MERGE_COLS):
            cols = slice(c, c + _MERGE_COLS)
            term = gate(branch, cols) * _mm(act_ref[...], w_ref[:, cols])
            m_ref[:, cols] = term if branch == 0 else m_ref[:, cols] + term
        rows = slice(i * quarter, (i + 1) * quarter)
        conv(rows.start, rows.stop)
        cp_ref[rows, :] = _mm(cv_ref[rows, :], wco_ref[...])
    for c in range(0, D, _MERGE_COLS):
        cols = slice(c, c + _MERGE_COLS)
        mb_ref[:, cols] = (m_ref[:, cols] + gate(1, cols) * cp_ref[:, cols]).astype(BF16)
    o = _mm(mb_ref[...], wout_ref[...])
    if slab:
        _load_interleaved(x_ref, slab[0])
        x = _read_slabs(slab[0])
    else:
        x = x_ref[...]
    o_ref[...] = x + gt_ref[...] * _rms(o, gpost_ref[...])


def _merge_call(x, h, mods, l, mod_row, gpost, wg, attn, u, cw, cb, lg, lb, rnn, wao, wco, wro, wout, *, tm):
    time_order = x.ndim == 4
    B, D = x.shape[0], x.shape[-1]
    T = attn.shape[1]
    tok = lambda w_: pl.BlockSpec((None, tm, w_), lambda b, t: (b, t, 0))
    u_prev, u_next = _halo_specs(tm, _CONV_HALO, D_CONV, T)
    return pl.pallas_call(
        _merge_kernel,
        grid=(B, T // tm),
        in_specs=[_time_order_spec(tm, D) if time_order else tok(D), tok(D),
                  _mod_spec(l, 2, mod_row)(D), _layer_spec(gpost, l),
                  _layer_spec(wg, l), tok(D_Q),
                  u_prev, tok(D_CONV), u_next, _layer_spec(cw, l), _layer_spec(cb, l),
                  _layer_spec(lg, l), _layer_spec(lb, l), tok(D_RNN),
                  _layer_spec(wao, l), _layer_spec(wco, l), _layer_spec(wro, l), _layer_spec(wout, l)],
        out_specs=tok(D),
        out_shape=jax.ShapeDtypeStruct((B, T, D), F32),
        scratch_shapes=[pltpu.VMEM((tm + 2 * _CONV_HALO, D_CONV), F32),
                        pltpu.VMEM((CONV_K * SEG, D_CONV), F32), pltpu.VMEM((tm, D_CONV), BF16),
                        pltpu.VMEM((tm, D), F32), pltpu.VMEM((tm, D), F32), pltpu.VMEM((tm, D), BF16)]
        + ([_slab_scratch(tm, D)] if time_order else []),
        compiler_params=_params("arbitrary", "arbitrary"),
        name="merge",
    )(x, h, mods, gpost, wg, attn, u, u, u, cw, cb, lg, lb, rnn, wao, wco, wro, wout)


_FFN_CHUNK = 3072


def _ffn_kernel(xp_ref, x_ref, xn_ref, sh_ref, sc_ref, gt_ref, gpre_ref, gpost_ref, up_ref,
                dw_ref, db_ref, down_ref, o_ref, hp_ref, hn_ref, *slab):
    tm = x_ref.shape[0]
    t = pl.program_id(1)
    mod = lambda v: _modulated(v, gpre_ref[...], sh_ref[...], sc_ref[...])
    hp_ref[...] = mod(xp_ref[...])
    hn_ref[...] = mod(xn_ref[...])

    @pl.when(t == 0)
    def _():
        hp_ref[...] = _seam_prev(hp_ref[...])

    @pl.when(t == pl.num_programs(1) - 1)
    def _():
        hn_ref[...] = _seam_next(hn_ref[...])

    h = jnp.concatenate([hp_ref[...], mod(x_ref[...]), hn_ref[...]], axis=0).astype(BF16)

    acc = jnp.zeros(x_ref.shape, F32)
    for c in range(D_FF // _FFN_CHUNK):
        cg = slice(c * _FFN_CHUNK, (c + 1) * _FFN_CHUNK)
        cv = slice(D_FF + c * _FFN_CHUNK, D_FF + (c + 1) * _FFN_CHUNK)
        ug = _mm(h, up_ref[:, cg])
        uv = _mm(h, up_ref[:, cv])
        yg = db_ref[:, cg]
        yv = db_ref[:, cv]
        for j in range(3):
            rows = slice(j * SEG, j * SEG + tm)
            yg = yg + ug[rows, :] * dw_ref[j:j + 1, cg]
            yv = yv + uv[rows, :] * dw_ref[j:j + 1, cv]
        act = (jax.nn.gelu(yg) * yv).astype(BF16)
        acc = acc + _mm(act, down_ref[cg, :])
    y = x_ref[...] + gt_ref[...] * _rms(acc, gpost_ref[...])
    if slab:
        _store_time_order(y, o_ref, slab[0])
    else:
        o_ref[...] = y


def _ffn_call(x, mods, l, mod_row, gpre, gpost, up, dw, db, down, *, tm, time_order_out=False):
    B, T, D = x.shape
    prev, nxt = _halo_specs(tm, SEG, D, T)
    if time_order_out:
        out_spec = _time_order_spec(tm, D)
        out_shape = jax.ShapeDtypeStruct((B, SEG, T // SEG, D), F32)
    else:
        out_spec = pl.BlockSpec((None, tm, D), lambda b, t: (b, t, 0))
        out_shape = jax.ShapeDtypeStruct((B, T, D), F32)
    return pl.pallas_call(
        _ffn_kernel,
        grid=(B, T // tm),
        in_specs=[prev, pl.BlockSpec((None, tm, D), lambda b, t: (b, t, 0)), nxt,
                  _mod_spec(l, 3, mod_row)(D), _mod_spec(l, 4, mod_row)(D), _mod_spec(l, 5, mod_row)(D),
                  _layer_spec(gpre, l), _layer_spec(gpost, l), _layer_spec(up, l),
                  _layer_spec(dw, l), _layer_spec(db, l), _layer_spec(down, l)],
        out_specs=out_spec,
        out_shape=out_shape,
        scratch_shapes=[pltpu.VMEM((SEG, D), F32), pltpu.VMEM((SEG, D), F32)]
        + ([_slab_scratch(tm, D)] if time_order_out else []),
        compiler_params=_params("arbitrary", "arbitrary"),
        name="conv_ffn",
    )(x, x, x, mods, mods, mods, gpre, gpost, up, dw, db, down)


def _seg_interleave(a):
    *lead, T, W = a.shape
    return a.reshape(*lead, SEG, T // SEG, W).swapaxes(-3, -2).reshape(*lead, T, W)


def _rope_tables(n):
    rows = n // GRID_W
    row = jnp.repeat(jnp.arange(rows), GRID_W).astype(F32)
    col = jnp.tile(jnp.arange(GRID_W), rows).astype(F32)
    n_freq = HEAD_DIM // 4
    freq = ROPE_THETA ** (-jnp.arange(n_freq, dtype=F32) / n_freq)
    ang = jnp.concatenate([row[:, None] * freq, col[:, None] * freq], axis=-1)
    cos, sin = jnp.cos(ang), jnp.sin(ang)
    cos_t = jnp.tile(jnp.concatenate([cos, cos], axis=-1), (1, N_Q_HEADS))
    sin_t = jnp.tile(jnp.concatenate([-sin, sin], axis=-1), (1, N_Q_HEADS))
    return cos_t, sin_t


def _tile_plan(n_tokens):
    return min(n_tokens, 512), min(n_tokens, 1024), min(n_tokens, 512)


def _block_diag(w):
    *lead, nb, bi, bj = w.shape
    eye = jnp.eye(nb, dtype=w.dtype)
    return jnp.einsum('...hij,hg->...higj', w, eye).reshape(*lead, nb * bi, nb * bj)


def kernel(x, c, ctx, c_ctx, w_mod, b_mod, norm_pre_mix, norm_post_mix, norm_pre_ffn, norm_post_ffn,
           w_in, q_norm, k_norm, w_attn_out, conv_dw, conv_dw_b, conv_ln_g, conv_ln_b, w_conv_out,
           rnn_conv_w, rnn_conv_b, rnn_wa, rnn_ba, rnn_wx, rnn_bx, rnn_lambda, w_rnn_out, w_out,
           ffn_up, ffn_dw, ffn_dw_b, ffn_down):
    B, N, D = x.shape
    M = ctx.shape[1]
    L = w_mod.shape[0]
    tm_lat, tq_lat, chunk_lat = _tile_plan(N)
    tm_ctx, _, chunk_ctx = _tile_plan(M)

    mod_rows = 2 * SUBLANES
    cc = jnp.concatenate([c, c_ctx[None, :], jnp.zeros((mod_rows - B - 1, D), F32)], axis=0)
    mods = _mod_call(cc, w_mod, b_mod[:, None, :]).reshape(L, mod_rows, N_MOD, 1, D)

    cos_t, sin_t = (_seg_interleave(t) for t in _rope_tables(N))
    head_id = jnp.arange(D_Q) // HEAD_DIM
    bd = (head_id[:, None] == head_id[None, :]).astype(BF16)
    h0 = jnp.zeros((B, 2, SEG, D_RNN), F32)

    vec = lambda a: a[:, None, :]
    w_proj = w_gate = w_in.astype(BF16)
    qg = vec(jnp.tile(q_norm, (1, N_Q_HEADS)))
    kg = vec(jnp.tile(k_norm, (1, N_KV_HEADS)))
    rnn_w = (0.5 * jnp.concatenate([_block_diag(rnn_wa), _block_diag(rnn_wx)], axis=-1)).astype(BF16)
    rnn_b = 0.5 * jnp.concatenate([rnn_ba, rnn_bx], axis=-1)[:, :, None, :]
    rnn_cb = rnn_conv_b[:, :, None, :]
    lam = rnn_lambda[:, :, None, :]
    wao, wco, wro, wout = (w_attn_out.astype(BF16), w_conv_out.astype(BF16), w_rnn_out.astype(BF16),
                           w_out.astype(BF16))
    up, down = ffn_up.astype(BF16), ffn_down.astype(BF16)
    gpre_mix, gpost_mix, gpre_ffn, gpost_ffn = (vec(norm_pre_mix), vec(norm_post_mix), vec(norm_pre_ffn),
                                                vec(norm_post_ffn))
    cdb, clg, clb, fdb = vec(conv_dw_b), vec(conv_ln_g), vec(conv_ln_b), vec(ffn_dw_b)

    x = x.reshape(B, SEG, N // SEG, D)
    ctx = ctx.reshape(B, SEG, M // SEG, D)
    for l in range(L):
        need_ctx = l < L - 1
        hcx, qc, kc, vc, uc, zxc, zrc = _inproj_call(ctx, mods, l, B, gpre_mix, w_proj, qg, kg, bd, None, None,
                                                     tm=tm_ctx)
        rnn_c, hc_last = _rnn_call(zxc, zrc, h0, l, rnn_conv_w, rnn_cb, rnn_w, rnn_b, lam, chunk=chunk_ctx)

        hx, q, k, v, u, zx, zr = _inproj_call(x, mods, l, None, gpre_mix, w_proj, qg, kg, bd, cos_t, sin_t,
                                              tm=tm_lat)
        attn = _attn_call(q, [(kc, vc), (k, v)], tq=tq_lat)
        rnn, _ = _rnn_call(zx, zr, hc_last, l, rnn_conv_w, rnn_cb, rnn_w, rnn_b, lam, chunk=chunk_lat)
        x = _merge_call(x, hx, mods, l, None, gpost_mix, w_gate, attn, u, conv_dw, cdb, clg, clb,
                        rnn, wao, wco, wro, wout, tm=tm_lat)
        x = _ffn_call(x, mods, l, None, gpre_ffn, gpost_ffn, up, ffn_dw, fdb, down, tm=tm_lat,
                      time_order_out=not need_ctx)

        if need_ctx:
            attn_c = _attn_call(qc, [(kc, vc)], tq=tm_ctx)
            ctx = _merge_call(ctx, hcx, mods, l, B, gpost_mix, w_gate, attn_c, uc, conv_dw, cdb, clg, clb,
                              rnn_c, wao, wco, wro, wout, tm=tm_ctx)
            ctx = _ffn_call(ctx, mods, l, B, gpre_ffn, gpost_ffn, up, ffn_dw, fdb, down, tm=tm_ctx)
    return x.reshape(B, N, D)
```

```python
import functools
import math

import jax
import jax.numpy as jnp
from jax import lax
from jax.experimental import pallas as pl
from jax.experimental.pallas import tpu as pltpu

F32 = jnp.float32
BF16 = jnp.bfloat16

EPS = 1e-6
HEAD_DIM = 64
N_Q_HEADS = 8
N_KV_HEADS = 2
Q_GROUP = N_Q_HEADS // N_KV_HEADS
D_Q = N_Q_HEADS * HEAD_DIM
D_KV = N_KV_HEADS * HEAD_DIM
D_CONV = 512
CONV_K = 31
D_RNN = 512
RNN_CONV_K = 4
LRU_C = 8.0
D_FF = 3072
GRID_W = 64
ROPE_THETA = 10000.0
N_MOD = 6

LANES = 128
SUBLANES = 8
SEG = SUBLANES
VMEM_LIMIT = 56 * 1024 * 1024

_OFF_Q = 0
_OFF_KV = D_Q
_OFF_CONV = _OFF_KV + 2 * D_KV
_OFF_ZX = _OFF_CONV + 2 * D_CONV
_OFF_ZR = _OFF_ZX + D_RNN
_OFF_GATE = _OFF_ZR + D_RNN

_Q_SCALE = HEAD_DIM ** -0.5 * math.log2(math.e)


def _const_spec(shape):
    zeros = (0,) * len(shape)
    return pl.BlockSpec(tuple(shape), lambda *_: zeros, pipeline_mode=pl.Buffered(1))


def _layer_spec(arr, l):
    tail = (0,) * (arr.ndim - 1)
    return pl.BlockSpec((None,) + tuple(arr.shape[1:]), lambda *_: (l,) + tail,
                        pipeline_mode=pl.Buffered(1))


def _mod_spec(l, k, row):
    def index(b, *_):
        return (l, b if row is None else row, k, 0, 0)
    return lambda d: pl.BlockSpec((None, None, None, 1, d), index)


def _params(*sem):
    return pltpu.CompilerParams(dimension_semantics=sem, vmem_limit_bytes=VMEM_LIMIT)


def _rms(x, g):
    return x * lax.rsqrt(jnp.mean(x * x, axis=-1, keepdims=True) + EPS) * g


def _mm(a, b):
    return jnp.dot(a, b, preferred_element_type=F32)


def _modulated(x, g, shift, scale):
    return _rms(x, g) * (1.0 + scale) + shift


def _seam_prev(a):
    sub = lax.broadcasted_iota(jnp.int32, a.shape, 0) % SEG
    return jnp.where(sub == 0, 0.0, pltpu.roll(a, 1, 0))


def _seam_next(a):
    sub = lax.broadcasted_iota(jnp.int32, a.shape, 0) % SEG
    return jnp.where(sub == SEG - 1, 0.0, pltpu.roll(a, a.shape[0] - 1, 0))


def _time_order_spec(tm, width):
    return pl.BlockSpec((None, SEG, tm // SEG, width), lambda b, t: (b, 0, t, 0))


def _slab_scratch(tm, width):
    return pltpu.VMEM((width // LANES, tm, LANES), F32)


def _load_interleaved(x_ref, slab_ref):
    seg, steps, width = x_ref.shape
    for j in range(width // LANES):
        for s in range(seg):
            slab_ref[j, pl.ds(s, steps, stride=seg), :] = x_ref[s, :, j * LANES:(j + 1) * LANES]


def _read_slabs(slab_ref):
    return jnp.concatenate([slab_ref[j] for j in range(slab_ref.shape[0])], axis=1)


def _store_time_order(y, o_ref, slab_ref):
    seg, steps, width = o_ref.shape
    for j in range(width // LANES):
        slab_ref[j] = y[:, j * LANES:(j + 1) * LANES]
    for j in range(width // LANES):
        for s in range(seg):
            o_ref[s, :, j * LANES:(j + 1) * LANES] = slab_ref[j, pl.ds(s, steps, stride=seg), :]


def _halo_specs(tm, halo, width, n_rows):
    per = tm // halo
    n_blocks = n_rows // halo
    prev = pl.BlockSpec((None, halo, width), lambda b, t: (b, (t * per + n_blocks - 1) % n_blocks, 0))
    nxt = pl.BlockSpec((None, halo, width), lambda b, t: (b, ((t + 1) * per) % n_blocks, 0))
    return prev, nxt


def _mod_kernel(c_ref, w_ref, b_ref, o_ref):
    c = c_ref[...]
    a = (c * jax.nn.sigmoid(c)).astype(BF16)
    o_ref[...] = _mm(a, w_ref[...].astype(BF16)) + b_ref[...]


def _mod_call(cc, w_mod, b_mod):
    L, D, D6 = w_mod.shape
    R = cc.shape[0]
    bn = D6 // 4
    return pl.pallas_call(
        _mod_kernel,
        grid=(L, D6 // bn),
        in_specs=[
            pl.BlockSpec((R, D), lambda l, j: (0, 0)),
            pl.BlockSpec((None, D, bn), lambda l, j: (l, 0, j)),
            pl.BlockSpec((None, 1, bn), lambda l, j: (l, 0, j)),
        ],
        out_specs=pl.BlockSpec((None, R, bn), lambda l, j: (l, 0, j)),
        out_shape=jax.ShapeDtypeStruct((L, R, D6), F32),
        compiler_params=_params("arbitrary", "arbitrary"),
        name="adaln_mod",
    )(cc, w_mod, b_mod)


def _head_rms(z, gain, bd):
    ms = _mm((z * z).astype(BF16), bd) * (1.0 / HEAD_DIM)
    return z * lax.rsqrt(ms + EPS) * gain


def _rope(x, c, s):
    rows, width = x.shape
    lane = lax.broadcasted_iota(jnp.int32, (rows, LANES), 1)
    first_half = (lane & (HEAD_DIM // 2)) == 0
    cols = []
    for j in range(width // LANES):
        sl = slice(j * LANES, (j + 1) * LANES)
        xc = x[:, sl]
        partner = jnp.where(first_half,
                            pltpu.roll(xc, LANES - HEAD_DIM // 2, 1),
                            pltpu.roll(xc, HEAD_DIM // 2, 1))
        cols.append(xc * c[:, sl] + partner * s[:, sl])
    return jnp.concatenate(cols, axis=1) if len(cols) > 1 else cols[0]


def _inproj_kernel(*refs, rope, time_order, keys_only):
    x_ref, sh_ref, sc_ref, g_ref, w_ref, qg_ref, kg_ref, bd_ref = refs[:8]
    refs = refs[8:]
    if rope:
        cos_ref, sin_ref = refs[:2]
        refs = refs[2:]
    n_out = 3 if keys_only else 7
    outs, scratch = refs[:n_out], refs[n_out:]
    if time_order:
        _load_interleaved(x_ref, scratch[0])
        x = _read_slabs(scratch[0])
    else:
        x = x_ref[...]
    h = _modulated(x, g_ref[...], sh_ref[...], sc_ref[...]).astype(BF16)
    if keys_only:
        k_ref, v_ref, zx_ref = outs
    else:
        h_ref, q_ref, k_ref, v_ref, u_ref, zx_ref, zr_ref = outs
        h_ref[...] = h
        zq = _mm(h, w_ref[:, _OFF_Q:_OFF_Q + D_Q])
        qn = _head_rms(zq, qg_ref[...], bd_ref[...])
        if rope:
            qn = _rope(qn, cos_ref[...], sin_ref[...])
        q_ref[...] = (qn * _Q_SCALE).astype(BF16)

    zkv = _mm(h, w_ref[:, _OFF_KV:_OFF_KV + 2 * D_KV])
    kn = _head_rms(zkv[:, :D_KV], kg_ref[...], bd_ref[:D_KV, :D_KV])
    if rope:
        kn = _rope(kn, cos_ref[:, :D_KV], sin_ref[:, :D_KV])
    k_ref[...] = kn
    v_ref[...] = zkv[:, D_KV:]

    if not keys_only:
        zc = _mm(h, w_ref[:, _OFF_CONV:_OFF_CONV + 2 * D_CONV])
        u_ref[...] = zc[:, :D_CONV] * jax.nn.sigmoid(zc[:, D_CONV:])

    zx_ref[...] = _mm(h, w_ref[:, _OFF_ZX:_OFF_ZX + D_RNN])
    if not keys_only:
        zr_ref[...] = _mm(h, w_ref[:, _OFF_ZR:_OFF_ZR + D_RNN])


def _inproj_call(x, mods, l, mod_row, gain, w, qg, kg, bd, cos, sin, *, tm, keys_only=False):
    time_order = x.ndim == 4
    B, D = x.shape[0], x.shape[-1]
    T = x.shape[1] * x.shape[2] if time_order else x.shape[1]
    rope = cos is not None
    tok = lambda w_: pl.BlockSpec((None, tm, w_), lambda b, t: (b, t, 0))
    x_spec = _time_order_spec(tm, D) if time_order else tok(D)
    in_specs = [x_spec, _mod_spec(l, 0, mod_row)(D), _mod_spec(l, 1, mod_row)(D),
                _layer_spec(gain, l), _layer_spec(w, l), _layer_spec(qg, l), _layer_spec(kg, l),
                _const_spec(bd.shape)]
    args = [x, mods, mods, gain, w, qg, kg, bd]
    if rope:
        tab = pl.BlockSpec((tm, D_Q), lambda b, t: (t, 0))
        in_specs += [tab, tab]
        args += [cos, sin]
    out_shape = [jax.ShapeDtypeStruct((B, T, D), BF16),
                 jax.ShapeDtypeStruct((B, T, D_Q), BF16),
                 jax.ShapeDtypeStruct((B, T, D_KV), F32),
                 jax.ShapeDtypeStruct((B, T, D_KV), F32),
                 jax.ShapeDtypeStruct((B, T, D_CONV), F32),
                 jax.ShapeDtypeStruct((B, T, D_RNN), F32),
                 jax.ShapeDtypeStruct((B, T, D_RNN), F32)]
    out_specs = [tok(D), tok(D_Q), tok(D_KV), tok(D_KV), tok(D_CONV), tok(D_RNN), tok(D_RNN)]
    if keys_only:
        keep = (2, 3, 5)
        out_shape = [out_shape[i] for i in keep]
        out_specs = [out_specs[i] for i in keep]
    return pl.pallas_call(
        functools.partial(_inproj_kernel, rope=rope, time_order=time_order, keys_only=keys_only),
        grid=(B, T // tm),
        in_specs=in_specs, out_specs=out_specs, out_shape=out_shape,
        scratch_shapes=[_slab_scratch(tm, D)] if time_order else [],
        compiler_params=_params("arbitrary", "arbitrary"),
        name="inproj_rope" if rope else "inproj",
    )(*args)


_VT_ROWS = HEAD_DIM + 2 * SUBLANES
_ATTN_KEYS = 256
_ATTN_SAFE_BOUND = 60.0


def _attn_kernel(*refs, seg_lens):
    nseg = len(seg_lens)
    q_ref = refs[0]
    kv_refs = refs[1:1 + 2 * nseg]
    o_ref = refs[1 + 2 * nseg]
    kq_ref, vt_ref, kn_ref = refs[2 + 2 * nseg:]
    m_tot = sum(seg_lens)
    g = pl.program_id(1)

    @pl.when(pl.program_id(2) == 0)
    def _():
        off = 0
        k_norm2 = jnp.zeros((1, 1), F32)
        for i, ms in enumerate(seg_lens):
            lane_half = lax.broadcasted_iota(jnp.int32, (ms, LANES), 1) // HEAD_DIM
            k = kv_refs[2 * i][...]
            kd = jnp.where(lane_half == g, k, pltpu.roll(k, HEAD_DIM, 1)).astype(BF16)
            kq_ref[off:off + ms, :] = jnp.concatenate([kd, kd], axis=1)
            own = jnp.where(lane_half == g, k, 0.0)
            k_norm2 = jnp.maximum(k_norm2, jnp.max(jnp.sum(own * own, axis=1, keepdims=True),
                                                   axis=0, keepdims=True))
            v = kv_refs[2 * i + 1][...]
            vd = jnp.where(lane_half == g, v, pltpu.roll(v, HEAD_DIM, 1))
            vt_ref[0:HEAD_DIM, off:off + ms] = vd.T[0:HEAD_DIM, :].astype(BF16)
            off += ms
        vt_ref[HEAD_DIM:, :] = jnp.ones((_VT_ROWS - HEAD_DIM, m_tot), BF16)
        kn_ref[...] = jnp.broadcast_to(k_norm2, kn_ref.shape)

    q = q_ref[...]
    head_lane = lax.broadcasted_iota(jnp.int32, q.shape, 1) // HEAD_DIM
    q_heads = [jnp.where(head_lane == h, q, jnp.zeros_like(q)) for h in range(Q_GROUP)]

    def scores(h):
        return lax.dot_general(kq_ref[...], q_heads[h], (((1,), (1,)), ((), ())),
                               preferred_element_type=F32)

    def finish(outs):
        for c in range(Q_GROUP // 2):
            pair = jnp.concatenate([outs[2 * c], outs[2 * c + 1]], axis=0)
            o_ref[:, c * LANES:(c + 1) * LANES] = pair.T.astype(BF16)

    qf = q.astype(F32)
    row_id = lax.broadcasted_iota(jnp.int32, (2 * SUBLANES, q.shape[1]), 0)
    lane_id = lax.broadcasted_iota(jnp.int32, (2 * SUBLANES, q.shape[1]), 1) // HEAD_DIM
    head_rows = jnp.where(row_id == lane_id, 1.0, 0.0).astype(BF16)
    q_norm2 = lax.dot_general(head_rows, (qf * qf).astype(BF16), (((1,), (1,)), ((), ())),
                              preferred_element_type=F32)
    bound = jnp.sqrt(q_norm2 * kn_ref[0:1, 0:1] * (1.0 + 2.0 ** -7))
    safe = jnp.max(bound) <= _ATTN_SAFE_BOUND

    @pl.when(safe)
    def _():
        outs = []
        st_next = scores(0)
        for h in range(Q_GROUP):
            st = st_next
            if h + 1 < Q_GROUP:
                st_next = scores(h + 1)
            p = jnp.exp2(st - bound[h:h + 1, :]).astype(BF16)
            acc = _mm(vt_ref[...], p)
            outs.append(acc[0:HEAD_DIM, :] * (1.0 / acc[HEAD_DIM:HEAD_DIM + 1, :]))
        finish(outs)

    @pl.when(jnp.logical_not(safe))
    def _():
        keys = min(_ATTN_KEYS, m_tot)
        n_chunks = m_tot // keys
        outs = []
        st_next = scores(0)
        for h in range(Q_GROUP):
            st_all = st_next
            if h + 1 < Q_GROUP:
                st_next = scores(h + 1)
            for c in range(n_chunks):
                rows = slice(c * keys, (c + 1) * keys)
                st = st_all[rows, :]
                vt = vt_ref[:, rows]
                m_chunk = jnp.max(st, axis=0, keepdims=True)
                if c == 0:
                    m_run = m_chunk
                    acc = _mm(vt, jnp.exp2(st - m_run).astype(BF16))
                else:
                    m_new = jnp.maximum(m_run, m_chunk)
                    acc = acc * jnp.exp2(m_run - m_new) + _mm(vt, jnp.exp2(st - m_new).astype(BF16))
                    m_run = m_new
            outs.append(acc[0:HEAD_DIM, :] * (1.0 / acc[HEAD_DIM:HEAD_DIM + 1, :]))
        finish(outs)


def _attn_call(q, segs, *, tq):
    B, T, _ = q.shape
    seg_lens = tuple(k.shape[1] for k, _ in segs)
    m_tot = sum(seg_lens)
    gw = Q_GROUP * HEAD_DIM
    in_specs = [pl.BlockSpec((None, tq, gw), lambda b, g, t: (b, t, g))]
    args = [q]
    for k, v in segs:
        spec = pl.BlockSpec((None, k.shape[1], D_KV), lambda b, g, t: (b, 0, 0))
        in_specs += [spec, spec]
        args += [k, v]
    return pl.pallas_call(
        functools.partial(_attn_kernel, seg_lens=seg_lens),
        grid=(B, N_KV_HEADS, T // tq),
        in_specs=in_specs,
        out_specs=pl.BlockSpec((None, tq, gw), lambda b, g, t: (b, t, g)),
        out_shape=jax.ShapeDtypeStruct((B, T, D_Q), BF16),
        scratch_shapes=[pltpu.VMEM((m_tot, gw), BF16),
                        pltpu.VMEM((_VT_ROWS, m_tot), BF16),
                        pltpu.VMEM((SUBLANES, LANES), F32)],
        compiler_params=_params("arbitrary", "arbitrary", "arbitrary"),
        name="gqa_attention",
    )(*args)


_CONV_HALO = 16 * SEG
_CONV_ROWS = 32


def _conv_fill(up_ref, u_ref, un_ref, w_ref, pad_ref, wb_ref):
    tm = u_ref.shape[0]
    t = pl.program_id(1)
    for j in range(CONV_K):
        wb_ref[j * SEG:(j + 1) * SEG, :] = jnp.broadcast_to(w_ref[j:j + 1, :], (SEG, D_CONV))
    pad_ref[0:_CONV_HALO, :] = up_ref[...]
    pad_ref[_CONV_HALO:_CONV_HALO + tm, :] = u_ref[...]
    pad_ref[_CONV_HALO + tm:, :] = un_ref[...]

    @pl.when(t == 0)
    def _():
        pad_ref[0:_CONV_HALO, :] = _seam_prev(up_ref[...])

    @pl.when(t == pl.num_programs(1) - 1)
    def _():
        pad_ref[_CONV_HALO + tm:, :] = _seam_next(un_ref[...])


def _conv_rows(pad_ref, wb_ref, b_ref, lg_ref, lb_ref, o_ref, row_lo, row_hi):
    first = _CONV_HALO // SEG - CONV_K // 2
    for r in range(row_lo, row_hi, _CONV_ROWS):
        acc = jnp.zeros((_CONV_ROWS, D_CONV), F32)
        for j in range(CONV_K):
            lo = r + (first + j) * SEG
            w_tile = wb_ref[j * SEG:(j + 1) * SEG, :]
            acc = acc + pad_ref[lo:lo + _CONV_ROWS, :] * jnp.concatenate([w_tile] * (_CONV_ROWS // SEG), axis=0)
        acc = acc + b_ref[...]
        mu = jnp.mean(acc, axis=-1, keepdims=True)
        cen = acc - mu
        var = jnp.mean(cen * cen, axis=-1, keepdims=True)
        y = cen * lax.rsqrt(var + EPS) * lg_ref[...] + lb_ref[...]
        o_ref[r:r + _CONV_ROWS, :] = (y * jax.nn.sigmoid(y)).astype(BF16)


_RNN_HALO = 4 * SEG


_RNN_OUT_ROWS = 2 * SUBLANES


def _rnn_kernel(*refs, chunk, emit):
    if emit:
        (zx_ref, zr_ref, h0_ref, cw_ref, cb_ref, w_ref, b_ref, lam_ref, o_ref, hl_ref,
         xp_ref, a_ref, bb_ref, ac_ref, h_ref, hf_ref) = refs
    else:
        (zx_ref, h0_ref, cw_ref, cb_ref, w_ref, b_ref, lam_ref, hl_ref,
         xp_ref, a_ref, bb_ref, ac_ref, h_ref) = refs
    T = zx_ref.shape[0]
    n_chunks = T // chunk
    steps = chunk // SEG

    xp_ref[0:_RNN_HALO, :] = _seam_prev(zx_ref[T - _RNN_HALO:T, :])
    xp_ref[_RNN_HALO + T:, :] = _seam_next(zx_ref[0:_RNN_HALO, :])

    def copy(i, carry):
        r = pl.multiple_of(i * chunk, chunk)
        xp_ref[pl.ds(_RNN_HALO + r, chunk), :] = zx_ref[pl.ds(r, chunk), :]
        return carry

    lax.fori_loop(0, n_chunks, copy, 0)

    lam = lam_ref[...]
    log_sig = jnp.minimum(lam, 0.0) - jnp.log1p(jnp.exp(-jnp.abs(lam)))
    half_c = (0.5 * LRU_C * math.log2(math.e)) * log_sig
    row = lax.broadcasted_iota(jnp.int32, (SEG, D_RNN), 0)

    def run(reverse):
        def chunk_body(c, carry):
            cc = (n_chunks - 1 - c) if reverse else c
            r0 = pl.multiple_of(cc * chunk, chunk)
            xc = jnp.zeros((chunk, D_RNN), F32)
            for j in range(RNN_CONV_K):
                step = (RNN_CONV_K - 1 - j) if reverse else (j - RNN_CONV_K + 1)
                xc = xc + xp_ref[pl.ds(r0 + _RNN_HALO + step * SEG, chunk), :] * cw_ref[j:j + 1, :]
            xc = xc + cb_ref[...]
            gates = _mm(xc.astype(BF16), w_ref[...]) + b_ref[...]
            tanh_r = jnp.tanh(gates[:, :D_RNN])
            i_gate = 0.5 + 0.5 * jnp.tanh(gates[:, D_RNN:])
            a = jnp.exp2(half_c + half_c * tanh_r)
            a_ref[...] = a
            bb_ref[...] = jnp.sqrt((1.0 - a) * (1.0 + a)) * (i_gate * xc)

            def step_body(k, state):
                a_cum, h_zero = state
                kk = (steps - 1 - k) if reverse else k
                t0 = pl.multiple_of(kk * SEG, SEG)
                at = a_ref[pl.ds(t0, SEG), :]
                h_zero = at * h_zero + bb_ref[pl.ds(t0, SEG), :]
                a_cum = at * a_cum
                h_ref[pl.ds(r0 + t0, SEG), :] = h_zero
                ac_ref[pl.ds(r0 + t0, SEG), :] = a_cum
                return a_cum, h_zero

            return lax.fori_loop(0, steps, step_body, carry, unroll=8)

        a_end, h_end = lax.fori_loop(0, n_chunks, chunk_body,
                                     (jnp.ones((SEG, D_RNN), F32), jnp.zeros((SEG, D_RNN), F32)))

        start = SEG - 1 if reverse else 0
        state_in = jnp.where(row == start, h0_ref[...], 0.0)
        for i in range(1, SEG):
            s = start - i if reverse else start + i
            leaving = h_end + a_end * state_in
            state_in = jnp.where(row == s, pltpu.roll(leaving, SEG - 1 if reverse else 1, 0), state_in)

        state_rows = jnp.concatenate([state_in] * (_RNN_OUT_ROWS // SEG), axis=0)

        def fix(k, carry):
            t0 = pl.multiple_of(k * _RNN_OUT_ROWS, _RNN_OUT_ROWS)
            rows = pl.ds(t0, _RNN_OUT_ROWS)
            h = h_ref[rows, :] + ac_ref[rows, :] * state_rows
            if reverse:
                o_ref[rows, :] = ((hf_ref[rows, :] + h) * jax.nn.gelu(zr_ref[rows, :])).astype(BF16)
            else:
                hf_ref[rows, :] = h
            return carry

        if emit:
            lax.fori_loop(0, T // _RNN_OUT_ROWS, fix, 0, unroll=4)
        edge = slice(0, SEG) if reverse else slice(T - SEG, T)
        last_rows = h_ref[edge, :] + ac_ref[edge, :] * state_in
        final = last_rows[0:1, :] if reverse else last_rows[SEG - 1:SEG, :]
        hl_ref[...] = jnp.broadcast_to(final, (SEG, D_RNN))

    d = pl.program_id(1)
    pl.when(d == 0)(lambda: run(False))
    pl.when(d == 1)(lambda: run(True))


def _rnn_call(zx, zr, h0, l, cw, cb, w, b, lam, *, chunk):
    emit = zr is not None
    B, T, C = zx.shape
    per_dir = lambda a: pl.BlockSpec((None, None) + tuple(a.shape[2:]), lambda b_, d: (l, d, 0, 0))
    state = pl.BlockSpec((None, None, SEG, C), lambda b_, d: (b_, d, 0, 0))
    seq = pl.BlockSpec((None, T, C), lambda b_, d: (b_, 0, 0))
    weights = [per_dir(cw), per_dir(cb), per_dir(w), per_dir(b), per_dir(lam)]
    state_shape = jax.ShapeDtypeStruct((B, 2, SEG, C), F32)
    scratch = [pltpu.VMEM((T + 2 * _RNN_HALO, C), F32),
               pltpu.VMEM((chunk, C), F32), pltpu.VMEM((chunk, C), F32),
               pltpu.VMEM((T, C), F32), pltpu.VMEM((T, C), F32)]
    if emit:
        args = (zx, zr, h0, cw, cb, w, b, lam)
        in_specs = [seq, seq, state] + weights
        out_specs = [seq, state]
        out_shape = [jax.ShapeDtypeStruct((B, T, C), BF16), state_shape]
        scratch = scratch + [pltpu.VMEM((T, C), F32)]
    else:
        args = (zx, h0, cw, cb, w, b, lam)
        in_specs = [seq, state] + weights
        out_specs = state
        out_shape = state_shape
    return pl.pallas_call(
        functools.partial(_rnn_kernel, chunk=chunk, emit=emit),
        grid=(B, 2),
        in_specs=in_specs, out_specs=out_specs, out_shape=out_shape,
        scratch_shapes=scratch,
        compiler_params=_params("arbitrary", "arbitrary"),
        name="rglru",
    )(*args)


_MERGE_COLS = 256


def _merge_kernel(x_ref, h_ref, gt_ref, gpost_ref, wg_ref, at_ref,
                  up_ref, u_ref, un_ref, cw_ref, cb_ref, lg_ref, lb_ref,
                  rn_ref, wao_ref, wco_ref, wro_ref, wout_ref, o_ref, pad_ref, wb_ref, cv_ref, m_ref, cp_ref,
                  mb_ref, *slab):
    D = x_ref.shape[-1]
    _conv_fill(up_ref, u_ref, un_ref, cw_ref, pad_ref, wb_ref)
    tm = u_ref.shape[0]
    conv = functools.partial(_conv_rows, pad_ref, wb_ref, cb_ref, lg_ref, lb_ref, cv_ref)

    def gate(branch, cols):
        first = _OFF_GATE + branch * D
        return jax.nn.sigmoid(_mm(h_ref[...], wg_ref[:, first + cols.start:first + cols.stop]))

    quarter = tm // 4
    half = D // 2
    groups = ((at_ref, wao_ref, 0, 0), (at_ref, wao_ref, 0, half), (rn_ref, wro_ref, 2, 0),
              (rn_ref, wro_ref, 2, half))
    for i, (act_ref, w_ref, branch, col0) in enumerate(groups):
        for c in range(col0, col0 + half, _MERGE_COLS):
            cols = slice(c, c + _MERGE_COLS)
            term = gate(branch, cols) * _mm(act_ref[...], w_ref[:, cols])
            m_ref[:, cols] = term if branch == 0 else m_ref[:, cols] + term
        rows = slice(i * quarter, (i + 1) * quarter)
        conv(rows.start, rows.stop)
        cp_ref[rows, :] = _mm(cv_ref[rows, :], wco_ref[...])
    for c in range(0, D, _MERGE_COLS):
        cols = slice(c, c + _MERGE_COLS)
        mb_ref[:, cols] = (m_ref[:, cols] + gate(1, cols) * cp_ref[:, cols]).astype(BF16)
    o = _mm(mb_ref[...], wout_ref[...])
    if slab:
        _load_interleaved(x_ref, slab[0])
        x = _read_slabs(slab[0])
    else:
        x = x_ref[...]
    o_ref[...] = x + gt_ref[...] * _rms(o, gpost_ref[...])


def _merge_call(x, h, mods, l, mod_row, gpost, wg, attn, u, cw, cb, lg, lb, rnn, wao, wco, wro, wout, *, tm):
    time_order = x.ndim == 4
    B, D = x.shape[0], x.shape[-1]
    T = attn.shape[1]
    tok = lambda w_: pl.BlockSpec((None, tm, w_), lambda b, t: (b, t, 0))
    u_prev, u_next = _halo_specs(tm, _CONV_HALO, D_CONV, T)
    return pl.pallas_call(
        _merge_kernel,
        grid=(B, T // tm),
        in_specs=[_time_order_spec(tm, D) if time_order else tok(D), tok(D),
                  _mod_spec(l, 2, mod_row)(D), _layer_spec(gpost, l),
                  _layer_spec(wg, l), tok(D_Q),
                  u_prev, tok(D_CONV), u_next, _layer_spec(cw, l), _layer_spec(cb, l),
                  _layer_spec(lg, l), _layer_spec(lb, l), tok(D_RNN),
                  _layer_spec(wao, l), _layer_spec(wco, l), _layer_spec(wro, l), _layer_spec(wout, l)],
        out_specs=tok(D),
        out_shape=jax.ShapeDtypeStruct((B, T, D), F32),
        scratch_shapes=[pltpu.VMEM((tm + 2 * _CONV_HALO, D_CONV), F32),
                        pltpu.VMEM((CONV_K * SEG, D_CONV), F32), pltpu.VMEM((tm, D_CONV), BF16),
                        pltpu.VMEM((tm, D), F32), pltpu.VMEM((tm, D), F32), pltpu.VMEM((tm, D), BF16)]
        + ([_slab_scratch(tm, D)] if time_order else []),
        compiler_params=_params("arbitrary", "arbitrary"),
        name="merge",
    )(x, h, mods, gpost, wg, attn, u, u, u, cw, cb, lg, lb, rnn, wao, wco, wro, wout)


_FFN_CHUNK = 3072


def _ffn_kernel(xp_ref, x_ref, xn_ref, sh_ref, sc_ref, gt_ref, gpre_ref, gpost_ref, up_ref,
                dw_ref, db_ref, down_ref, o_ref, hp_ref, hn_ref, *slab):
    tm = x_ref.shape[0]
    t = pl.program_id(1)
    mod = lambda v: _modulated(v, gpre_ref[...], sh_ref[...], sc_ref[...])
    hp_ref[...] = mod(xp_ref[...])
    hn_ref[...] = mod(xn_ref[...])

    @pl.when(t == 0)
    def _():
        hp_ref[...] = _seam_prev(hp_ref[...])

    @pl.when(t == pl.num_programs(1) - 1)
    def _():
        hn_ref[...] = _seam_next(hn_ref[...])

    h = jnp.concatenate([hp_ref[...], mod(x_ref[...]), hn_ref[...]], axis=0).astype(BF16)

    acc = jnp.zeros(x_ref.shape, F32)
    for c in range(D_FF // _FFN_CHUNK):
        cg = slice(c * _FFN_CHUNK, (c + 1) * _FFN_CHUNK)
        cv = slice(D_FF + c * _FFN_CHUNK, D_FF + (c + 1) * _FFN_CHUNK)
        ug = _mm(h, up_ref[:, cg])
        uv = _mm(h, up_ref[:, cv])
        yg = db_ref[:, cg]
        yv = db_ref[:, cv]
        for j in range(3):
            rows = slice(j * SEG, j * SEG + tm)
            yg = yg + ug[rows, :] * dw_ref[j:j + 1, cg]
            yv = yv + uv[rows, :] * dw_ref[j:j + 1, cv]
        act = (jax.nn.gelu(yg) * yv).astype(BF16)
        acc = acc + _mm(act, down_ref[cg, :])
    y = x_ref[...] + gt_ref[...] * _rms(acc, gpost_ref[...])
    if slab:
        _store_time_order(y, o_ref, slab[0])
    else:
        o_ref[...] = y


def _ffn_call(x, mods, l, mod_row, gpre, gpost, up, dw, db, down, *, tm, time_order_out=False):
    B, T, D = x.shape
    prev, nxt = _halo_specs(tm, SEG, D, T)
    if time_order_out:
        out_spec = _time_order_spec(tm, D)
        out_shape = jax.ShapeDtypeStruct((B, SEG, T // SEG, D), F32)
    else:
        out_spec = pl.BlockSpec((None, tm, D), lambda b, t: (b, t, 0))
        out_shape = jax.ShapeDtypeStruct((B, T, D), F32)
    return pl.pallas_call(
        _ffn_kernel,
        grid=(B, T // tm),
        in_specs=[prev, pl.BlockSpec((None, tm, D), lambda b, t: (b, t, 0)), nxt,
                  _mod_spec(l, 3, mod_row)(D), _mod_spec(l, 4, mod_row)(D), _mod_spec(l, 5, mod_row)(D),
                  _layer_spec(gpre, l), _layer_spec(gpost, l), _layer_spec(up, l),
                  _layer_spec(dw, l), _layer_spec(db, l), _layer_spec(down, l)],
        out_specs=out_spec,
        out_shape=out_shape,
        scratch_shapes=[pltpu.VMEM((SEG, D), F32), pltpu.VMEM((SEG, D), F32)]
        + ([_slab_scratch(tm, D)] if time_order_out else []),
        compiler_params=_params("arbitrary", "arbitrary"),
        name="conv_ffn",
    )(x, x, x, mods, mods, mods, gpre, gpost, up, dw, db, down)


def _seg_interleave(a):
    *lead, T, W = a.shape
    return a.reshape(*lead, SEG, T // SEG, W).swapaxes(-3, -2).reshape(*lead, T, W)


def _rope_tables(n):
    rows = n // GRID_W
    row = jnp.repeat(jnp.arange(rows), GRID_W).astype(F32)
    col = jnp.tile(jnp.arange(GRID_W), rows).astype(F32)
    n_freq = HEAD_DIM // 4
    freq = ROPE_THETA ** (-jnp.arange(n_freq, dtype=F32) / n_freq)
    ang = jnp.concatenate([row[:, None] * freq, col[:, None] * freq], axis=-1)
    cos, sin = jnp.cos(ang), jnp.sin(ang)
    cos_t = jnp.tile(jnp.concatenate([cos, cos], axis=-1), (1, N_Q_HEADS))
    sin_t = jnp.tile(jnp.concatenate([-sin, sin], axis=-1), (1, N_Q_HEADS))
    return cos_t, sin_t


def _tile_plan(n_tokens):
    return min(n_tokens, 512), min(n_tokens, 1024), min(n_tokens, 512)


def _block_diag(w):
    *lead, nb, bi, bj = w.shape
    eye = jnp.eye(nb, dtype=w.dtype)
    return jnp.einsum('...hij,hg->...higj', w, eye).reshape(*lead, nb * bi, nb * bj)


def kernel(x, c, ctx, c_ctx, w_mod, b_mod, norm_pre_mix, norm_post_mix, norm_pre_ffn, norm_post_ffn,
           w_in, q_norm, k_norm, w_attn_out, conv_dw, conv_dw_b, conv_ln_g, conv_ln_b, w_conv_out,
           rnn_conv_w, rnn_conv_b, rnn_wa, rnn_ba, rnn_wx, rnn_bx, rnn_lambda, w_rnn_out, w_out,
           ffn_up, ffn_dw, ffn_dw_b, ffn_down):
    B, N, D = x.shape
    M = ctx.shape[1]
    L = w_mod.shape[0]
    tm_lat, tq_lat, chunk_lat = _tile_plan(N)
    tm_ctx, _, chunk_ctx = _tile_plan(M)

    mod_rows = 2 * SUBLANES
    cc = jnp.concatenate([c, c_ctx[None, :], jnp.zeros((mod_rows - B - 1, D), F32)], axis=0)
    mods = _mod_call(cc, w_mod, b_mod[:, None, :]).reshape(L, mod_rows, N_MOD, 1, D)

    cos_t, sin_t = (_seg_interleave(t) for t in _rope_tables(N))
    head_id = jnp.arange(D_Q) // HEAD_DIM
    bd = (head_id[:, None] == head_id[None, :]).astype(BF16)
    h0 = jnp.zeros((B, 2, SEG, D_RNN), F32)

    vec = lambda a: a[:, None, :]
    w_proj = w_gate = w_in.astype(BF16)
    qg = vec(jnp.tile(q_norm, (1, N_Q_HEADS)))
    kg = vec(jnp.tile(k_norm, (1, N_KV_HEADS)))
    rnn_w = (0.5 * jnp.concatenate([_block_diag(rnn_wa), _block_diag(rnn_wx)], axis=-1)).astype(BF16)
    rnn_b = 0.5 * jnp.concatenate([rnn_ba, rnn_bx], axis=-1)[:, :, None, :]
    rnn_cb = rnn_conv_b[:, :, None, :]
    lam = rnn_lambda[:, :, None, :]
    wao, wco, wro, wout = (w_attn_out.astype(BF16), w_conv_out.astype(BF16), w_rnn_out.astype(BF16),
                           w_out.astype(BF16))
    up, down = ffn_up.astype(BF16), ffn_down.astype(BF16)
    gpre_mix, gpost_mix, gpre_ffn, gpost_ffn = (vec(norm_pre_mix), vec(norm_post_mix), vec(norm_pre_ffn),
                                                vec(norm_post_ffn))
    cdb, clg, clb, fdb = vec(conv_dw_b), vec(conv_ln_g), vec(conv_ln_b), vec(ffn_dw_b)

    x = x.reshape(B, SEG, N // SEG, D)
    ctx = ctx.reshape(B, SEG, M // SEG, D)
    for l in range(L):
        need_ctx = l < L - 1
        ctx_out = _inproj_call(ctx, mods, l, B, gpre_mix, w_proj, qg, kg, bd, None, None, tm=tm_ctx,
                               keys_only=not need_ctx)
        if need_ctx:
            hcx, qc, kc, vc, uc, zxc, zrc = ctx_out
            rnn_c, hc_last = _rnn_call(zxc, zrc, h0, l, rnn_conv_w, rnn_cb, rnn_w, rnn_b, lam, chunk=chunk_ctx)
        else:
            kc, vc, zxc = ctx_out
            hc_last = _rnn_call(zxc, None, h0, l, rnn_conv_w, rnn_cb, rnn_w, rnn_b, lam, chunk=chunk_ctx)

        hx, q, k, v, u, zx, zr = _inproj_call(x, mods, l, None, gpre_mix, w_proj, qg, kg, bd, cos_t, sin_t,
                                              tm=tm_lat)
        attn = _attn_call(q, [(kc, vc), (k, v)], tq=tq_lat)
        rnn, _ = _rnn_call(zx, zr, hc_last, l, rnn_conv_w, rnn_cb, rnn_w, rnn_b, lam, chunk=chunk_lat)
        x = _merge_call(x, hx, mods, l, None, gpost_mix, w_gate, attn, u, conv_dw, cdb, clg, clb,
                        rnn, wao, wco, wro, wout, tm=tm_lat)
        x = _ffn_call(x, mods, l, None, gpre_ffn, gpost_ffn, up, ffn_dw, fdb, down, tm=tm_lat,
                      time_order_out=not need_ctx)

        if need_ctx:
            attn_c = _attn_call(qc, [(kc, vc)], tq=tm_ctx)
            ctx = _merge_call(ctx, hcx, mods, l, B, gpost_mix, w_gate, attn_c, uc, conv_dw, cdb, clg, clb,
                              rnn_c, wao, wco, wro, wout, tm=tm_ctx)
            ctx = _ffn_call(ctx, mods, l, B, gpre_ffn, gpost_ffn, up, ffn_dw, fdb, down, tm=tm_ctx)
    return x.reshape(B, N, D)
```

```python
import functools
import math

import jax
import jax.numpy as jnp
from jax import lax
from jax.experimental import pallas as pl
from jax.experimental.pallas import tpu as pltpu

F32 = jnp.float32
BF16 = jnp.bfloat16

EPS = 1e-6
HEAD_DIM = 64
N_Q_HEADS = 8
N_KV_HEADS = 2
Q_GROUP = N_Q_HEADS // N_KV_HEADS
D_Q = N_Q_HEADS * HEAD_DIM
D_KV = N_KV_HEADS * HEAD_DIM
D_CONV = 512
CONV_K = 31
D_RNN = 512
RNN_CONV_K = 4
LRU_C = 8.0
D_FF = 3072
GRID_W = 64
ROPE_THETA = 10000.0
N_MOD = 6

LANES = 128
SUBLANES = 8
SEG = SUBLANES
VMEM_LIMIT = 56 * 1024 * 1024

_OFF_Q = 0
_OFF_KV = D_Q
_OFF_CONV = _OFF_KV + 2 * D_KV
_OFF_ZX = _OFF_CONV + 2 * D_CONV
_OFF_ZR = _OFF_ZX + D_RNN
_OFF_GATE = _OFF_ZR + D_RNN

_Q_SCALE = HEAD_DIM ** -0.5 * math.log2(math.e)


def _const_spec(shape):
    zeros = (0,) * len(shape)
    return pl.BlockSpec(tuple(shape), lambda *_: zeros, pipeline_mode=pl.Buffered(1))


def _layer_spec(arr, l):
    tail = (0,) * (arr.ndim - 1)
    return pl.BlockSpec((None,) + tuple(arr.shape[1:]), lambda *_: (l,) + tail,
                        pipeline_mode=pl.Buffered(1))


def _mod_spec(l, k, row):
    def index(b, *_):
        return (l, b if row is None else row, k, 0, 0)
    return lambda d: pl.BlockSpec((None, None, None, 1, d), index)


def _params(*sem):
    return pltpu.CompilerParams(dimension_semantics=sem, vmem_limit_bytes=VMEM_LIMIT)


def _rms(x, g):
    return x * lax.rsqrt(jnp.mean(x * x, axis=-1, keepdims=True) + EPS) * g


def _mm(a, b):
    return jnp.dot(a, b, preferred_element_type=F32)


def _modulated(x, g, shift, scale):
    return _rms(x, g) * (1.0 + scale) + shift


def _seam_prev(a):
    sub = lax.broadcasted_iota(jnp.int32, a.shape, 0) % SEG
    return jnp.where(sub == 0, 0.0, pltpu.roll(a, 1, 0))


def _seam_next(a):
    sub = lax.broadcasted_iota(jnp.int32, a.shape, 0) % SEG
    return jnp.where(sub == SEG - 1, 0.0, pltpu.roll(a, a.shape[0] - 1, 0))


def _time_order_spec(tm, width):
    return pl.BlockSpec((None, SEG, tm // SEG, width), lambda b, t: (b, 0, t, 0))


def _slab_scratch(tm, width):
    return pltpu.VMEM((width // LANES, tm, LANES), F32)


def _load_interleaved(x_ref, slab_ref):
    seg, steps, width = x_ref.shape
    for j in range(width // LANES):
        for s in range(seg):
            slab_ref[j, pl.ds(s, steps, stride=seg), :] = x_ref[s, :, j * LANES:(j + 1) * LANES]


def _read_slabs(slab_ref):
    return jnp.concatenate([slab_ref[j] for j in range(slab_ref.shape[0])], axis=1)


def _store_time_order(y, o_ref, slab_ref):
    seg, steps, width = o_ref.shape
    for j in range(width // LANES):
        slab_ref[j] = y[:, j * LANES:(j + 1) * LANES]
    for j in range(width // LANES):
        for s in range(seg):
            o_ref[s, :, j * LANES:(j + 1) * LANES] = slab_ref[j, pl.ds(s, steps, stride=seg), :]


def _halo_specs(tm, halo, width, n_rows):
    per = tm // halo
    n_blocks = n_rows // halo
    prev = pl.BlockSpec((None, halo, width), lambda b, t: (b, (t * per + n_blocks - 1) % n_blocks, 0))
    nxt = pl.BlockSpec((None, halo, width), lambda b, t: (b, ((t + 1) * per) % n_blocks, 0))
    return prev, nxt


def _mod_kernel(c_ref, w_ref, b_ref, o_ref):
    c = c_ref[...]
    a = (c * jax.nn.sigmoid(c)).astype(BF16)
    o_ref[...] = _mm(a, w_ref[...].astype(BF16)) + b_ref[...]


def _mod_call(cc, w_mod, b_mod):
    L, D, D6 = w_mod.shape
    R = cc.shape[0]
    bn = D6 // 4
    return pl.pallas_call(
        _mod_kernel,
        grid=(L, D6 // bn),
        in_specs=[
            pl.BlockSpec((R, D), lambda l, j: (0, 0)),
            pl.BlockSpec((None, D, bn), lambda l, j: (l, 0, j)),
            pl.BlockSpec((None, 1, bn), lambda l, j: (l, 0, j)),
        ],
        out_specs=pl.BlockSpec((None, R, bn), lambda l, j: (l, 0, j)),
        out_shape=jax.ShapeDtypeStruct((L, R, D6), F32),
        compiler_params=_params("arbitrary", "arbitrary"),
        name="adaln_mod",
    )(cc, w_mod, b_mod)


def _head_rms(z, gain, bd):
    ms = _mm((z * z).astype(BF16), bd) * (1.0 / HEAD_DIM)
    return z * lax.rsqrt(ms + EPS) * gain


def _rope(x, c, s):
    rows, width = x.shape
    lane = lax.broadcasted_iota(jnp.int32, (rows, LANES), 1)
    first_half = (lane & (HEAD_DIM // 2)) == 0
    cols = []
    for j in range(width // LANES):
        sl = slice(j * LANES, (j + 1) * LANES)
        xc = x[:, sl]
        partner = jnp.where(first_half,
                            pltpu.roll(xc, LANES - HEAD_DIM // 2, 1),
                            pltpu.roll(xc, HEAD_DIM // 2, 1))
        cols.append(xc * c[:, sl] + partner * s[:, sl])
    return jnp.concatenate(cols, axis=1) if len(cols) > 1 else cols[0]


def _inproj_kernel(*refs, rope, time_order, keys_only):
    x_ref, sh_ref, sc_ref, g_ref, w_ref, qg_ref, kg_ref, bd_ref = refs[:8]
    refs = refs[8:]
    if rope:
        cos_ref, sin_ref = refs[:2]
        refs = refs[2:]
    n_out = 3 if keys_only else 7
    outs, scratch = refs[:n_out], refs[n_out:]
    if time_order:
        _load_interleaved(x_ref, scratch[0])
        x = _read_slabs(scratch[0])
    else:
        x = x_ref[...]
    h = _modulated(x, g_ref[...], sh_ref[...], sc_ref[...]).astype(BF16)
    if keys_only:
        k_ref, v_ref, zx_ref = outs
    else:
        h_ref, q_ref, k_ref, v_ref, u_ref, zx_ref, zr_ref = outs
        h_ref[...] = h
        zq = _mm(h, w_ref[:, _OFF_Q:_OFF_Q + D_Q])
        qn = _head_rms(zq, qg_ref[...], bd_ref[...])
        if rope:
            qn = _rope(qn, cos_ref[...], sin_ref[...])
        q_ref[...] = (qn * _Q_SCALE).astype(BF16)

    zkv = _mm(h, w_ref[:, _OFF_KV:_OFF_KV + 2 * D_KV])
    kn = _head_rms(zkv[:, :D_KV], kg_ref[...], bd_ref[:D_KV, :D_KV])
    if rope:
        kn = _rope(kn, cos_ref[:, :D_KV], sin_ref[:, :D_KV])
    k_ref[...] = kn
    v_ref[...] = zkv[:, D_KV:]

    if not keys_only:
        zc = _mm(h, w_ref[:, _OFF_CONV:_OFF_CONV + 2 * D_CONV])
        u_ref[...] = zc[:, :D_CONV] * jax.nn.sigmoid(zc[:, D_CONV:])

    zx_ref[...] = _mm(h, w_ref[:, _OFF_ZX:_OFF_ZX + D_RNN])
    if not keys_only:
        zr_ref[...] = _mm(h, w_ref[:, _OFF_ZR:_OFF_ZR + D_RNN])


def _inproj_call(x, mods, l, mod_row, gain, w, qg, kg, bd, cos, sin, *, tm, keys_only=False):
    time_order = x.ndim == 4
    B, D = x.shape[0], x.shape[-1]
    T = x.shape[1] * x.shape[2] if time_order else x.shape[1]
    rope = cos is not None
    tok = lambda w_: pl.BlockSpec((None, tm, w_), lambda b, t: (b, t, 0))
    x_spec = _time_order_spec(tm, D) if time_order else tok(D)
    in_specs = [x_spec, _mod_spec(l, 0, mod_row)(D), _mod_spec(l, 1, mod_row)(D),
                _layer_spec(gain, l), _layer_spec(w, l), _layer_spec(qg, l), _layer_spec(kg, l),
                _const_spec(bd.shape)]
    args = [x, mods, mods, gain, w, qg, kg, bd]
    if rope:
        tab = pl.BlockSpec((tm, D_Q), lambda b, t: (t, 0))
        in_specs += [tab, tab]
        args += [cos, sin]
    out_shape = [jax.ShapeDtypeStruct((B, T, D), BF16),
                 jax.ShapeDtypeStruct((B, T, D_Q), BF16),
                 jax.ShapeDtypeStruct((B, T, D_KV), F32),
                 jax.ShapeDtypeStruct((B, T, D_KV), F32),
                 jax.ShapeDtypeStruct((B, T, D_CONV), F32),
                 jax.ShapeDtypeStruct((B, T, D_RNN), F32),
                 jax.ShapeDtypeStruct((B, T, D_RNN), F32)]
    out_specs = [tok(D), tok(D_Q), tok(D_KV), tok(D_KV), tok(D_CONV), tok(D_RNN), tok(D_RNN)]
    if keys_only:
        keep = (2, 3, 5)
        out_shape = [out_shape[i] for i in keep]
        out_specs = [out_specs[i] for i in keep]
    return pl.pallas_call(
        functools.partial(_inproj_kernel, rope=rope, time_order=time_order, keys_only=keys_only),
        grid=(B, T // tm),
        in_specs=in_specs, out_specs=out_specs, out_shape=out_shape,
        scratch_shapes=[_slab_scratch(tm, D)] if time_order else [],
        compiler_params=_params("arbitrary", "arbitrary"),
        name="inproj_rope" if rope else "inproj",
    )(*args)


_VT_ROWS = HEAD_DIM + 2 * SUBLANES
_ATTN_KEYS = 256
_ATTN_SAFE_BOUND = 60.0


def _attn_kernel(*refs, seg_lens):
    nseg = len(seg_lens)
    q_ref = refs[0]
    kv_refs = refs[1:1 + 2 * nseg]
    o_ref = refs[1 + 2 * nseg]
    kq_ref, vt_ref, kn_ref = refs[2 + 2 * nseg:]
    m_tot = sum(seg_lens)
    g = pl.program_id(1)

    @pl.when(pl.program_id(2) == 0)
    def _():
        off = 0
        k_norm2 = jnp.zeros((1, 1), F32)
        for i, ms in enumerate(seg_lens):
            lane_half = lax.broadcasted_iota(jnp.int32, (ms, LANES), 1) // HEAD_DIM
            k = kv_refs[2 * i][...]
            kd = jnp.where(lane_half == g, k, pltpu.roll(k, HEAD_DIM, 1)).astype(BF16)
            kq_ref[off:off + ms, :] = jnp.concatenate([kd, kd], axis=1)
            own = jnp.where(lane_half == g, k, 0.0)
            k_norm2 = jnp.maximum(k_norm2, jnp.max(jnp.sum(own * own, axis=1, keepdims=True),
                                                   axis=0, keepdims=True))
            v = kv_refs[2 * i + 1][...]
            vd = jnp.where(lane_half == g, v, pltpu.roll(v, HEAD_DIM, 1))
            vt_ref[0:HEAD_DIM, off:off + ms] = vd.T[0:HEAD_DIM, :].astype(BF16)
            off += ms
        vt_ref[HEAD_DIM:, :] = jnp.ones((_VT_ROWS - HEAD_DIM, m_tot), BF16)
        kn_ref[...] = jnp.broadcast_to(k_norm2, kn_ref.shape)

    q = q_ref[...]
    head_lane = lax.broadcasted_iota(jnp.int32, q.shape, 1) // HEAD_DIM
    q_heads = [jnp.where(head_lane == h, q, jnp.zeros_like(q)) for h in range(Q_GROUP)]

    def scores(h):
        return lax.dot_general(kq_ref[...], q_heads[h], (((1,), (1,)), ((), ())),
                               preferred_element_type=F32)

    def finish(outs):
        for c in range(Q_GROUP // 2):
            pair = jnp.concatenate([outs[2 * c], outs[2 * c + 1]], axis=0)
            o_ref[:, c * LANES:(c + 1) * LANES] = pair.T.astype(BF16)

    qf = q.astype(F32)
    row_id = lax.broadcasted_iota(jnp.int32, (2 * SUBLANES, q.shape[1]), 0)
    lane_id = lax.broadcasted_iota(jnp.int32, (2 * SUBLANES, q.shape[1]), 1) // HEAD_DIM
    head_rows = jnp.where(row_id == lane_id, 1.0, 0.0).astype(BF16)
    q_norm2 = lax.dot_general(head_rows, (qf * qf).astype(BF16), (((1,), (1,)), ((), ())),
                              preferred_element_type=F32)
    bound = jnp.sqrt(q_norm2 * kn_ref[0:1, 0:1] * (1.0 + 2.0 ** -7))
    safe = jnp.max(bound) <= _ATTN_SAFE_BOUND

    @pl.when(safe)
    def _():
        outs = []
        st_next = scores(0)
        for h in range(Q_GROUP):
            st = st_next
            if h + 1 < Q_GROUP:
                st_next = scores(h + 1)
            p = jnp.exp2(st - bound[h:h + 1, :]).astype(BF16)
            acc = _mm(vt_ref[...], p)
            outs.append(acc[0:HEAD_DIM, :] * (1.0 / acc[HEAD_DIM:HEAD_DIM + 1, :]))
        finish(outs)

    @pl.when(jnp.logical_not(safe))
    def _():
        keys = min(_ATTN_KEYS, m_tot)
        n_chunks = m_tot // keys
        outs = []
        st_next = scores(0)
        for h in range(Q_GROUP):
            st_all = st_next
            if h + 1 < Q_GROUP:
                st_next = scores(h + 1)
            for c in range(n_chunks):
                rows = slice(c * keys, (c + 1) * keys)
                st = st_all[rows, :]
                vt = vt_ref[:, rows]
                m_chunk = jnp.max(st, axis=0, keepdims=True)
                if c == 0:
                    m_run = m_chunk
                    acc = _mm(vt, jnp.exp2(st - m_run).astype(BF16))
                else:
                    m_new = jnp.maximum(m_run, m_chunk)
                    acc = acc * jnp.exp2(m_run - m_new) + _mm(vt, jnp.exp2(st - m_new).astype(BF16))
                    m_run = m_new
            outs.append(acc[0:HEAD_DIM, :] * (1.0 / acc[HEAD_DIM:HEAD_DIM + 1, :]))
        finish(outs)


def _attn_call(q, segs, *, tq):
    B, T, _ = q.shape
    seg_lens = tuple(k.shape[1] for k, _ in segs)
    m_tot = sum(seg_lens)
    gw = Q_GROUP * HEAD_DIM
    in_specs = [pl.BlockSpec((None, tq, gw), lambda b, g, t: (b, t, g))]
    args = [q]
    for k, v in segs:
        spec = pl.BlockSpec((None, k.shape[1], D_KV), lambda b, g, t: (b, 0, 0))
        in_specs += [spec, spec]
        args += [k, v]
    return pl.pallas_call(
        functools.partial(_attn_kernel, seg_lens=seg_lens),
        grid=(B, N_KV_HEADS, T // tq),
        in_specs=in_specs,
        out_specs=pl.BlockSpec((None, tq, gw), lambda b, g, t: (b, t, g)),
        out_shape=jax.ShapeDtypeStruct((B, T, D_Q), BF16),
        scratch_shapes=[pltpu.VMEM((m_tot, gw), BF16),
                        pltpu.VMEM((_VT_ROWS, m_tot), BF16),
                        pltpu.VMEM((SUBLANES, LANES), F32)],
        compiler_params=_params("arbitrary", "arbitrary", "arbitrary"),
        name="gqa_attention",
    )(*args)


_CONV_HALO = 16 * SEG
_CONV_ROWS = 32


def _conv_fill(up_ref, u_ref, un_ref, w_ref, pad_ref, wb_ref):
    tm = u_ref.shape[0]
    t = pl.program_id(1)
    for j in range(CONV_K):
        wb_ref[j * SEG:(j + 1) * SEG, :] = jnp.broadcast_to(w_ref[j:j + 1, :], (SEG, D_CONV))
    pad_ref[0:_CONV_HALO, :] = up_ref[...]
    pad_ref[_CONV_HALO:_CONV_HALO + tm, :] = u_ref[...]
    pad_ref[_CONV_HALO + tm:, :] = un_ref[...]

    @pl.when(t == 0)
    def _():
        pad_ref[0:_CONV_HALO, :] = _seam_prev(up_ref[...])

    @pl.when(t == pl.num_programs(1) - 1)
    def _():
        pad_ref[_CONV_HALO + tm:, :] = _seam_next(un_ref[...])


def _conv_rows(pad_ref, wb_ref, b_ref, lg_ref, lb_ref, o_ref, row_lo, row_hi):
    first = _CONV_HALO // SEG - CONV_K // 2
    for r in range(row_lo, row_hi, _CONV_ROWS):
        acc = jnp.zeros((_CONV_ROWS, D_CONV), F32)
        for j in range(CONV_K):
            lo = r + (first + j) * SEG
            w_tile = wb_ref[j * SEG:(j + 1) * SEG, :]
            acc = acc + pad_ref[lo:lo + _CONV_ROWS, :] * jnp.concatenate([w_tile] * (_CONV_ROWS // SEG), axis=0)
        acc = acc + b_ref[...]
        mu = jnp.mean(acc, axis=-1, keepdims=True)
        cen = acc - mu
        var = jnp.mean(cen * cen, axis=-1, keepdims=True)
        y = cen * lax.rsqrt(var + EPS) * lg_ref[...] + lb_ref[...]
        o_ref[r:r + _CONV_ROWS, :] = (y * jax.nn.sigmoid(y)).astype(BF16)


_RNN_HALO = 4 * SEG


_RNN_OUT_ROWS = 2 * SUBLANES
_RNN_TINY = 1e-30


def _rnn_kernel(*refs, chunk, emit):
    if emit:
        (zx_ref, zr_ref, h0_ref, cw_ref, cb_ref, w_ref, b_ref, lam_ref, o_ref, hl_ref,
         xp_ref, a_ref, bb_ref, ac_ref, h_ref, hf_ref) = refs
    else:
        (zx_ref, h0_ref, cw_ref, cb_ref, w_ref, b_ref, lam_ref, hl_ref,
         xp_ref, a_ref, bb_ref, ac_ref, h_ref) = refs
    T = zx_ref.shape[0]
    n_chunks = T // chunk
    steps = chunk // SEG

    xp_ref[0:_RNN_HALO, :] = _seam_prev(zx_ref[T - _RNN_HALO:T, :])
    xp_ref[_RNN_HALO + T:, :] = _seam_next(zx_ref[0:_RNN_HALO, :])

    def copy(i, carry):
        r = pl.multiple_of(i * chunk, chunk)
        xp_ref[pl.ds(_RNN_HALO + r, chunk), :] = zx_ref[pl.ds(r, chunk), :]
        return carry

    lax.fori_loop(0, n_chunks, copy, 0)

    lam = lam_ref[...]
    log_sig = jnp.minimum(lam, 0.0) - jnp.log1p(jnp.exp(-jnp.abs(lam)))
    half_c = (0.5 * LRU_C * math.log2(math.e)) * log_sig
    row = lax.broadcasted_iota(jnp.int32, (SEG, D_RNN), 0)

    def run(reverse):
        def chunk_body(c, carry):
            cc = (n_chunks - 1 - c) if reverse else c
            r0 = pl.multiple_of(cc * chunk, chunk)
            xc = jnp.zeros((chunk, D_RNN), F32)
            for j in range(RNN_CONV_K):
                step = (RNN_CONV_K - 1 - j) if reverse else (j - RNN_CONV_K + 1)
                xc = xc + xp_ref[pl.ds(r0 + _RNN_HALO + step * SEG, chunk), :] * cw_ref[j:j + 1, :]
            xc = xc + cb_ref[...]
            gates = _mm(xc.astype(BF16), w_ref[...]) + b_ref[...]
            tanh_r = jnp.tanh(gates[:, :D_RNN])
            i_gate = 0.5 + 0.5 * jnp.tanh(gates[:, D_RNN:])
            a = jnp.exp2(half_c + half_c * tanh_r)
            a_ref[...] = a
            s = (1.0 - a) * (1.0 + a)
            bb_ref[...] = (s * lax.rsqrt(jnp.maximum(s, _RNN_TINY))) * (i_gate * xc)

            def step_body(k, state):
                a_cum, h_zero = state
                kk = (steps - 1 - k) if reverse else k
                t0 = pl.multiple_of(kk * SEG, SEG)
                at = a_ref[pl.ds(t0, SEG), :]
                h_zero = at * h_zero + bb_ref[pl.ds(t0, SEG), :]
                a_cum = at * a_cum
                h_ref[pl.ds(r0 + t0, SEG), :] = h_zero
                ac_ref[pl.ds(r0 + t0, SEG), :] = a_cum
                return a_cum, h_zero

            return lax.fori_loop(0, steps, step_body, carry, unroll=8)

        a_end, h_end = lax.fori_loop(0, n_chunks, chunk_body,
                                     (jnp.ones((SEG, D_RNN), F32), jnp.zeros((SEG, D_RNN), F32)))

        start = SEG - 1 if reverse else 0
        state_in = jnp.where(row == start, h0_ref[...], 0.0)
        for i in range(1, SEG):
            s = start - i if reverse else start + i
            leaving = h_end + a_end * state_in
            state_in = jnp.where(row == s, pltpu.roll(leaving, SEG - 1 if reverse else 1, 0), state_in)

        state_rows = jnp.concatenate([state_in] * (_RNN_OUT_ROWS // SEG), axis=0)

        def fix(k, carry):
            t0 = pl.multiple_of(k * _RNN_OUT_ROWS, _RNN_OUT_ROWS)
            rows = pl.ds(t0, _RNN_OUT_ROWS)
            h = h_ref[rows, :] + ac_ref[rows, :] * state_rows
            if reverse:
                o_ref[rows, :] = ((hf_ref[rows, :] + h) * jax.nn.gelu(zr_ref[rows, :])).astype(BF16)
            else:
                hf_ref[rows, :] = h
            return carry

        if emit:
            lax.fori_loop(0, T // _RNN_OUT_ROWS, fix, 0, unroll=4)
        edge = slice(0, SEG) if reverse else slice(T - SEG, T)
        last_rows = h_ref[edge, :] + ac_ref[edge, :] * state_in
        final = last_rows[0:1, :] if reverse else last_rows[SEG - 1:SEG, :]
        hl_ref[...] = jnp.broadcast_to(final, (SEG, D_RNN))

    d = pl.program_id(1)
    pl.when(d == 0)(lambda: run(False))
    pl.when(d == 1)(lambda: run(True))


def _rnn_call(zx, zr, h0, l, cw, cb, w, b, lam, *, chunk):
    emit = zr is not None
    B, T, C = zx.shape
    per_dir = lambda a: pl.BlockSpec((None, None) + tuple(a.shape[2:]), lambda b_, d: (l, d, 0, 0))
    state = pl.BlockSpec((None, None, SEG, C), lambda b_, d: (b_, d, 0, 0))
    seq = pl.BlockSpec((None, T, C), lambda b_, d: (b_, 0, 0))
    weights = [per_dir(cw), per_dir(cb), per_dir(w), per_dir(b), per_dir(lam)]
    state_shape = jax.ShapeDtypeStruct((B, 2, SEG, C), F32)
    scratch = [pltpu.VMEM((T + 2 * _RNN_HALO, C), F32),
               pltpu.VMEM((chunk, C), F32), pltpu.VMEM((chunk, C), F32),
               pltpu.VMEM((T, C), F32), pltpu.VMEM((T, C), F32)]
    if emit:
        args = (zx, zr, h0, cw, cb, w, b, lam)
        in_specs = [seq, seq, state] + weights
        out_specs = [seq, state]
        out_shape = [jax.ShapeDtypeStruct((B, T, C), BF16), state_shape]
        scratch = scratch + [pltpu.VMEM((T, C), F32)]
    else:
        args = (zx, h0, cw, cb, w, b, lam)
        in_specs = [seq, state] + weights
        out_specs = state
        out_shape = state_shape
    return pl.pallas_call(
        functools.partial(_rnn_kernel, chunk=chunk, emit=emit),
        grid=(B, 2),
        in_specs=in_specs, out_specs=out_specs, out_shape=out_shape,
        scratch_shapes=scratch,
        compiler_params=_params("arbitrary", "arbitrary"),
        name="rglru",
    )(*args)


_MERGE_COLS = 256


def _merge_kernel(x_ref, h_ref, gt_ref, gpost_ref, wg_ref, at_ref,
                  up_ref, u_ref, un_ref, cw_ref, cb_ref, lg_ref, lb_ref,
                  rn_ref, wao_ref, wco_ref, wro_ref, wout_ref, o_ref, pad_ref, wb_ref, cv_ref, m_ref, cp_ref,
                  mb_ref, *slab):
    D = x_ref.shape[-1]
    _conv_fill(up_ref, u_ref, un_ref, cw_ref, pad_ref, wb_ref)
    tm = u_ref.shape[0]
    conv = functools.partial(_conv_rows, pad_ref, wb_ref, cb_ref, lg_ref, lb_ref, cv_ref)

    def gate(branch, cols):
        first = _OFF_GATE + branch * D
        return jax.nn.sigmoid(_mm(h_ref[...], wg_ref[:, first + cols.start:first + cols.stop]))

    quarter = tm // 4
    half = D // 2
    groups = ((at_ref, wao_ref, 0, 0), (at_ref, wao_ref, 0, half), (rn_ref, wro_ref, 2, 0),
              (rn_ref, wro_ref, 2, half))
    for i, (act_ref, w_ref, branch, col0) in enumerate(groups):
        for c in range(col0, col0 + half, _MERGE_COLS):
            cols = slice(c, c + _MERGE_COLS)
            term = gate(branch, cols) * _mm(act_ref[...], w_ref[:, cols])
            m_ref[:, cols] = term if branch == 0 else m_ref[:, cols] + term
        rows = slice(i * quarter, (i + 1) * quarter)
        conv(rows.start, rows.stop)
        cp_ref[rows, :] = _mm(cv_ref[rows, :], wco_ref[...])
    for c in range(0, D, _MERGE_COLS):
        cols = slice(c, c + _MERGE_COLS)
        mb_ref[:, cols] = (m_ref[:, cols] + gate(1, cols) * cp_ref[:, cols]).astype(BF16)
    o = _mm(mb_ref[...], wout_ref[...])
    if slab:
        _load_interleaved(x_ref, slab[0])
        x = _read_slabs(slab[0])
    else:
        x = x_ref[...]
    o_ref[...] = x + gt_ref[...] * _rms(o, gpost_ref[...])


def _merge_call(x, h, mods, l, mod_row, gpost, wg, attn, u, cw, cb, lg, lb, rnn, wao, wco, wro, wout, *, tm):
    time_order = x.ndim == 4
    B, D = x.shape[0], x.shape[-1]
    T = attn.shape[1]
    tok = lambda w_: pl.BlockSpec((None, tm, w_), lambda b, t: (b, t, 0))
    u_prev, u_next = _halo_specs(tm, _CONV_HALO, D_CONV, T)
    return pl.pallas_call(
        _merge_kernel,
        grid=(B, T // tm),
        in_specs=[_time_order_spec(tm, D) if time_order else tok(D), tok(D),
                  _mod_spec(l, 2, mod_row)(D), _layer_spec(gpost, l),
                  _layer_spec(wg, l), tok(D_Q),
                  u_prev, tok(D_CONV), u_next, _layer_spec(cw, l), _layer_spec(cb, l),
                  _layer_spec(lg, l), _layer_spec(lb, l), tok(D_RNN),
                  _layer_spec(wao, l), _layer_spec(wco, l), _layer_spec(wro, l), _layer_spec(wout, l)],
        out_specs=tok(D),
        out_shape=jax.ShapeDtypeStruct((B, T, D), F32),
        scratch_shapes=[pltpu.VMEM((tm + 2 * _CONV_HALO, D_CONV), F32),
                        pltpu.VMEM((CONV_K * SEG, D_CONV), F32), pltpu.VMEM((tm, D_CONV), BF16),
                        pltpu.VMEM((tm, D), F32), pltpu.VMEM((tm, D), F32), pltpu.VMEM((tm, D), BF16)]
        + ([_slab_scratch(tm, D)] if time_order else []),
        compiler_params=_params("arbitrary", "arbitrary"),
        name="merge",
    )(x, h, mods, gpost, wg, attn, u, u, u, cw, cb, lg, lb, rnn, wao, wco, wro, wout)


_FFN_CHUNK = 3072


def _ffn_kernel(xp_ref, x_ref, xn_ref, sh_ref, sc_ref, gt_ref, gpre_ref, gpost_ref, up_ref,
                dw_ref, db_ref, down_ref, o_ref, hp_ref, hn_ref, *slab):
    tm = x_ref.shape[0]
    t = pl.program_id(1)
    mod = lambda v: _modulated(v, gpre_ref[...], sh_ref[...], sc_ref[...])
    hp_ref[...] = mod(xp_ref[...])
    hn_ref[...] = mod(xn_ref[...])

    @pl.when(t == 0)
    def _():
        hp_ref[...] = _seam_prev(hp_ref[...])

    @pl.when(t == pl.num_programs(1) - 1)
    def _():
        hn_ref[...] = _seam_next(hn_ref[...])

    h = jnp.concatenate([hp_ref[...], mod(x_ref[...]), hn_ref[...]], axis=0).astype(BF16)

    acc = jnp.zeros(x_ref.shape, F32)
    for c in range(D_FF // _FFN_CHUNK):
        cg = slice(c * _FFN_CHUNK, (c + 1) * _FFN_CHUNK)
        cv = slice(D_FF + c * _FFN_CHUNK, D_FF + (c + 1) * _FFN_CHUNK)
        ug = _mm(h, up_ref[:, cg])
        uv = _mm(h, up_ref[:, cv])
        yg = db_ref[:, cg]
        yv = db_ref[:, cv]
        for j in range(3):
            rows = slice(j * SEG, j * SEG + tm)
            yg = yg + ug[rows, :] * dw_ref[j:j + 1, cg]
            yv = yv + uv[rows, :] * dw_ref[j:j + 1, cv]
        act = (jax.nn.gelu(yg) * yv).astype(BF16)
        acc = acc + _mm(act, down_ref[cg, :])
    y = x_ref[...] + gt_ref[...] * _rms(acc, gpost_ref[...])
    if slab:
        _store_time_order(y, o_ref, slab[0])
    else:
        o_ref[...] = y


def _ffn_call(x, mods, l, mod_row, gpre, gpost, up, dw, db, down, *, tm, time_order_out=False):
    B, T, D = x.shape
    prev, nxt = _halo_specs(tm, SEG, D, T)
    if time_order_out:
        out_spec = _time_order_spec(tm, D)
        out_shape = jax.ShapeDtypeStruct((B, SEG, T // SEG, D), F32)
    else:
        out_spec = pl.BlockSpec((None, tm, D), lambda b, t: (b, t, 0))
        out_shape = jax.ShapeDtypeStruct((B, T, D), F32)
    return pl.pallas_call(
        _ffn_kernel,
        grid=(B, T // tm),
        in_specs=[prev, pl.BlockSpec((None, tm, D), lambda b, t: (b, t, 0)), nxt,
                  _mod_spec(l, 3, mod_row)(D), _mod_spec(l, 4, mod_row)(D), _mod_spec(l, 5, mod_row)(D),
                  _layer_spec(gpre, l), _layer_spec(gpost, l), _layer_spec(up, l),
                  _layer_spec(dw, l), _layer_spec(db, l), _layer_spec(down, l)],
        out_specs=out_spec,
        out_shape=out_shape,
        scratch_shapes=[pltpu.VMEM((SEG, D), F32), pltpu.VMEM((SEG, D), F32)]
        + ([_slab_scratch(tm, D)] if time_order_out else []),
        compiler_params=_params("arbitrary", "arbitrary"),
        name="conv_ffn",
    )(x, x, x, mods, mods, mods, gpre, gpost, up, dw, db, down)


def _seg_interleave(a):
    *lead, T, W = a.shape
    return a.reshape(*lead, SEG, T // SEG, W).swapaxes(-3, -2).reshape(*lead, T, W)


def _rope_tables(n):
    rows = n // GRID_W
    row = jnp.repeat(jnp.arange(rows), GRID_W).astype(F32)
    col = jnp.tile(jnp.arange(GRID_W), rows).astype(F32)
    n_freq = HEAD_DIM // 4
    freq = ROPE_THETA ** (-jnp.arange(n_freq, dtype=F32) / n_freq)
    ang = jnp.concatenate([row[:, None] * freq, col[:, None] * freq], axis=-1)
    cos, sin = jnp.cos(ang), jnp.sin(ang)
    cos_t = jnp.tile(jnp.concatenate([cos, cos], axis=-1), (1, N_Q_HEADS))
    sin_t = jnp.tile(jnp.concatenate([-sin, sin], axis=-1), (1, N_Q_HEADS))
    return cos_t, sin_t


def _tile_plan(n_tokens):
    return min(n_tokens, 512), min(n_tokens, 1024), min(n_tokens, 512)


def _block_diag(w):
    *lead, nb, bi, bj = w.shape
    eye = jnp.eye(nb, dtype=w.dtype)
    return jnp.einsum('...hij,hg->...higj', w, eye).reshape(*lead, nb * bi, nb * bj)


def kernel(x, c, ctx, c_ctx, w_mod, b_mod, norm_pre_mix, norm_post_mix, norm_pre_ffn, norm_post_ffn,
           w_in, q_norm, k_norm, w_attn_out, conv_dw, conv_dw_b, conv_ln_g, conv_ln_b, w_conv_out,
           rnn_conv_w, rnn_conv_b, rnn_wa, rnn_ba, rnn_wx, rnn_bx, rnn_lambda, w_rnn_out, w_out,
           ffn_up, ffn_dw, ffn_dw_b, ffn_down):
    B, N, D = x.shape
    M = ctx.shape[1]
    L = w_mod.shape[0]
    tm_lat, tq_lat, chunk_lat = _tile_plan(N)
    tm_ctx, _, chunk_ctx = _tile_plan(M)

    mod_rows = 2 * SUBLANES
    cc = jnp.concatenate([c, c_ctx[None, :], jnp.zeros((mod_rows - B - 1, D), F32)], axis=0)
    mods = _mod_call(cc, w_mod, b_mod[:, None, :]).reshape(L, mod_rows, N_MOD, 1, D)

    cos_t, sin_t = (_seg_interleave(t) for t in _rope_tables(N))
    head_id = jnp.arange(D_Q) // HEAD_DIM
    bd = (head_id[:, None] == head_id[None, :]).astype(BF16)
    h0 = jnp.zeros((B, 2, SEG, D_RNN), F32)

    vec = lambda a: a[:, None, :]
    w_proj = w_gate = w_in.astype(BF16)
    qg = vec(jnp.tile(q_norm, (1, N_Q_HEADS)))
    kg = vec(jnp.tile(k_norm, (1, N_KV_HEADS)))
    rnn_w = (0.5 * jnp.concatenate([_block_diag(rnn_wa), _block_diag(rnn_wx)], axis=-1)).astype(BF16)
    rnn_b = 0.5 * jnp.concatenate([rnn_ba, rnn_bx], axis=-1)[:, :, None, :]
    rnn_cb = rnn_conv_b[:, :, None, :]
    lam = rnn_lambda[:, :, None, :]
    wao, wco, wro, wout = (w_attn_out.astype(BF16), w_conv_out.astype(BF16), w_rnn_out.astype(BF16),
                           w_out.astype(BF16))
    up, down = ffn_up.astype(BF16), ffn_down.astype(BF16)
    gpre_mix, gpost_mix, gpre_ffn, gpost_ffn = (vec(norm_pre_mix), vec(norm_post_mix), vec(norm_pre_ffn),
                                                vec(norm_post_ffn))
    cdb, clg, clb, fdb = vec(conv_dw_b), vec(conv_ln_g), vec(conv_ln_b), vec(ffn_dw_b)

    x = x.reshape(B, SEG, N // SEG, D)
    ctx = ctx.reshape(B, SEG, M // SEG, D)
    for l in range(L):
        need_ctx = l < L - 1
        ctx_out = _inproj_call(ctx, mods, l, B, gpre_mix, w_proj, qg, kg, bd, None, None, tm=tm_ctx,
                               keys_only=not need_ctx)
        if need_ctx:
            hcx, qc, kc, vc, uc, zxc, zrc = ctx_out
            rnn_c, hc_last = _rnn_call(zxc, zrc, h0, l, rnn_conv_w, rnn_cb, rnn_w, rnn_b, lam, chunk=chunk_ctx)
        else:
            kc, vc, zxc = ctx_out
            hc_last = _rnn_call(zxc, None, h0, l, rnn_conv_w, rnn_cb, rnn_w, rnn_b, lam, chunk=chunk_ctx)

        hx, q, k, v, u, zx, zr = _inproj_call(x, mods, l, None, gpre_mix, w_proj, qg, kg, bd, cos_t, sin_t,
                                              tm=tm_lat)
        attn = _attn_call(q, [(kc, vc), (k, v)], tq=tq_lat)
        rnn, _ = _rnn_call(zx, zr, hc_last, l, rnn_conv_w, rnn_cb, rnn_w, rnn_b, lam, chunk=chunk_lat)
        x = _merge_call(x, hx, mods, l, None, gpost_mix, w_gate, attn, u, conv_dw, cdb, clg, clb,
                        rnn, wao, wco, wro, wout, tm=tm_lat)
        x = _ffn_call(x, mods, l, None, gpre_ffn, gpost_ffn, up, ffn_dw, fdb, down, tm=tm_lat,
                      time_order_out=not need_ctx)

        if need_ctx:
            attn_c = _attn_call(qc, [(kc, vc)], tq=tm_ctx)
            ctx = _merge_call(ctx, hcx, mods, l, B, gpost_mix, w_gate, attn_c, uc, conv_dw, cdb, clg, clb,
                              rnn_c, wao, wco, wro, wout, tm=tm_ctx)
            ctx = _ffn_call(ctx, mods, l, B, gpre_ffn, gpost_ffn, up, ffn_dw, fdb, down, tm=tm_ctx)
    return x.reshape(B, N, D)
```

```python
import functools
import math

import jax
import jax.numpy as jnp
from jax import lax
from jax.experimental import pallas as pl
from jax.experimental.pallas import tpu as pltpu

F32 = jnp.float32
BF16 = jnp.bfloat16

EPS = 1e-6
HEAD_DIM = 64
N_Q_HEADS = 8
N_KV_HEADS = 2
Q_GROUP = N_Q_HEADS // N_KV_HEADS
D_Q = N_Q_HEADS * HEAD_DIM
D_KV = N_KV_HEADS * HEAD_DIM
D_CONV = 512
CONV_K = 31
D_RNN = 512
RNN_CONV_K = 4
LRU_C = 8.0
D_FF = 3072
GRID_W = 64
ROPE_THETA = 10000.0
N_MOD = 6

LANES = 128
SUBLANES = 8
SEG = SUBLANES
VMEM_LIMIT = 56 * 1024 * 1024

_OFF_Q = 0
_OFF_KV = D_Q
_OFF_CONV = _OFF_KV + 2 * D_KV
_OFF_ZX = _OFF_CONV + 2 * D_CONV
_OFF_ZR = _OFF_ZX + D_RNN
_OFF_GATE = _OFF_ZR + D_RNN

_Q_SCALE = HEAD_DIM ** -0.5 * math.log2(math.e)


def _const_spec(shape):
    zeros = (0,) * len(shape)
    return pl.BlockSpec(tuple(shape), lambda *_: zeros, pipeline_mode=pl.Buffered(1))


def _layer_spec(arr, l):
    tail = (0,) * (arr.ndim - 1)
    return pl.BlockSpec((None,) + tuple(arr.shape[1:]), lambda *_: (l,) + tail,
                        pipeline_mode=pl.Buffered(1))


def _mod_spec(l, k, row):
    def index(b, *_):
        return (l, b if row is None else row, k, 0, 0)
    return lambda d: pl.BlockSpec((None, None, None, 1, d), index)


def _params(*sem):
    return pltpu.CompilerParams(dimension_semantics=sem, vmem_limit_bytes=VMEM_LIMIT)


def _rms(x, g):
    return x * lax.rsqrt(jnp.mean(x * x, axis=-1, keepdims=True) + EPS) * g


def _mm(a, b):
    return jnp.dot(a, b, preferred_element_type=F32)


def _modulated(x, g, shift, scale):
    return _rms(x, g) * (1.0 + scale) + shift


def _seam_prev(a):
    sub = lax.broadcasted_iota(jnp.int32, a.shape, 0) % SEG
    return jnp.where(sub == 0, 0.0, pltpu.roll(a, 1, 0))


def _seam_next(a):
    sub = lax.broadcasted_iota(jnp.int32, a.shape, 0) % SEG
    return jnp.where(sub == SEG - 1, 0.0, pltpu.roll(a, a.shape[0] - 1, 0))


def _time_order_spec(tm, width):
    return pl.BlockSpec((None, SEG, tm // SEG, width), lambda b, t: (b, 0, t, 0))


def _slab_scratch(tm, width):
    return pltpu.VMEM((width // LANES, tm, LANES), F32)


def _load_interleaved(x_ref, slab_ref):
    seg, steps, width = x_ref.shape
    for j in range(width // LANES):
        for s in range(seg):
            slab_ref[j, pl.ds(s, steps, stride=seg), :] = x_ref[s, :, j * LANES:(j + 1) * LANES]


def _read_slabs(slab_ref):
    return jnp.concatenate([slab_ref[j] for j in range(slab_ref.shape[0])], axis=1)


def _store_time_order(y, o_ref, slab_ref):
    seg, steps, width = o_ref.shape
    for j in range(width // LANES):
        slab_ref[j] = y[:, j * LANES:(j + 1) * LANES]
    for j in range(width // LANES):
        for s in range(seg):
            o_ref[s, :, j * LANES:(j + 1) * LANES] = slab_ref[j, pl.ds(s, steps, stride=seg), :]


def _halo_specs(tm, halo, width, n_rows):
    per = tm // halo
    n_blocks = n_rows // halo
    prev = pl.BlockSpec((None, halo, width), lambda b, t: (b, (t * per + n_blocks - 1) % n_blocks, 0))
    nxt = pl.BlockSpec((None, halo, width), lambda b, t: (b, ((t + 1) * per) % n_blocks, 0))
    return prev, nxt


def _mod_kernel(c_ref, w_ref, b_ref, o_ref):
    c = c_ref[...]
    a = (c * jax.nn.sigmoid(c)).astype(BF16)
    o_ref[...] = _mm(a, w_ref[...].astype(BF16)) + b_ref[...]


def _mod_call(cc, w_mod, b_mod):
    L, D, D6 = w_mod.shape
    R = cc.shape[0]
    bn = D6 // 4
    return pl.pallas_call(
        _mod_kernel,
        grid=(L, D6 // bn),
        in_specs=[
            pl.BlockSpec((R, D), lambda l, j: (0, 0)),
            pl.BlockSpec((None, D, bn), lambda l, j: (l, 0, j)),
            pl.BlockSpec((None, 1, bn), lambda l, j: (l, 0, j)),
        ],
        out_specs=pl.BlockSpec((None, R, bn), lambda l, j: (l, 0, j)),
        out_shape=jax.ShapeDtypeStruct((L, R, D6), F32),
        compiler_params=_params("arbitrary", "arbitrary"),
        name="adaln_mod",
    )(cc, w_mod, b_mod)


def _head_rms(z, gain, bd):
    ms = _mm((z * z).astype(BF16), bd) * (1.0 / HEAD_DIM)
    return z * lax.rsqrt(ms + EPS) * gain


def _rope(x, c, s):
    rows, width = x.shape
    lane = lax.broadcasted_iota(jnp.int32, (rows, LANES), 1)
    first_half = (lane & (HEAD_DIM // 2)) == 0
    cols = []
    for j in range(width // LANES):
        sl = slice(j * LANES, (j + 1) * LANES)
        xc = x[:, sl]
        partner = jnp.where(first_half,
                            pltpu.roll(xc, LANES - HEAD_DIM // 2, 1),
                            pltpu.roll(xc, HEAD_DIM // 2, 1))
        cols.append(xc * c[:, sl] + partner * s[:, sl])
    return jnp.concatenate(cols, axis=1) if len(cols) > 1 else cols[0]


def _inproj_kernel(*refs, rope, time_order, keys_only):
    x_ref, sh_ref, sc_ref, g_ref, w_ref, qg_ref, kg_ref, bd_ref = refs[:8]
    refs = refs[8:]
    if rope:
        cos_ref, sin_ref = refs[:2]
        refs = refs[2:]
    n_out = 3 if keys_only else 7
    outs, scratch = refs[:n_out], refs[n_out:]
    if time_order:
        _load_interleaved(x_ref, scratch[0])
        x = _read_slabs(scratch[0])
    else:
        x = x_ref[...]
    h = _modulated(x, g_ref[...], sh_ref[...], sc_ref[...]).astype(BF16)
    if keys_only:
        k_ref, v_ref, zx_ref = outs
    else:
        h_ref, q_ref, k_ref, v_ref, u_ref, zx_ref, zr_ref = outs
        h_ref[...] = h
        zq = _mm(h, w_ref[:, _OFF_Q:_OFF_Q + D_Q])
        qn = _head_rms(zq, qg_ref[...], bd_ref[...])
        if rope:
            qn = _rope(qn, cos_ref[...], sin_ref[...])
        q_ref[...] = (qn * _Q_SCALE).astype(BF16)

    zkv = _mm(h, w_ref[:, _OFF_KV:_OFF_KV + 2 * D_KV])
    kn = _head_rms(zkv[:, :D_KV], kg_ref[...], bd_ref[:D_KV, :D_KV])
    if rope:
        kn = _rope(kn, cos_ref[:, :D_KV], sin_ref[:, :D_KV])
    k_ref[...] = kn
    v_ref[...] = zkv[:, D_KV:]

    if not keys_only:
        zc = _mm(h, w_ref[:, _OFF_CONV:_OFF_CONV + 2 * D_CONV])
        u_ref[...] = zc[:, :D_CONV] * jax.nn.sigmoid(zc[:, D_CONV:])

    zx_ref[...] = _mm(h, w_ref[:, _OFF_ZX:_OFF_ZX + D_RNN])
    if not keys_only:
        zr_ref[...] = _mm(h, w_ref[:, _OFF_ZR:_OFF_ZR + D_RNN])


def _inproj_call(x, mods, l, mod_row, gain, w, qg, kg, bd, cos, sin, *, tm, keys_only=False):
    time_order = x.ndim == 4
    B, D = x.shape[0], x.shape[-1]
    T = x.shape[1] * x.shape[2] if time_order else x.shape[1]
    rope = cos is not None
    tok = lambda w_: pl.BlockSpec((None, tm, w_), lambda b, t: (b, t, 0))
    x_spec = _time_order_spec(tm, D) if time_order else tok(D)
    in_specs = [x_spec, _mod_spec(l, 0, mod_row)(D), _mod_spec(l, 1, mod_row)(D),
                _layer_spec(gain, l), _layer_spec(w, l), _layer_spec(qg, l), _layer_spec(kg, l),
                _const_spec(bd.shape)]
    args = [x, mods, mods, gain, w, qg, kg, bd]
    if rope:
        tab = pl.BlockSpec((tm, D_Q), lambda b, t: (t, 0))
        in_specs += [tab, tab]
        args += [cos, sin]
    out_shape = [jax.ShapeDtypeStruct((B, T, D), BF16),
                 jax.ShapeDtypeStruct((B, T, D_Q), BF16),
                 jax.ShapeDtypeStruct((B, T, D_KV), F32),
                 jax.ShapeDtypeStruct((B, T, D_KV), F32),
                 jax.ShapeDtypeStruct((B, T, D_CONV), F32),
                 jax.ShapeDtypeStruct((B, T, D_RNN), F32),
                 jax.ShapeDtypeStruct((B, T, D_RNN), F32)]
    out_specs = [tok(D), tok(D_Q), tok(D_KV), tok(D_KV), tok(D_CONV), tok(D_RNN), tok(D_RNN)]
    if keys_only:
        keep = (2, 3, 5)
        out_shape = [out_shape[i] for i in keep]
        out_specs = [out_specs[i] for i in keep]
    return pl.pallas_call(
        functools.partial(_inproj_kernel, rope=rope, time_order=time_order, keys_only=keys_only),
        grid=(B, T // tm),
        in_specs=in_specs, out_specs=out_specs, out_shape=out_shape,
        scratch_shapes=[_slab_scratch(tm, D)] if time_order else [],
        compiler_params=_params("arbitrary", "arbitrary"),
        name="inproj_rope" if rope else "inproj",
    )(*args)


_VT_ROWS = HEAD_DIM + 2 * SUBLANES
_ATTN_KEYS = 256
_ATTN_SAFE_BOUND = 60.0


def _attn_kernel(*refs, seg_lens):
    nseg = len(seg_lens)
    q_ref = refs[0]
    kv_refs = refs[1:1 + 2 * nseg]
    o_ref = refs[1 + 2 * nseg]
    kq_ref, vt_ref, kn_ref = refs[2 + 2 * nseg:]
    m_tot = sum(seg_lens)
    g = pl.program_id(1)

    @pl.when(pl.program_id(2) == 0)
    def _():
        off = 0
        k_norm2 = jnp.zeros((1, 1), F32)
        for i, ms in enumerate(seg_lens):
            lane_half = lax.broadcasted_iota(jnp.int32, (ms, LANES), 1) // HEAD_DIM
            k = kv_refs[2 * i][...]
            kd = jnp.where(lane_half == g, k, pltpu.roll(k, HEAD_DIM, 1)).astype(BF16)
            kq_ref[off:off + ms, :] = jnp.concatenate([kd, kd], axis=1)
            own = jnp.where(lane_half == g, k, 0.0)
            k_norm2 = jnp.maximum(k_norm2, jnp.max(jnp.sum(own * own, axis=1, keepdims=True),
                                                   axis=0, keepdims=True))
            v = kv_refs[2 * i + 1][...]
            vd = jnp.where(lane_half == g, v, pltpu.roll(v, HEAD_DIM, 1))
            vt_ref[0:HEAD_DIM, off:off + ms] = vd.T[0:HEAD_DIM, :].astype(BF16)
            off += ms
        vt_ref[HEAD_DIM:, :] = jnp.ones((_VT_ROWS - HEAD_DIM, m_tot), BF16)
        kn_ref[...] = jnp.broadcast_to(k_norm2, kn_ref.shape)

    q = q_ref[...]
    head_lane = lax.broadcasted_iota(jnp.int32, q.shape, 1) // HEAD_DIM
    q_heads = [jnp.where(head_lane == h, q, jnp.zeros_like(q)) for h in range(Q_GROUP)]

    def scores(h):
        return lax.dot_general(kq_ref[...], q_heads[h], (((1,), (1,)), ((), ())),
                               preferred_element_type=F32)

    def finish(outs):
        for c in range(Q_GROUP // 2):
            pair = jnp.concatenate([outs[2 * c], outs[2 * c + 1]], axis=0)
            o_ref[:, c * LANES:(c + 1) * LANES] = pair.T.astype(BF16)

    qf = q.astype(F32)
    row_id = lax.broadcasted_iota(jnp.int32, (2 * SUBLANES, q.shape[1]), 0)
    lane_id = lax.broadcasted_iota(jnp.int32, (2 * SUBLANES, q.shape[1]), 1) // HEAD_DIM
    head_rows = jnp.where(row_id == lane_id, 1.0, 0.0).astype(BF16)
    q_norm2 = lax.dot_general(head_rows, (qf * qf).astype(BF16), (((1,), (1,)), ((), ())),
                              preferred_element_type=F32)
    bound = jnp.sqrt(q_norm2 * kn_ref[0:1, 0:1] * (1.0 + 2.0 ** -7))
    safe = jnp.max(bound) <= _ATTN_SAFE_BOUND

    @pl.when(safe)
    def _():
        outs = []
        st_next = scores(0)
        for h in range(Q_GROUP):
            st = st_next
            if h + 1 < Q_GROUP:
                st_next = scores(h + 1)
            p = jnp.exp2(st - bound[h:h + 1, :]).astype(BF16)
            acc = _mm(vt_ref[...], p)
            outs.append(acc[0:HEAD_DIM, :] * (1.0 / acc[HEAD_DIM:HEAD_DIM + 1, :]))
        finish(outs)

    @pl.when(jnp.logical_not(safe))
    def _():
        keys = min(_ATTN_KEYS, m_tot)
        n_chunks = m_tot // keys
        outs = []
        st_next = scores(0)
        for h in range(Q_GROUP):
            st_all = st_next
            if h + 1 < Q_GROUP:
                st_next = scores(h + 1)
            for c in range(n_chunks):
                rows = slice(c * keys, (c + 1) * keys)
                st = st_all[rows, :]
                vt = vt_ref[:, rows]
                m_chunk = jnp.max(st, axis=0, keepdims=True)
                if c == 0:
                    m_run = m_chunk
                    acc = _mm(vt, jnp.exp2(st - m_run).astype(BF16))
                else:
                    m_new = jnp.maximum(m_run, m_chunk)
                    acc = acc * jnp.exp2(m_run - m_new) + _mm(vt, jnp.exp2(st - m_new).astype(BF16))
                    m_run = m_new
            outs.append(acc[0:HEAD_DIM, :] * (1.0 / acc[HEAD_DIM:HEAD_DIM + 1, :]))
        finish(outs)


def _attn_call(q, segs, *, tq):
    B, T, _ = q.shape
    seg_lens = tuple(k.shape[1] for k, _ in segs)
    m_tot = sum(seg_lens)
    gw = Q_GROUP * HEAD_DIM
    in_specs = [pl.BlockSpec((None, tq, gw), lambda b, g, t: (b, t, g))]
    args = [q]
    for k, v in segs:
        spec = pl.BlockSpec((None, k.shape[1], D_KV), lambda b, g, t: (b, 0, 0))
        in_specs += [spec, spec]
        args += [k, v]
    return pl.pallas_call(
        functools.partial(_attn_kernel, seg_lens=seg_lens),
        grid=(B, N_KV_HEADS, T // tq),
        in_specs=in_specs,
        out_specs=pl.BlockSpec((None, tq, gw), lambda b, g, t: (b, t, g)),
        out_shape=jax.ShapeDtypeStruct((B, T, D_Q), BF16),
        scratch_shapes=[pltpu.VMEM((m_tot, gw), BF16),
                        pltpu.VMEM((_VT_ROWS, m_tot), BF16),
                        pltpu.VMEM((SUBLANES, LANES), F32)],
        compiler_params=_params("arbitrary", "arbitrary", "arbitrary"),
        name="gqa_attention",
    )(*args)


_CONV_HALO = 16 * SEG
_CONV_ROWS = 32


def _conv_fill(up_ref, u_ref, un_ref, w_ref, pad_ref, wb_ref):
    tm = u_ref.shape[0]
    t = pl.program_id(1)
    for j in range(CONV_K):
        wb_ref[j * SEG:(j + 1) * SEG, :] = jnp.broadcast_to(w_ref[j:j + 1, :], (SEG, D_CONV))
    pad_ref[0:_CONV_HALO, :] = up_ref[...]
    pad_ref[_CONV_HALO:_CONV_HALO + tm, :] = u_ref[...]
    pad_ref[_CONV_HALO + tm:, :] = un_ref[...]

    @pl.when(t == 0)
    def _():
        pad_ref[0:_CONV_HALO, :] = _seam_prev(up_ref[...])

    @pl.when(t == pl.num_programs(1) - 1)
    def _():
        pad_ref[_CONV_HALO + tm:, :] = _seam_next(un_ref[...])


def _conv_rows(pad_ref, wb_ref, b_ref, lg_ref, lb_ref, o_ref, row_lo, row_hi):
    first = _CONV_HALO // SEG - CONV_K // 2
    for r in range(row_lo, row_hi, _CONV_ROWS):
        acc = jnp.zeros((_CONV_ROWS, D_CONV), F32)
        for j in range(CONV_K):
            lo = r + (first + j) * SEG
            w_tile = wb_ref[j * SEG:(j + 1) * SEG, :]
            acc = acc + pad_ref[lo:lo + _CONV_ROWS, :] * jnp.concatenate([w_tile] * (_CONV_ROWS // SEG), axis=0)
        acc = acc + b_ref[...]
        mu = jnp.mean(acc, axis=-1, keepdims=True)
        cen = acc - mu
        var = jnp.mean(cen * cen, axis=-1, keepdims=True)
        y = cen * lax.rsqrt(var + EPS) * lg_ref[...] + lb_ref[...]
        o_ref[r:r + _CONV_ROWS, :] = (y * jax.nn.sigmoid(y)).astype(BF16)


_RNN_HALO = 4 * SEG


_RNN_OUT_ROWS = 2 * SUBLANES
_RNN_TINY = 1e-30


def _rnn_kernel(*refs, chunk, emit):
    if emit:
        (zx_ref, zr_ref, h0_ref, cw_ref, cb_ref, w_ref, b_ref, lam_ref, o_ref, hl_ref,
         xp_ref, a_ref, bb_ref, ac_ref, h_ref, hf_ref) = refs
    else:
        (zx_ref, h0_ref, cw_ref, cb_ref, w_ref, b_ref, lam_ref, hl_ref,
         xp_ref, a_ref, bb_ref, ac_ref, h_ref) = refs
    T = zx_ref.shape[0]
    n_chunks = T // chunk
    steps = chunk // SEG

    xp_ref[0:_RNN_HALO, :] = _seam_prev(zx_ref[T - _RNN_HALO:T, :])
    xp_ref[_RNN_HALO + T:, :] = _seam_next(zx_ref[0:_RNN_HALO, :])

    def copy(i, carry):
        r = pl.multiple_of(i * chunk, chunk)
        xp_ref[pl.ds(_RNN_HALO + r, chunk), :] = zx_ref[pl.ds(r, chunk), :]
        return carry

    lax.fori_loop(0, n_chunks, copy, 0)

    lam = lam_ref[...]
    log_sig = jnp.minimum(lam, 0.0) - jnp.log1p(jnp.exp(-jnp.abs(lam)))
    half_c = (0.5 * LRU_C * math.log2(math.e)) * log_sig
    row = lax.broadcasted_iota(jnp.int32, (SEG, D_RNN), 0)

    def run(reverse):
        def chunk_body(c, carry):
            cc = (n_chunks - 1 - c) if reverse else c
            r0 = pl.multiple_of(cc * chunk, chunk)
            xc = jnp.zeros((chunk, D_RNN), F32)
            for j in range(RNN_CONV_K):
                step = (RNN_CONV_K - 1 - j) if reverse else (j - RNN_CONV_K + 1)
                xc = xc + xp_ref[pl.ds(r0 + _RNN_HALO + step * SEG, chunk), :] * cw_ref[j:j + 1, :]
            xc = xc + cb_ref[...]
            gates = _mm(xc.astype(BF16), w_ref[...]) + b_ref[...]
            tanh_r = jnp.tanh(gates[:, :D_RNN])
            i_gate = 0.5 + 0.5 * jnp.tanh(gates[:, D_RNN:])
            a = jnp.exp2(half_c + half_c * tanh_r)
            a_ref[...] = a
            s = (1.0 - a) * (1.0 + a)
            bb_ref[...] = (s * lax.rsqrt(jnp.maximum(s, _RNN_TINY))) * (i_gate * xc)

            def step_body(k, state):
                a_cum, h_zero = state
                kk = (steps - 1 - k) if reverse else k
                t0 = pl.multiple_of(kk * SEG, SEG)
                at = a_ref[pl.ds(t0, SEG), :]
                h_zero = at * h_zero + bb_ref[pl.ds(t0, SEG), :]
                a_cum = at * a_cum
                h_ref[pl.ds(r0 + t0, SEG), :] = h_zero
                ac_ref[pl.ds(r0 + t0, SEG), :] = a_cum
                return a_cum, h_zero

            return lax.fori_loop(0, steps, step_body, carry, unroll=8)

        a_end, h_end = lax.fori_loop(0, n_chunks, chunk_body,
                                     (jnp.ones((SEG, D_RNN), F32), jnp.zeros((SEG, D_RNN), F32)))

        start = SEG - 1 if reverse else 0
        state_in = jnp.where(row == start, h0_ref[...], 0.0)
        for i in range(1, SEG):
            s = start - i if reverse else start + i
            leaving = h_end + a_end * state_in
            state_in = jnp.where(row == s, pltpu.roll(leaving, SEG - 1 if reverse else 1, 0), state_in)

        state_rows = jnp.concatenate([state_in] * (_RNN_OUT_ROWS // SEG), axis=0)

        def fix(k, carry):
            t0 = pl.multiple_of(k * _RNN_OUT_ROWS, _RNN_OUT_ROWS)
            rows = pl.ds(t0, _RNN_OUT_ROWS)
            h = h_ref[rows, :] + ac_ref[rows, :] * state_rows
            if reverse:
                o_ref[rows, :] = ((hf_ref[rows, :] + h) * jax.nn.gelu(zr_ref[rows, :])).astype(BF16)
            else:
                hf_ref[rows, :] = h
            return carry

        if emit:
            lax.fori_loop(0, T // _RNN_OUT_ROWS, fix, 0, unroll=4)
        edge = slice(0, SEG) if reverse else slice(T - SEG, T)
        last_rows = h_ref[edge, :] + ac_ref[edge, :] * state_in
        final = last_rows[0:1, :] if reverse else last_rows[SEG - 1:SEG, :]
        hl_ref[...] = jnp.broadcast_to(final, (SEG, D_RNN))

    d = pl.program_id(1)
    pl.when(d == 0)(lambda: run(False))
    pl.when(d == 1)(lambda: run(True))


def _rnn_call(zx, zr, h0, l, cw, cb, w, b, lam, *, chunk):
    emit = zr is not None
    B, T, C = zx.shape
    per_dir = lambda a: pl.BlockSpec((None, None) + tuple(a.shape[2:]), lambda b_, d: (l, d, 0, 0))
    state = pl.BlockSpec((None, None, SEG, C), lambda b_, d: (b_, d, 0, 0))
    seq = pl.BlockSpec((None, T, C), lambda b_, d: (b_, 0, 0))
    weights = [per_dir(cw), per_dir(cb), per_dir(w), per_dir(b), per_dir(lam)]
    state_shape = jax.ShapeDtypeStruct((B, 2, SEG, C), F32)
    scratch = [pltpu.VMEM((T + 2 * _RNN_HALO, C), F32),
               pltpu.VMEM((chunk, C), F32), pltpu.VMEM((chunk, C), F32),
               pltpu.VMEM((T, C), F32), pltpu.VMEM((T, C), F32)]
    if emit:
        args = (zx, zr, h0, cw, cb, w, b, lam)
        in_specs = [seq, seq, state] + weights
        out_specs = [seq, state]
        out_shape = [jax.ShapeDtypeStruct((B, T, C), BF16), state_shape]
        scratch = scratch + [pltpu.VMEM((T, C), F32)]
    else:
        args = (zx, h0, cw, cb, w, b, lam)
        in_specs = [seq, state] + weights
        out_specs = state
        out_shape = state_shape
    return pl.pallas_call(
        functools.partial(_rnn_kernel, chunk=chunk, emit=emit),
        grid=(B, 2),
        in_specs=in_specs, out_specs=out_specs, out_shape=out_shape,
        scratch_shapes=scratch,
        compiler_params=_params("arbitrary", "arbitrary"),
        name="rglru",
    )(*args)


_MERGE_COLS = 256


def _merge_kernel(x_ref, h_ref, gt_ref, gpost_ref, wg_ref, at_ref,
                  up_ref, u_ref, un_ref, cw_ref, cb_ref, lg_ref, lb_ref,
                  rn_ref, wao_ref, wco_ref, wro_ref, wout_ref, o_ref, pad_ref, wb_ref, cv_ref, m_ref, cp_ref,
                  mb_ref, *slab):
    D = x_ref.shape[-1]
    _conv_fill(up_ref, u_ref, un_ref, cw_ref, pad_ref, wb_ref)
    tm = u_ref.shape[0]
    conv = functools.partial(_conv_rows, pad_ref, wb_ref, cb_ref, lg_ref, lb_ref, cv_ref)

    def gate(branch, cols):
        first = _OFF_GATE + branch * D
        return jax.nn.sigmoid(_mm(h_ref[...], wg_ref[:, first + cols.start:first + cols.stop]))

    quarter = tm // 4
    half = D // 2
    groups = ((at_ref, wao_ref, 0, 0), (at_ref, wao_ref, 0, half), (rn_ref, wro_ref, 2, 0),
              (rn_ref, wro_ref, 2, half))
    for i, (act_ref, w_ref, branch, col0) in enumerate(groups):
        for c in range(col0, col0 + half, _MERGE_COLS):
            cols = slice(c, c + _MERGE_COLS)
            term = gate(branch, cols) * _mm(act_ref[...], w_ref[:, cols])
            m_ref[:, cols] = term if branch == 0 else m_ref[:, cols] + term
        rows = slice(i * quarter, (i + 1) * quarter)
        conv(rows.start, rows.stop)
        cp_ref[rows, :] = _mm(cv_ref[rows, :], wco_ref[...])
    for c in range(0, D, _MERGE_COLS):
        cols = slice(c, c + _MERGE_COLS)
        mb_ref[:, cols] = (m_ref[:, cols] + gate(1, cols) * cp_ref[:, cols]).astype(BF16)
    o = _mm(mb_ref[...], wout_ref[...])
    if slab:
        _load_interleaved(x_ref, slab[0])
        x = _read_slabs(slab[0])
    else:
        x = x_ref[...]
    o_ref[...] = x + gt_ref[...] * _rms(o, gpost_ref[...])


def _merge_call(x, h, mods, l, mod_row, gpost, wg, attn, u, cw, cb, lg, lb, rnn, wao, wco, wro, wout, *, tm):
    time_order = x.ndim == 4
    B, D = x.shape[0], x.shape[-1]
    T = attn.shape[1]
    tok = lambda w_: pl.BlockSpec((None, tm, w_), lambda b, t: (b, t, 0))
    u_prev, u_next = _halo_specs(tm, _CONV_HALO, D_CONV, T)
    return pl.pallas_call(
        _merge_kernel,
        grid=(B, T // tm),
        in_specs=[_time_order_spec(tm, D) if time_order else tok(D), tok(D),
                  _mod_spec(l, 2, mod_row)(D), _layer_spec(gpost, l),
                  _layer_spec(wg, l), tok(D_Q),
                  u_prev, tok(D_CONV), u_next, _layer_spec(cw, l), _layer_spec(cb, l),
                  _layer_spec(lg, l), _layer_spec(lb, l), tok(D_RNN),
                  _layer_spec(wao, l), _layer_spec(wco, l), _layer_spec(wro, l), _layer_spec(wout, l)],
        out_specs=tok(D),
        out_shape=jax.ShapeDtypeStruct((B, T, D), F32),
        scratch_shapes=[pltpu.VMEM((tm + 2 * _CONV_HALO, D_CONV), F32),
                        pltpu.VMEM((CONV_K * SEG, D_CONV), F32), pltpu.VMEM((tm, D_CONV), BF16),
                        pltpu.VMEM((tm, D), F32), pltpu.VMEM((tm, D), F32), pltpu.VMEM((tm, D), BF16)]
        + ([_slab_scratch(tm, D)] if time_order else []),
        compiler_params=_params("arbitrary", "arbitrary"),
        name="merge",
    )(x, h, mods, gpost, wg, attn, u, u, u, cw, cb, lg, lb, rnn, wao, wco, wro, wout)


_FFN_CHUNK = 3072


def _ffn_kernel(xp_ref, x_ref, xn_ref, sh_ref, sc_ref, gt_ref, gpre_ref, gpost_ref, up_ref,
                dw_ref, db_ref, down_ref, o_ref, hp_ref, hn_ref, *slab):
    tm = x_ref.shape[0]
    t = pl.program_id(1)
    mod = lambda v: _modulated(v, gpre_ref[...], sh_ref[...], sc_ref[...])
    hp_ref[...] = mod(xp_ref[...])
    hn_ref[...] = mod(xn_ref[...])

    @pl.when(t == 0)
    def _():
        hp_ref[...] = _seam_prev(hp_ref[...])

    @pl.when(t == pl.num_programs(1) - 1)
    def _():
        hn_ref[...] = _seam_next(hn_ref[...])

    h = jnp.concatenate([hp_ref[...], mod(x_ref[...]), hn_ref[...]], axis=0).astype(BF16)

    acc = jnp.zeros(x_ref.shape, F32)
    for c in range(D_FF // _FFN_CHUNK):
        cg = slice(c * _FFN_CHUNK, (c + 1) * _FFN_CHUNK)
        cv = slice(D_FF + c * _FFN_CHUNK, D_FF + (c + 1) * _FFN_CHUNK)
        ug = _mm(h, up_ref[:, cg])
        uv = _mm(h, up_ref[:, cv])
        yg = db_ref[:, cg]
        yv = db_ref[:, cv]
        for j in range(3):
            rows = slice(j * SEG, j * SEG + tm)
            yg = yg + ug[rows, :] * dw_ref[j:j + 1, cg]
            yv = yv + uv[rows, :] * dw_ref[j:j + 1, cv]
        act = (jax.nn.gelu(yg) * yv).astype(BF16)
        acc = acc + _mm(act, down_ref[cg, :])
    y = x_ref[...] + gt_ref[...] * _rms(acc, gpost_ref[...])
    if slab:
        _store_time_order(y, o_ref, slab[0])
    else:
        o_ref[...] = y


def _ffn_call(x, mods, l, mod_row, gpre, gpost, up, dw, db, down, *, tm, time_order_out=False):
    B, T, D = x.shape
    prev, nxt = _halo_specs(tm, SEG, D, T)
    if time_order_out:
        out_spec = _time_order_spec(tm, D)
        out_shape = jax.ShapeDtypeStruct((B, SEG, T // SEG, D), F32)
    else:
        out_spec = pl.BlockSpec((None, tm, D), lambda b, t: (b, t, 0))
        out_shape = jax.ShapeDtypeStruct((B, T, D), F32)
    return pl.pallas_call(
        _ffn_kernel,
        grid=(B, T // tm),
        in_specs=[prev, pl.BlockSpec((None, tm, D), lambda b, t: (b, t, 0)), nxt,
                  _mod_spec(l, 3, mod_row)(D), _mod_spec(l, 4, mod_row)(D), _mod_spec(l, 5, mod_row)(D),
                  _layer_spec(gpre, l), _layer_spec(gpost, l), _layer_spec(up, l),
                  _layer_spec(dw, l), _layer_spec(db, l), _layer_spec(down, l)],
        out_specs=out_spec,
        out_shape=out_shape,
        scratch_shapes=[pltpu.VMEM((SEG, D), F32), pltpu.VMEM((SEG, D), F32)]
        + ([_slab_scratch(tm, D)] if time_order_out else []),
        compiler_params=_params("arbitrary", "arbitrary"),
        name="conv_ffn",
    )(x, x, x, mods, mods, mods, gpre, gpost, up, dw, db, down)


def _seg_interleave(a):
    *lead, T, W = a.shape
    return a.reshape(*lead, SEG, T // SEG, W).swapaxes(-3, -2).reshape(*lead, T, W)


def _rope_tables(n):
    rows = n // GRID_W
    row = jnp.repeat(jnp.arange(rows), GRID_W).astype(F32)
    col = jnp.tile(jnp.arange(GRID_W), rows).astype(F32)
    n_freq = HEAD_DIM // 4
    freq = ROPE_THETA ** (-jnp.arange(n_freq, dtype=F32) / n_freq)
    ang = jnp.concatenate([row[:, None] * freq, col[:, None] * freq], axis=-1)
    cos, sin = jnp.cos(ang), jnp.sin(ang)
    cos_t = jnp.tile(jnp.concatenate([cos, cos], axis=-1), (1, N_Q_HEADS))
    sin_t = jnp.tile(jnp.concatenate([-sin, sin], axis=-1), (1, N_Q_HEADS))
    return cos_t, sin_t


def _tile_plan(n_tokens):
    return min(n_tokens, 512), min(n_tokens, 1024), min(n_tokens, 512)


def _block_diag(w):
    *lead, nb, bi, bj = w.shape
    eye = jnp.eye(nb, dtype=w.dtype)
    return jnp.einsum('...hij,hg->...higj', w, eye).reshape(*lead, nb * bi, nb * bj)


def kernel(x, c, ctx, c_ctx, w_mod, b_mod, norm_pre_mix, norm_post_mix, norm_pre_ffn, norm_post_ffn,
           w_in, q_norm, k_norm, w_attn_out, conv_dw, conv_dw_b, conv_ln_g, conv_ln_b, w_conv_out,
           rnn_conv_w, rnn_conv_b, rnn_wa, rnn_ba, rnn_wx, rnn_bx, rnn_lambda, w_rnn_out, w_out,
           ffn_up, ffn_dw, ffn_dw_b, ffn_down):
    B, N, D = x.shape
    M = ctx.shape[1]
    L = w_mod.shape[0]
    tm_lat, tq_lat, chunk_lat = _tile_plan(N)
    tm_ctx, _, chunk_ctx = _tile_plan(M)

    mod_rows = 2 * SUBLANES
    cc = jnp.concatenate([c, c_ctx[None, :], jnp.zeros((mod_rows - B - 1, D), F32)], axis=0)
    mods = _mod_call(cc, w_mod, b_mod[:, None, :]).reshape(L, mod_rows, N_MOD, 1, D)

    cos_t, sin_t = (_seg_interleave(t) for t in _rope_tables(N))
    head_id = jnp.arange(D_Q) // HEAD_DIM
    bd = (head_id[:, None] == head_id[None, :]).astype(BF16)
    h0 = jnp.zeros((B, 2, SEG, D_RNN), F32)

    vec = lambda a: a[:, None, :]
    w_proj = w_gate = w_in.astype(BF16)
    qg = vec(jnp.tile(q_norm, (1, N_Q_HEADS)))
    kg = vec(jnp.tile(k_norm, (1, N_KV_HEADS)))
    rnn_w = (0.5 * jnp.concatenate([_block_diag(rnn_wa), _block_diag(rnn_wx)], axis=-1)).astype(BF16)
    rnn_b = 0.5 * jnp.concatenate([rnn_ba, rnn_bx], axis=-1)[:, :, None, :]
    rnn_cb = rnn_conv_b[:, :, None, :]
    lam = rnn_lambda[:, :, None, :]
    wao, wco, wro, wout = (w_attn_out.astype(BF16), w_conv_out.astype(BF16), w_rnn_out.astype(BF16),
                           w_out.astype(BF16))
    up, down = ffn_up.astype(BF16), ffn_down.astype(BF16)
    gpre_mix, gpost_mix, gpre_ffn, gpost_ffn = (vec(norm_pre_mix), vec(norm_post_mix), vec(norm_pre_ffn),
                                                vec(norm_post_ffn))
    cdb, clg, clb, fdb = vec(conv_dw_b), vec(conv_ln_g), vec(conv_ln_b), vec(ffn_dw_b)

    x = x.reshape(B, SEG, N // SEG, D)
    ctx = ctx.reshape(B, SEG, M // SEG, D)
    for l in range(L):
        need_ctx = l < L - 1
        if not need_ctx and ctx.ndim == 3 and B % 2 == 0:
            pair = _inproj_call(ctx.reshape(B // 2, 2 * M, D), mods, l, B, gpre_mix, w_proj, qg, kg, bd,
                                None, None, tm=_tile_plan(2 * M)[0], keys_only=True)
            ctx_out = tuple(a.reshape(B, M, a.shape[-1]) for a in pair)
        else:
            ctx_out = _inproj_call(ctx, mods, l, B, gpre_mix, w_proj, qg, kg, bd, None, None, tm=tm_ctx,
                                   keys_only=not need_ctx)
        if need_ctx:
            hcx, qc, kc, vc, uc, zxc, zrc = ctx_out
            rnn_c, hc_last = _rnn_call(zxc, zrc, h0, l, rnn_conv_w, rnn_cb, rnn_w, rnn_b, lam, chunk=chunk_ctx)
        else:
            kc, vc, zxc = ctx_out
            hc_last = _rnn_call(zxc, None, h0, l, rnn_conv_w, rnn_cb, rnn_w, rnn_b, lam, chunk=chunk_ctx)

        hx, q, k, v, u, zx, zr = _inproj_call(x, mods, l, None, gpre_mix, w_proj, qg, kg, bd, cos_t, sin_t,
                                              tm=tm_lat)
        attn = _attn_call(q, [(kc, vc), (k, v)], tq=tq_lat)
        rnn, _ = _rnn_call(zx, zr, hc_last, l, rnn_conv_w, rnn_cb, rnn_w, rnn_b, lam, chunk=chunk_lat)
        x = _merge_call(x, hx, mods, l, None, gpost_mix, w_gate, attn, u, conv_dw, cdb, clg, clb,
                        rnn, wao, wco, wro, wout, tm=tm_lat)
        x = _ffn_call(x, mods, l, None, gpre_ffn, gpost_ffn, up, ffn_dw, fdb, down, tm=tm_lat,
                      time_order_out=not need_ctx)

        if need_ctx:
            attn_c = _attn_call(qc, [(kc, vc)], tq=tm_ctx)
            ctx = _merge_call(ctx, hcx, mods, l, B, gpost_mix, w_gate, attn_c, uc, conv_dw, cdb, clg, clb,
                              rnn_c, wao, wco, wro, wout, tm=tm_ctx)
            ctx = _ffn_call(ctx, mods, l, B, gpre_ffn, gpost_ffn, up, ffn_dw, fdb, down, tm=tm_ctx)
    return x.reshape(B, N, D)
```
